```python
import math
import jax
import jax.numpy as jnp
from jax import lax
import numpy as np

D_MODEL = 1024
BATCH = 8
SEQ = 4096
DEPTH = 1

GRID_W = 64
CTX_LEN = 256
EPS = 1e-6

M_HEADS = 8
M_DQK = 64
M_DV = 128
M_QK_W = M_HEADS * M_DQK
M_V_W = M_HEADS * M_DV
M_CHUNK = 128
CONV_K = 3

S5_WIDTH = 512
S5_GROUP = 16
S5_GROUPS = S5_WIDTH // S5_GROUP
S5_STATE = 64

N_EXPERTS = 32
TOP_K = 4
D_FF = D_MODEL
SWIGLU_LIMIT = 7.0
SWIGLU_ALPHA = 1.702

OFF_QK = 0
OFF_V = OFF_QK + 2 * M_QK_W
OFF_IF = OFF_V + M_V_W
N_IF = 2 * 2 * M_HEADS
OFF_U = OFF_IF + N_IF
STATE_COLS = OFF_U + S5_WIDTH
OFF_O = STATE_COLS
OFF_G = OFF_O + M_V_W
IN_COLS = OFF_G + 2 * D_MODEL
N_MOD = 6

kernel_name = 'hybrid_mlstm_s5_moe_diffusion_block'


def rmsnorm(x, g):
    xf = x.astype(jnp.float32)
    y = xf * lax.rsqrt(jnp.mean(xf * xf, axis=-1, keepdims=True) + EPS)
    return (y * g.astype(jnp.float32)).astype(x.dtype)


def modulate(h, shift, scale):
    return h * (1.0 + scale) + shift


def conv_ctx(x, w):
    length = x.shape[1]
    w = w.astype(x.dtype)
    xp = jnp.pad(x, ((0, 0), (1, 1), (0, 0)))
    return xp[:, 0:length] * w[1, 0] + xp[:, 1:length + 1] * w[1, 1] + xp[:, 2:length + 2] * w[1, 2]


def conv_lat(x, w):
    bsz, length, ch = x.shape
    rows = length // GRID_W
    xg = x.reshape(bsz, rows, GRID_W, ch)
    y = lax.conv_general_dilated(xg, w.astype(x.dtype).reshape(CONV_K, CONV_K, 1, ch), (1, 1), 'SAME',
                                 dimension_numbers=('NHWC', 'HWIO', 'NHWC'), feature_group_count=ch)
    return y.reshape(bsz, length, ch)


def split_heads(a, dh):
    bsz, length, _ = a.shape
    return a.reshape(bsz, length, -1, dh).transpose(0, 2, 1, 3)


def mlstm_chunkwise(q, k, v, li, lf, state, return_h):
    bsz, nh, length, _ = q.shape
    nc = length // M_CHUNK

    def chunks(a):
        return jnp.moveaxis(a.reshape(bsz, nh, nc, M_CHUNK, *a.shape[3:]), 2, 0)

    lower = jnp.tril(jnp.ones((M_CHUNK, M_CHUNK), dtype=bool))

    def step(carry, xs):
        cm, nv, m = carry
        qc, kc, vc, lic, lfc = xs
        b = jnp.cumsum(lfc, axis=-1)
        b_end = b[..., -1]
        w_end = b_end[..., None] - b + lic
        m_new = jnp.maximum(b_end + m, jnp.max(w_end, axis=-1))
        dec = jnp.exp(b_end + m - m_new)
        we = jnp.exp(w_end - m_new[..., None])
        c_new = dec[..., None, None] * cm + jnp.einsum('bhs,bhsv,bhsk->bhvk', we, vc, kc)
        n_new = dec[..., None] * nv + jnp.einsum('bhs,bhsk->bhk', we, kc)
        if not return_h:
            return (c_new, n_new, m_new), None
        logd = jnp.where(lower, b[..., :, None] - b[..., None, :] + lic[..., None, :], -jnp.inf)
        inter = b + m[..., None]
        mt = jnp.maximum(inter, jnp.max(logd, axis=-1))
        s = jnp.einsum('bhtk,bhsk->bhts', qc, kc) * jnp.exp(logd - mt[..., None])
        ie = jnp.exp(inter - mt)
        num = ie[..., None] * jnp.einsum('bhvk,bhtk->bhtv', cm, qc) + jnp.einsum('bhts,bhsv->bhtv', s, vc)
        den = ie * jnp.einsum('bhk,bhtk->bht', nv, qc) + jnp.sum(s, axis=-1)
        h = num / jnp.maximum(jnp.abs(den), jnp.exp(-mt))[..., None]
        return (c_new, n_new, m_new), h

    state, hs = lax.scan(step, state, (chunks(q), chunks(k), chunks(v), chunks(li), chunks(lf)))
    if not return_h:
        return state, None
    return state, jnp.moveaxis(hs, 0, 2).reshape(bsz, nh, length, -1)


def mlstm_dir(q, k, v, li, lf, state, reverse, return_h):
    if reverse:
        q, k, v, li, lf = (jnp.flip(a, axis=2) for a in (q, k, v, li, lf))
    state, h = mlstm_chunkwise(q, k, v, li, lf, state, return_h)
    if reverse and return_h:
        h = jnp.flip(h, axis=2)
    return state, h


def s5_discretise(a_re, a_im, log_dt, b_re, b_im):
    f32 = jnp.float32
    a_re, a_im, b_re, b_im = (t.astype(f32) for t in (a_re, a_im, b_re, b_im))
    dt = jnp.exp(log_dt.astype(f32))[:, None]
    mag = jnp.exp(dt * a_re)
    ang = dt * a_im
    ab_re = mag * jnp.cos(ang)
    ab_im = mag * jnp.sin(ang)
    den = a_re * a_re + a_im * a_im
    xr = ab_re - 1.0
    cf_re = (xr * a_re + ab_im * a_im) / den
    cf_im = (ab_im * a_re - xr * a_im) / den
    bb_re = cf_re[..., None] * b_re - cf_im[..., None] * b_im
    bb_im = cf_re[..., None] * b_im + cf_im[..., None] * b_re
    return ab_re, ab_im, bb_re, bb_im


def _complex_affine_combine(e1, e2):
    a1r, a1i, b1r, b1i = e1
    a2r, a2i, b2r, b2i = e2
    return (a2r * a1r - a2i * a1i,
            a2r * a1i + a2i * a1r,
            a2r * b1r - a2i * b1i + b2r,
            a2r * b1i + a2i * b1r + b2i)


def s5_scan(bu_re, bu_im, ab_re, ab_im, h0, reverse):
    h0_re, h0_im = h0
    first = -1 if reverse else 0
    bu_re = bu_re.at[:, first].add(ab_re * h0_re - ab_im * h0_im)
    bu_im = bu_im.at[:, first].add(ab_re * h0_im + ab_im * h0_re)
    shape = bu_re.shape
    elems = (jnp.broadcast_to(ab_re, shape), jnp.broadcast_to(ab_im, shape), bu_re, bu_im)
    _, _, hr, hi = lax.associative_scan(_complex_affine_combine, elems, reverse=reverse, axis=1)
    last = 0 if reverse else -1
    return hr, hi, (hr[:, last], hi[:, last])


def mixer_core(proj, conv_fn, conv_w, b_if, disc, c_re, c_im, d_skip, m_state, s_state, return_out):
    f32 = jnp.float32
    bsz, length, _ = proj.shape
    qk = jax.nn.silu(conv_fn(proj[..., OFF_QK:OFF_V], conv_w)).astype(f32)
    q = split_heads(qk[..., :M_QK_W], M_DQK) * (M_DQK ** -0.5)
    k = split_heads(qk[..., M_QK_W:], M_DQK)
    v = split_heads(proj[..., OFF_V:OFF_IF].astype(f32), M_DV)
    gif = (proj[..., OFF_IF:OFF_U].reshape(bsz, length, 2, 2, M_HEADS) + b_if).astype(f32)
    gif = jnp.transpose(gif, (2, 3, 0, 4, 1))
    u = proj[..., OFF_U:STATE_COLS].astype(f32).reshape(bsz, length, S5_GROUPS, S5_GROUP)
    m_final, s_final, h_dirs, y_dirs = [], [], [], []
    for d in range(2):
        rev = d == 1
        st, h = mlstm_dir(q, k, v, gif[d, 0], jax.nn.log_sigmoid(gif[d, 1]), m_state[d], rev, return_out)
        ab_re, ab_im, bb_re, bb_im = disc[d]
        bu_re = jnp.einsum('blgc,gpc->blgp', u, bb_re)
        bu_im = jnp.einsum('blgc,gpc->blgp', u, bb_im)
        hr, hi, fin = s5_scan(bu_re, bu_im, ab_re, ab_im, s_state[d], rev)
        m_final.append(st)
        s_final.append(fin)
        if return_out:
            h_dirs.append(h)
            y_dirs.append(jnp.einsum('blgp,gcp->blgc', hr, c_re[d].astype(f32))
                          - jnp.einsum('blgp,gcp->blgc', hi, c_im[d].astype(f32)))
    if not return_out:
        return m_final, s_final, None, None
    h_m = h_dirs[0] + h_dirs[1]
    y_s = y_dirs[0] + y_dirs[1] + d_skip.astype(f32).reshape(S5_GROUPS, S5_GROUP) * u
    return m_final, s_final, h_m, y_s.reshape(bsz, length, S5_WIDTH)


def mixer_out(proj, h_m, y_s, g_mh, w_glu, b_glu, w_a, w_b, b_gate, w_o):
    dt = proj.dtype
    f32 = jnp.float32
    bsz, length, _ = proj.shape
    hn = h_m * lax.rsqrt(jnp.mean(h_m * h_m, axis=-1, keepdims=True) + EPS)
    hn = hn.transpose(0, 2, 1, 3).reshape(bsz, length, M_V_W) * g_mh.astype(f32)
    o_gate = jax.nn.sigmoid(proj[..., OFF_O:OFF_G].astype(f32))
    y_a = (hn * o_gate).astype(dt) @ w_a
    ys = jax.nn.gelu(y_s).astype(dt)
    ys = ys * jax.nn.sigmoid(ys @ w_glu + b_glu)
    y_b = ys @ w_b
    gates = jax.nn.sigmoid(proj[..., OFF_G:IN_COLS] + b_gate)
    merged = gates[..., :D_MODEL] * y_a + gates[..., D_MODEL:] * y_b
    return merged @ w_o


def moe(h, w_router, b_router, w_e_in, b_e_in, w_e_out, b_e_out):
    logits = (h @ w_router + b_router).astype(jnp.float32)
    top_v, top_i = lax.top_k(logits, TOP_K)
    probs = jax.nn.softmax(top_v, axis=-1)
    gates = jnp.einsum('nk,nke->ne', probs, jax.nn.one_hot(top_i, N_EXPERTS, dtype=jnp.float32)).astype(h.dtype)
    out = jnp.zeros_like(h)
    for e in range(N_EXPERTS):
        z = h @ w_e_in[e] + b_e_in[e]
        glu = jnp.minimum(z[:, :D_FF], SWIGLU_LIMIT)
        lin = jnp.clip(z[:, D_FF:], -SWIGLU_LIMIT, SWIGLU_LIMIT)
        act = glu * jax.nn.sigmoid(SWIGLU_ALPHA * glu) * (lin + 1.0)
        out = out + gates[:, e:e + 1] * (act @ w_e_out[e] + b_e_out[e])
    return out


def setup_inputs(seed: int = 0) -> dict:
    key = jax.random.key(seed)
    ks = jax.random.split(key, 40)
    f32 = jnp.float32
    D = D_MODEL

    def nrm(k, shape, scale):
        return jax.random.normal(k, shape, f32) * scale

    x = nrm(ks[0], (BATCH, SEQ, D), 1.0)
    c = nrm(ks[1], (BATCH, D), 1.0)
    ctx = nrm(ks[2], (BATCH, CTX_LEN, D), 1.0)
    c_ctx = nrm(ks[3], (D,), 1.0)
    w_ada = nrm(ks[4], (DEPTH, D, N_MOD * D), 0.5 * D ** -0.5)
    b_ada = nrm(ks[5], (DEPTH, N_MOD * D), 0.02)
    g_norm1 = 1.0 + nrm(ks[6], (DEPTH, D), 0.02)
    g_norm2 = 1.0 + nrm(ks[7], (DEPTH, D), 0.02)
    w_in = nrm(ks[8], (DEPTH, D, IN_COLS), D ** -0.5)
    w_conv_qk = nrm(ks[9], (DEPTH, CONV_K, CONV_K, 2 * M_QK_W), 1.0 / CONV_K)
    b_i = nrm(ks[10], (DEPTH, 2, 1, M_HEADS), 0.1)
    b_f = jnp.linspace(3.0, 6.0, M_HEADS, dtype=f32) + nrm(ks[11], (DEPTH, 2, 1, M_HEADS), 0.1)
    b_ifgate = jnp.concatenate([b_i, b_f], axis=2)
    g_mh = 1.0 + nrm(ks[12], (DEPTH, M_V_W), 0.02)
    w_branch_m = nrm(ks[13], (DEPTH, M_V_W, D), M_V_W ** -0.5)
    s5_a_re = -0.5 * jnp.exp(nrm(ks[14], (DEPTH, 2, S5_GROUPS, S5_STATE), 0.02))
    s5_a_im = math.pi * jnp.arange(S5_STATE, dtype=f32) + nrm(ks[15], (DEPTH, 2, S5_GROUPS, S5_STATE), 0.01)
    s5_log_dt = jax.random.uniform(ks[16], (DEPTH, 2, S5_GROUPS), f32, math.log(1e-3), math.log(1e-1))
    s5_b_re = nrm(ks[17], (DEPTH, 2, S5_GROUPS, S5_STATE, S5_GROUP), (2 * S5_GROUP) ** -0.5)
    s5_b_im = nrm(ks[18], (DEPTH, 2, S5_GROUPS, S5_STATE, S5_GROUP), (2 * S5_GROUP) ** -0.5)
    s5_c_re = nrm(ks[19], (DEPTH, 2, S5_GROUPS, S5_GROUP, S5_STATE), 0.5)
    s5_c_im = nrm(ks[20], (DEPTH, 2, S5_GROUPS, S5_GROUP, S5_STATE), 0.5)
    s5_d = nrm(ks[21], (DEPTH, S5_WIDTH), 1.0)
    w_glu = nrm(ks[22], (DEPTH, S5_WIDTH, S5_WIDTH), S5_WIDTH ** -0.5)
    b_glu = nrm(ks[23], (DEPTH, S5_WIDTH), 0.02)
    w_branch_s = nrm(ks[24], (DEPTH, S5_WIDTH, D), S5_WIDTH ** -0.5)
    b_merge_gate = nrm(ks[25], (DEPTH, 2 * D), 0.02)
    w_o = nrm(ks[26], (DEPTH, D, D), D ** -0.5)
    w_router = nrm(ks[27], (DEPTH, D, N_EXPERTS), D ** -0.5)
    b_router = nrm(ks[28], (DEPTH, N_EXPERTS), 0.01)
    w_e_in = nrm(ks[29], (DEPTH, N_EXPERTS, D, 2 * D_FF), D ** -0.5)
    b_e_in = nrm(ks[30], (DEPTH, N_EXPERTS, 2 * D_FF), 0.01)
    w_e_out = nrm(ks[31], (DEPTH, N_EXPERTS, D_FF, D), D_FF ** -0.5)
    b_e_out = nrm(ks[32], (DEPTH, N_EXPERTS, D), 0.01)
    g_final = 1.0 + nrm(ks[33], (D,), 0.02)
    return {'x': x, 'c': c, 'ctx': ctx, 'c_ctx': c_ctx, 'w_ada': w_ada, 'b_ada': b_ada,
            'g_norm1': g_norm1, 'g_norm2': g_norm2, 'w_in': w_in, 'w_conv_qk': w_conv_qk,
            'b_ifgate': b_ifgate, 'g_mh': g_mh, 'w_branch_m': w_branch_m,
            's5_a_re': s5_a_re, 's5_a_im': s5_a_im, 's5_log_dt': s5_log_dt,
            's5_b_re': s5_b_re, 's5_b_im': s5_b_im, 's5_c_re': s5_c_re, 's5_c_im': s5_c_im,
            's5_d': s5_d, 'w_glu': w_glu, 'b_glu': b_glu, 'w_branch_s': w_branch_s,
            'b_merge_gate': b_merge_gate, 'w_o': w_o, 'w_router': w_router, 'b_router': b_router,
            'w_e_in': w_e_in, 'b_e_in': b_e_in, 'w_e_out': w_e_out, 'b_e_out': b_e_out,
            'g_final': g_final}


def reference(x, c, ctx, c_ctx, w_ada, b_ada, g_norm1, g_norm2, w_in, w_conv_qk, b_ifgate, g_mh,
              w_branch_m, s5_a_re, s5_a_im, s5_log_dt, s5_b_re, s5_b_im, s5_c_re, s5_c_im, s5_d,
              w_glu, b_glu, w_branch_s, b_merge_gate, w_o, w_router, b_router, w_e_in, b_e_in,
              w_e_out, b_e_out, g_final):
    f32 = jnp.float32
    D = D_MODEL
    bsz = x.shape[0]
    m_zero = (jnp.zeros((bsz, M_HEADS, M_DV, M_DQK), f32), jnp.zeros((bsz, M_HEADS, M_DQK), f32),
              jnp.zeros((bsz, M_HEADS), f32))
    s_zero = (jnp.zeros((bsz, S5_GROUPS, S5_STATE), f32), jnp.zeros((bsz, S5_GROUPS, S5_STATE), f32))
    for l in range(DEPTH):
        last = l == DEPTH - 1
        mod = jax.nn.silu(c) @ w_ada[l] + b_ada[l]
        sh1, sc1, gt1, sh2, sc2, gt2 = jnp.split(mod[:, None, :], N_MOD, axis=-1)
        n_mod_c = 2 if last else N_MOD
        mod_c = jnp.split(jax.nn.silu(c_ctx) @ w_ada[l][:, :n_mod_c * D] + b_ada[l][:n_mod_c * D], n_mod_c)
        h_l = modulate(rmsnorm(x, g_norm1[l]), sh1, sc1)
        h_c = modulate(rmsnorm(ctx, g_norm1[l]), mod_c[0], mod_c[1])
        n_cols_c = STATE_COLS if last else IN_COLS
        proj_c = h_c @ w_in[l][:, :n_cols_c]
        proj_l = h_l @ w_in[l]
        disc = [s5_discretise(s5_a_re[l, d], s5_a_im[l, d], s5_log_dt[l, d], s5_b_re[l, d], s5_b_im[l, d])
                for d in range(2)]
        m_ctx, s_ctx, hm_c, ys_c = mixer_core(proj_c, conv_ctx, w_conv_qk[l], b_ifgate[l], disc,
                                              s5_c_re[l], s5_c_im[l], s5_d[l],
                                              [m_zero, m_zero], [s_zero, s_zero], not last)
        _, _, hm_l, ys_l = mixer_core(proj_l, conv_lat, w_conv_qk[l], b_ifgate[l], disc,
                                      s5_c_re[l], s5_c_im[l], s5_d[l], m_ctx, s_ctx, True)
        x = x + gt1 * mixer_out(proj_l, hm_l, ys_l, g_mh[l], w_glu[l], b_glu[l], w_branch_m[l],
                                w_branch_s[l], b_merge_gate[l], w_o[l])
        h2 = modulate(rmsnorm(x, g_norm2[l]), sh2, sc2)
        x = x + gt2 * moe(h2.reshape(-1, D), w_router[l], b_router[l], w_e_in[l], b_e_in[l],
                          w_e_out[l], b_e_out[l]).reshape(x.shape)
        if not last:
            ctx = ctx + mod_c[2] * mixer_out(proj_c, hm_c, ys_c, g_mh[l], w_glu[l], b_glu[l], w_branch_m[l],
                                             w_branch_s[l], b_merge_gate[l], w_o[l])
            h2c = modulate(rmsnorm(ctx, g_norm2[l]), mod_c[3], mod_c[4])
            ctx = ctx + mod_c[5] * moe(h2c.reshape(-1, D), w_router[l], b_router[l], w_e_in[l], b_e_in[l],
                                       w_e_out[l], b_e_out[l]).reshape(ctx.shape)
    return rmsnorm(x, g_final)
```

```python
import functools
import math

import jax
import jax.numpy as jnp
from jax import lax
from jax.experimental import pallas as pl
from jax.experimental.pallas import tpu as pltpu

F32 = jnp.float32
BF16 = jnp.bfloat16
EPS = 1e-6

N_HEADS = 8
D_QK = 64
D_V = 128
M_CHUNK = 128
GRID_W = 64
S5_GROUPS = 32
S5_GROUP = 16
S5_STATE = 64
S5_CHUNK = 16
N_EXPERTS = 32
TOP_K = 4
SWIGLU_LIMIT = 7.0
SWIGLU_ALPHA = 1.702

V7X_VMEM_BYTES = 64 * 1024 * 1024
_VMEM_CAP = V7X_VMEM_BYTES - 8 * 1024 * 1024


def _cparams(n_axes, vmem_bytes):
    limit = int(min(_VMEM_CAP, max(32 * 1024 * 1024, vmem_bytes)))
    return pltpu.CompilerParams(dimension_semantics=("arbitrary",) * n_axes, vmem_limit_bytes=limit)


def _silu(x):
    return x * jax.nn.sigmoid(x)


def _norm_mod(x, g, scale, shift):
    ms = jnp.mean(x * x, axis=-1, keepdims=True)
    return (x * lax.rsqrt(ms + EPS) * g) * (1.0 + scale) + shift


def _split3(x):
    hi = x.astype(BF16)
    r1 = x - hi.astype(F32)
    mid = r1.astype(BF16)
    lo = (r1 - mid.astype(F32)).astype(BF16)
    return hi, mid, lo


def _dot_nt(a, b):
    return lax.dot_general(a, b, (((1,), (1,)), ((), ())), preferred_element_type=F32)


def _dot_tn(a, b):
    return lax.dot_general(a, b, (((0,), (0,)), ((), ())), preferred_element_type=F32)


def _ada_body(c_ref, w_ref, b_ref, o_ref):
    s = _silu(c_ref[...])
    o_ref[...] = jnp.dot(s, w_ref[...], preferred_element_type=F32,
                         precision=lax.Precision.HIGHEST) + b_ref[...]


def _ada(c_rows, w_ada, b_ada):
    rows, d = c_rows.shape
    n = w_ada.shape[1]
    tn = 1024
    return pl.pallas_call(
        _ada_body,
        grid=(n // tn,),
        in_specs=[pl.BlockSpec((rows, d), lambda j: (0, 0)),
                  pl.BlockSpec((d, tn), lambda j: (0, j)),
                  pl.BlockSpec((1, tn), lambda j: (0, j))],
        out_specs=pl.BlockSpec((rows, tn), lambda j: (0, j)),
        out_shape=jax.ShapeDtypeStruct((rows, n), F32),
        compiler_params=_cparams(1, 4 * d * tn * 4),
        name="ada",
    )(c_rows, w_ada, b_ada.reshape(1, n))


def _proj_body(x_ref, g_ref, sc_ref, sh_ref, w_ref, bif_ref, qk_ref, v_ref, u_ref, gif_ref):
    h = _norm_mod(x_ref[0], g_ref[...], sc_ref[0], sh_ref[0]).astype(BF16)
    n_qk = qk_ref.shape[-1]
    n_v = v_ref.shape[-1]
    n_u = u_ref.shape[-1]
    qk_ref[0] = jnp.dot(h, w_ref[:, :n_qk], preferred_element_type=F32).astype(BF16)
    v_ref[0] = jnp.dot(h, w_ref[:, n_qk:n_qk + n_v], preferred_element_type=F32).astype(BF16)
    r = jnp.dot(h, w_ref[:, n_qk + n_v:], preferred_element_type=F32)
    u_ref[0] = r[:, :n_u].astype(BF16)
    gif_ref[0] = r[:, n_u:] + bif_ref[...]


def _proj(x, g, scale, shift, w_state, b_if, n_qk, n_v, n_u, tm):
    bsz, length, d = x.shape
    n_if = b_if.shape[-1]
    per_batch = scale.shape[0] == bsz
    mod_map = (lambda b, i: (b, 0, 0)) if per_batch else (lambda b, i: (0, 0, 0))
    cols = w_state.shape[1]
    vmem = 2 * (tm * d * 4 + d * cols * 2 + tm * (n_qk + n_v + n_u) * 2 + tm * 128 * 4) + 6 * tm * d * 4
    return pl.pallas_call(
        _proj_body,
        grid=(bsz, length // tm),
        in_specs=[pl.BlockSpec((1, tm, d), lambda b, i: (b, i, 0)),
                  pl.BlockSpec((1, d), lambda b, i: (0, 0)),
                  pl.BlockSpec((1, 1, d), mod_map),
                  pl.BlockSpec((1, 1, d), mod_map),
                  pl.BlockSpec((d, cols), lambda b, i: (0, 0)),
                  pl.BlockSpec((1, n_if), lambda b, i: (0, 0))],
        out_specs=[pl.BlockSpec((1, tm, n_qk), lambda b, i: (b, i, 0)),
                   pl.BlockSpec((1, tm, n_v), lambda b, i: (b, i, 0)),
                   pl.BlockSpec((1, tm, n_u), lambda b, i: (b, i, 0)),
                   pl.BlockSpec((1, tm, n_if), lambda b, i: (b, i, 0))],
        out_shape=[jax.ShapeDtypeStruct((bsz, length, n_qk), BF16),
                   jax.ShapeDtypeStruct((bsz, length, n_v), BF16),
                   jax.ShapeDtypeStruct((bsz, length, n_u), BF16),
                   jax.ShapeDtypeStruct((bsz, length, n_if), F32)],
        compiler_params=_cparams(2, vmem),
        name="proj",
    )(x, g.reshape(1, d), scale, shift, w_state, b_if.reshape(1, n_if))


def _conv_body(main_ref, prev_ref, next_ref, w_ref, scale_ref, o_ref, *, width):
    i = pl.program_id(1)
    last = pl.num_programs(1) - 1
    t = main_ref.shape[1]
    n = t + 2 * width
    main = main_ref[0].astype(F32)
    prev = jnp.where(i > 0, prev_ref[0].astype(F32), 0.0)
    nxt = jnp.where(i < last, next_ref[0].astype(F32), 0.0)
    ext = jnp.concatenate([prev, main, nxt], axis=0)
    col = lax.broadcasted_iota(jnp.int32, (t, 1), 0) % width
    acc = None
    for dx in (-1, 0, 1):
        shifted = ext if dx == 0 else pltpu.roll(ext, (-dx) % n, axis=0)
        part = None
        for dy in (-1, 0, 1):
            tap = w_ref[(dy + 1) * 3 + (dx + 1):(dy + 1) * 3 + (dx + 1) + 1, :]
            term = tap * shifted[width + dy * width:width + dy * width + t]
            part = term if part is None else part + term
        if dx == -1:
            part = jnp.where(col == 0, 0.0, part)
        elif dx == 1:
            part = jnp.where(col == width - 1, 0.0, part)
        acc = part if acc is None else acc + part
    o_ref[0] = (_silu(acc) * scale_ref[...]).astype(o_ref.dtype)


def _conv(qk_pre, w9, col_scale, width, t_block, c_block):
    bsz, length, ch = qk_pre.shape
    rpb = t_block // width
    n_rows = length // width
    vmem = 4 * (t_block + 2 * width) * c_block * 2 + 12 * (t_block + 2 * width) * c_block * 4
    return pl.pallas_call(
        functools.partial(_conv_body, width=width),
        grid=(bsz, length // t_block, ch // c_block),
        in_specs=[pl.BlockSpec((1, t_block, c_block), lambda b, i, c: (b, i, c)),
                  pl.BlockSpec((1, width, c_block), lambda b, i, c: (b, jnp.maximum(i * rpb - 1, 0), c)),
                  pl.BlockSpec((1, width, c_block), lambda b, i, c: (b, jnp.minimum((i + 1) * rpb, n_rows - 1), c)),
                  pl.BlockSpec((9, c_block), lambda b, i, c: (0, c)),
                  pl.BlockSpec((1, c_block), lambda b, i, c: (0, c))],
        out_specs=pl.BlockSpec((1, t_block, c_block), lambda b, i, c: (b, i, c)),
        out_shape=jax.ShapeDtypeStruct((bsz, length, ch), BF16),
        compiler_params=_cparams(3, vmem),
        name="conv",
    )(qk_pre, qk_pre, qk_pre, w9, col_scale)


def _log_sigmoid(x):
    return jnp.minimum(x, 0.0) - jnp.log1p(jnp.exp(-jnp.abs(x)))


def _exact_dot01(a01, x, nt=False):
    out = None
    for piece in _split3(x):
        p = _dot_nt(a01, piece) if nt else jnp.dot(a01, piece, preferred_element_type=F32)
        out = p if out is None else out + p
    return out


def _mlstm_dir(q_ref, k_ref, v_ref, g_ref, st_ref, m_ref, h_ref, d):
    t = q_ref.shape[1]
    gates = g_ref[0]
    li_cols = gates[:, d * 2 * N_HEADS:d * 2 * N_HEADS + N_HEADS]
    lf_cols = _log_sigmoid(gates[:, d * 2 * N_HEADS + N_HEADS:(d + 1) * 2 * N_HEADS])
    r_idx = lax.broadcasted_iota(jnp.int32, (t, t), 0)
    c_idx = lax.broadcasted_iota(jnp.int32, (t, t), 1)
    causal = (c_idx <= r_idx) if d == 0 else (c_idx >= r_idx)
    tri = jnp.where(causal, 1.0, 0.0).astype(BF16)
    b_cols = _exact_dot01(tri, lf_cols)
    eye = jnp.where(lax.broadcasted_iota(jnp.int32, (2 * N_HEADS, 2 * N_HEADS), 0)
                    == lax.broadcasted_iota(jnp.int32, (2 * N_HEADS, 2 * N_HEADS), 1), 1.0, 0.0).astype(BF16)
    rows = _exact_dot01(eye, jnp.concatenate([b_cols, li_cols], axis=1), nt=True)
    end = t - 1 if d == 0 else 0
    ones_blk = jnp.ones((t, D_V), BF16)
    for h in range(N_HEADS):
        q = q_ref[0, :, h * D_QK:(h + 1) * D_QK]
        k = k_ref[0, :, h * D_QK:(h + 1) * D_QK]
        v = v_ref[0, :, h * D_V:(h + 1) * D_V]
        state = st_ref[0, d, h]
        m_old = m_ref[0, d, h]
        b_col = b_cols[:, h:h + 1]
        li_col = li_cols[:, h:h + 1]
        b_row = rows[h:h + 1, :]
        li_row = rows[N_HEADS + h:N_HEADS + h + 1, :]
        logd = jnp.where(causal, b_col - b_row + li_row, -jnp.inf)
        inter = b_col + m_old
        mt = jnp.maximum(inter, jnp.max(logd, axis=-1, keepdims=True))
        dmat = jnp.exp(logd - mt)
        ie = jnp.exp(inter - mt)
        kc = jnp.concatenate([k, state.astype(BF16)], axis=0)
        r = _dot_nt(q, kc)
        s = r[:, :t] * dmat
        p = jnp.dot(s.astype(BF16), jnp.concatenate([v, ones_blk], axis=1),
                    preferred_element_type=F32)
        num = ie * r[:, t:t + D_V] + p[:, :D_V]
        den = ie * r[:, t + D_V:t + D_V + 1] + p[:, D_V:D_V + 1]
        hh = num / jnp.maximum(jnp.abs(den), jnp.exp(-mt))
        h_ref[0, :, h * D_V:(h + 1) * D_V] = hh.astype(h_ref.dtype)
        b_end = b_col[end:end + 1, :]
        w_col = b_end - b_col + li_col
        m_new = jnp.maximum(b_end + m_old, jnp.max(w_col, axis=0, keepdims=True))
        dec = jnp.exp(b_end + m_old - m_new)
        we = jnp.exp(w_col - m_new)
        upd = jnp.concatenate([we * v.astype(F32), jnp.broadcast_to(we, (t, D_V))], axis=1).astype(BF16)
        st_ref[0, d, h] = dec * state + _dot_tn(upd, k)
        m_ref[0, d, h] = m_new


def _mlstm_body(qf_ref, kf_ref, vf_ref, gf_ref, qr_ref, kr_ref, vr_ref, gr_ref, st0_ref, m0_ref,
                hf_ref, hr_ref, st_ref, m_ref):
    @pl.when(pl.program_id(1) == 0)
    def _():
        st_ref[...] = st0_ref[...]
        m_ref[...] = m0_ref[...]

    _mlstm_dir(qf_ref, kf_ref, vf_ref, gf_ref, st_ref, m_ref, hf_ref, 0)
    _mlstm_dir(qr_ref, kr_ref, vr_ref, gr_ref, st_ref, m_ref, hr_ref, 1)


def _mlstm(qk, v, gif, st0, m0):
    bsz, length, _ = v.shape
    t = M_CHUNK
    nc = length // t
    hq = N_HEADS * D_QK
    hv = N_HEADS * D_V
    ng = gif.shape[-1]
    fwd = lambda b, i: (b, i, 0)
    rev = lambda b, i: (b, nc - 1 - i, 0)
    fwd_k = lambda b, i: (b, i, 1)
    rev_k = lambda b, i: (b, nc - 1 - i, 1)
    st_spec = pl.BlockSpec((1, 2, N_HEADS, 2 * D_V, D_QK), lambda b, i: (b, 0, 0, 0, 0))
    m_spec = pl.BlockSpec((1, 2, N_HEADS, 1, 1), lambda b, i: (b, 0, 0, 0, 0))
    vmem = 24 * 1024 * 1024
    return pl.pallas_call(
        _mlstm_body,
        grid=(bsz, nc),
        in_specs=[pl.BlockSpec((1, t, hq), fwd), pl.BlockSpec((1, t, hq), fwd_k),
                  pl.BlockSpec((1, t, hv), fwd), pl.BlockSpec((1, t, ng), fwd),
                  pl.BlockSpec((1, t, hq), rev), pl.BlockSpec((1, t, hq), rev_k),
                  pl.BlockSpec((1, t, hv), rev), pl.BlockSpec((1, t, ng), rev),
                  st_spec, m_spec],
        out_specs=[pl.BlockSpec((1, t, hv), fwd), pl.BlockSpec((1, t, hv), rev), st_spec, m_spec],
        out_shape=[jax.ShapeDtypeStruct((bsz, length, hv), BF16),
                   jax.ShapeDtypeStruct((bsz, length, hv), BF16),
                   jax.ShapeDtypeStruct(st0.shape, F32),
                   jax.ShapeDtypeStruct(m0.shape, F32)],
        compiler_params=_cparams(2, vmem),
        name="mlstm",
    )(qk, qk, v, gif, qk, qk, v, gif, st0, m0)


def _s5_tables(a_re, a_im, log_dt, b_re, b_im, c_re, c_im, d_skip):
    hp = lax.Precision.HIGHEST
    t = S5_CHUNK
    n_dir, groups, p = a_re.shape
    cg = b_re.shape[-1]
    dt = jnp.exp(log_dt)[..., None]

    def lam_pow(n):
        mag = jnp.exp(n * (dt * a_re)[..., None])
        ang = n * (dt * a_im)[..., None]
        return mag * jnp.cos(ang), mag * jnp.sin(ang)

    ab_re, ab_im = (z[..., 0] for z in lam_pow(jnp.ones((1,), F32)))
    den = a_re * a_re + a_im * a_im
    xr = ab_re - 1.0
    cf_re = (xr * a_re + ab_im * a_im) / den
    cf_im = (ab_im * a_re - xr * a_im) / den
    bb_re = cf_re[..., None] * b_re - cf_im[..., None] * b_im
    bb_im = cf_re[..., None] * b_im + cf_im[..., None] * b_re
    jj = jnp.arange(t + 1, dtype=F32)
    lp_re, lp_im = lam_pow(jj)

    def w_dir(d, exps):
        lr = lp_re[d][:, :, exps]
        li = lp_im[d][:, :, exps]
        wr = lr[..., None] * bb_re[d][:, :, None, :] - li[..., None] * bb_im[d][:, :, None, :]
        wi = lr[..., None] * bb_im[d][:, :, None, :] + li[..., None] * bb_re[d][:, :, None, :]
        to_rows = lambda w: jnp.transpose(w, (0, 2, 3, 1)).reshape(groups, t * cg, p)
        return to_rows(wr), to_rows(wi)

    s_idx = jnp.arange(t)
    wf_re, wf_im = w_dir(0, t - 1 - s_idx)
    wr_re, wr_im = w_dir(1, s_idx)
    w_in = jnp.concatenate([wf_re, wf_im, wr_re, wr_im], axis=-1)

    def c_dir(d, exps):
        lr = lp_re[d][:, :, exps]
        li = lp_im[d][:, :, exps]
        cr = jnp.transpose(c_re[d], (0, 2, 1))
        ci = jnp.transpose(c_im[d], (0, 2, 1))
        o_re = cr[:, :, None, :] * lr[..., None] - ci[:, :, None, :] * li[..., None]
        o_im = cr[:, :, None, :] * li[..., None] + ci[:, :, None, :] * lr[..., None]
        return o_re.reshape(groups, p, t * cg), (-o_im).reshape(groups, p, t * cg)

    cf_r, cf_i = c_dir(0, s_idx + 1)
    cr_r, cr_i = c_dir(1, t - s_idx)
    c_out = jnp.concatenate([cf_r, cf_i, cr_r, cr_i], axis=1)

    def k_dir(d):
        lr = lp_re[d][:, :, :t]
        li = lp_im[d][:, :, :t]
        clr = c_re[d][:, :, :, None] * lr[:, None] - c_im[d][:, :, :, None] * li[:, None]
        cli = c_re[d][:, :, :, None] * li[:, None] + c_im[d][:, :, :, None] * lr[:, None]
        return (jnp.einsum('gqpj,gpc->gjqc', clr, bb_re[d], precision=hp)
                - jnp.einsum('gqpj,gpc->gjqc', cli, bb_im[d], precision=hp))

    kf = k_dir(0)
    kr = k_dir(1)
    lag = s_idx[None, :] - s_idx[:, None]
    resp_f = jnp.where((lag >= 0)[None, :, :, None, None], kf[:, jnp.clip(lag, 0, t - 1)], 0.0)
    resp_r = jnp.where((lag <= 0)[None, :, :, None, None], kr[:, jnp.clip(-lag, 0, t - 1)], 0.0)
    skip = (jnp.eye(t, dtype=F32)[None, :, :, None, None] * jnp.eye(cg, dtype=F32)[None, None, None]
            * d_skip.reshape(groups, 1, 1, cg, 1))
    m_tz = jnp.transpose(resp_f + resp_r + skip, (0, 1, 4, 2, 3)).reshape(groups, t * cg, t * cg)

    n_pow = 8
    kk = (t * (2 ** jnp.arange(n_pow))).astype(F32)
    mp_re, mp_im = lam_pow(kk)
    mp_re = jnp.transpose(mp_re, (0, 1, 3, 2))
    mp_im = jnp.transpose(mp_im, (0, 1, 3, 2))
    mu_a = jnp.concatenate([mp_re[0], mp_re[0], mp_re[1], mp_re[1]], axis=-1)
    mu_b = jnp.concatenate([-mp_im[0], mp_im[0], -mp_im[1], mp_im[1]], axis=-1)
    return w_in.astype(BF16), m_tz.astype(BF16), c_out.astype(BF16), mu_a, mu_b


def _s5_body(x_ref, win_ref, mtz_ref, cout_ref, mua_ref, mub_ref, h0_ref, y_ref, hout_ref, *, n_steps):
    x = x_ref[0, 0]
    nch = x.shape[0]
    p2 = 2 * S5_STATE
    local = jnp.dot(x, win_ref[0], preferred_element_type=F32)
    h0 = h0_ref[0, 0]
    row = lax.broadcasted_iota(jnp.int32, (nch, 1), 0)
    lf = local[:, :p2]
    lr = local[:, p2:]
    zf = jnp.where(row == 0, h0[:, :p2], pltpu.roll(lf, 1, axis=0))
    zr = jnp.where(row == nch - 1, h0[:, p2:], pltpu.roll(lr, nch - 1, axis=0))
    for k in range(n_steps):
        sft = 1 << k
        a_f = mua_ref[0, k:k + 1, :p2]
        b_f = mub_ref[0, k:k + 1, :p2]
        a_r = mua_ref[0, k:k + 1, p2:]
        b_r = mub_ref[0, k:k + 1, p2:]
        pf = jnp.where(row >= sft, pltpu.roll(zf, sft, axis=0), 0.0)
        pr = jnp.where(row < nch - sft, pltpu.roll(zr, nch - sft, axis=0), 0.0)
        zf = zf + a_f * pf + b_f * pltpu.roll(pf, S5_STATE, axis=1)
        zr = zr + a_r * pr + b_r * pltpu.roll(pr, S5_STATE, axis=1)
    z = jnp.concatenate([zf, zr], axis=1)
    y = jnp.dot(x, mtz_ref[0], preferred_element_type=F32)
    y = y + jnp.dot(z.astype(BF16), cout_ref[0], preferred_element_type=F32)
    y_ref[0, 0] = jax.nn.gelu(y).astype(y_ref.dtype)
    a1 = mua_ref[0, 0:1, :]
    b1 = mub_ref[0, 0:1, :]
    zf_l = zf[nch - 1:nch, :]
    zr_l = zr[0:1, :]
    ends = jnp.concatenate([zf_l, zr_l], axis=1)
    ends_sw = jnp.concatenate([pltpu.roll(zf_l, S5_STATE, axis=1), pltpu.roll(zr_l, S5_STATE, axis=1)], axis=1)
    loc_end = jnp.concatenate([lf[nch - 1:nch, :], lr[0:1, :]], axis=1)
    hout_ref[0, 0] = a1 * ends + b1 * ends_sw + loc_end


def _s5(xg, w_in, m_tz, c_out, mu_a, mu_b, h0):
    bsz, groups, nch, tc = xg.shape
    p4 = 4 * S5_STATE
    n_steps = max(1, (nch - 1).bit_length())
    per_g = lambda g, b: (g, 0, 0)
    per_bg = lambda g, b: (b, g, 0, 0)
    return pl.pallas_call(
        functools.partial(_s5_body, n_steps=n_steps),
        grid=(groups, bsz),
        in_specs=[pl.BlockSpec((1, 1, nch, tc), per_bg),
                  pl.BlockSpec((1, tc, p4), per_g),
                  pl.BlockSpec((1, tc, tc), per_g),
                  pl.BlockSpec((1, p4, tc), per_g),
                  pl.BlockSpec((1, mu_a.shape[1], p4), per_g),
                  pl.BlockSpec((1, mu_b.shape[1], p4), per_g),
                  pl.BlockSpec((1, 1, 1, p4), per_bg)],
        out_specs=[pl.BlockSpec((1, 1, nch, tc), per_bg),
                   pl.BlockSpec((1, 1, 1, p4), per_bg)],
        out_shape=[jax.ShapeDtypeStruct((bsz, groups, nch, tc), BF16),
                   jax.ShapeDtypeStruct((bsz, groups, 1, p4), F32)],
        compiler_params=_cparams(2, 16 * 1024 * 1024),
        name="s5",
    )(xg, w_in, m_tz, c_out, mu_a, mu_b, h0)


def _mixout_body(x_ref, hf_ref, hr_ref, ys_ref, mod_ref, g1_ref, g2_ref, gmh_ref, wog_ref, wa_ref, wglu_ref,
                 bglu_ref, wb_ref, bgate_ref, wo_ref, wr_ref, br_ref, x1_ref, h2_ref, ti_ref, tp_ref):
    x = x_ref[0]
    d = x.shape[-1]
    mod = mod_ref[0]
    h = _norm_mod(x, g1_ref[...], mod[1:2], mod[0:1]).astype(BF16)
    og = jnp.dot(h, wog_ref[...], preferred_element_type=F32)
    hm = hf_ref[0].astype(F32) + hr_ref[0].astype(F32)
    heads = []
    for hd in range(N_HEADS):
        blk = hm[:, hd * D_V:(hd + 1) * D_V]
        heads.append(blk * lax.rsqrt(jnp.mean(blk * blk, axis=-1, keepdims=True) + EPS))
    hn = jnp.concatenate(heads, axis=1) * gmh_ref[...]
    y_a = jnp.dot((hn * jax.nn.sigmoid(og[:, :d])).astype(BF16), wa_ref[...], preferred_element_type=F32)
    ys = ys_ref[0]
    glu = jax.nn.sigmoid(jnp.dot(ys, wglu_ref[...], preferred_element_type=F32) + bglu_ref[...])
    y_b = jnp.dot((ys.astype(F32) * glu).astype(BF16), wb_ref[...], preferred_element_type=F32)
    gates = jax.nn.sigmoid(og[:, d:] + bgate_ref[...])
    merged = gates[:, :d] * y_a + gates[:, d:] * y_b
    x1 = x + mod[2:3] * jnp.dot(merged.astype(BF16), wo_ref[...], preferred_element_type=F32)
    x1_ref[0] = x1
    h2 = _norm_mod(x1, g2_ref[...], mod[4:5], mod[3:4]).astype(BF16)
    h2_ref[0] = h2
    logits = jnp.dot(h2, wr_ref[...], preferred_element_type=F32) + br_ref[...]
    tm, n_e = logits.shape
    e_iota = lax.broadcasted_iota(jnp.int32, (tm, n_e), 1)
    lane = lax.broadcasted_iota(jnp.int32, (tm, ti_ref.shape[-1]), 1)
    ti = jnp.zeros(lane.shape, jnp.int32)
    tv = jnp.zeros(lane.shape, F32)
    top = None
    for k in range(TOP_K):
        mx = jnp.max(logits, axis=-1, keepdims=True)
        idx = jnp.min(jnp.where(logits == mx, e_iota, n_e), axis=-1, keepdims=True)
        top = mx if top is None else top
        ti = jnp.where(lane == k, idx, ti)
        tv = jnp.where(lane == k, jnp.exp(mx - top), tv)
        logits = jnp.where(e_iota == idx, -jnp.inf, logits)
    ti_ref[0] = ti
    tp_ref[0] = tv / jnp.sum(tv, axis=-1, keepdims=True)


def _mixout(x, h_f, h_r, ys, mod, g1, g2, g_mh, w_og, w_a, w_glu, b_glu, w_b, b_gate, w_o, w_r, b_r, tm):
    bsz, length, d = x.shape
    sw = ys.shape[-1]
    n_e = w_r.shape[-1]
    tok = lambda b, i: (b, i, 0)
    const2 = lambda b, i: (0, 0)

    def wspec(w):
        return pl.BlockSpec(w.shape, const2, pipeline_mode=pl.Buffered(1))

    weights = (w_og, w_a, w_glu, w_b, w_o, w_r)
    w_bytes = sum(int(w.size) * w.dtype.itemsize for w in weights)
    vmem = w_bytes + 2 * tm * (d * 4 + 2 * d * 2 + sw * 2 + d * 4 + d * 2) + 14 * tm * d * 4
    return pl.pallas_call(
        _mixout_body,
        grid=(bsz, length // tm),
        in_specs=[pl.BlockSpec((1, tm, d), tok), pl.BlockSpec((1, tm, d), tok), pl.BlockSpec((1, tm, d), tok),
                  pl.BlockSpec((1, tm, sw), tok), pl.BlockSpec((1, 6, d), lambda b, i: (b, 0, 0)),
                  pl.BlockSpec((1, d), const2), pl.BlockSpec((1, d), const2), pl.BlockSpec((1, d), const2),
                  wspec(w_og), wspec(w_a), wspec(w_glu), pl.BlockSpec((1, sw), const2), wspec(w_b),
                  pl.BlockSpec((1, 2 * d), const2), wspec(w_o), wspec(w_r), pl.BlockSpec((1, n_e), const2)],
        out_specs=[pl.BlockSpec((1, tm, d), tok), pl.BlockSpec((1, tm, d), tok),
                   pl.BlockSpec((1, tm, 8), tok), pl.BlockSpec((1, tm, 8), tok)],
        out_shape=[jax.ShapeDtypeStruct((bsz, length, d), F32),
                   jax.ShapeDtypeStruct((bsz, length, d), BF16),
                   jax.ShapeDtypeStruct((bsz, length, 8), jnp.int32),
                   jax.ShapeDtypeStruct((bsz, length, 8), F32)],
        compiler_params=_cparams(2, vmem),
        name="mixout",
    )(x, h_f, h_r, ys, mod, g1.reshape(1, d), g2.reshape(1, d), g_mh.reshape(1, d), w_og, w_a, w_glu,
      b_glu.reshape(1, sw), w_b, b_gate.reshape(1, 2 * d), w_o, w_r, b_r.reshape(1, n_e))


def _experts_body(te_ref, nt_ref, xs_ref, p_ref, win_ref, bin_ref, wout_ref, bout_ref, y_ref, win_bf, wout_bf,
                  *, f_chunk, cast_rows):
    i = pl.program_id(0)
    e = te_ref[i]
    e_prev = te_ref[jnp.maximum(i - 1, 0)]
    d, f2 = win_bf.shape
    f = f2 // 2

    @pl.when((i == 0) | (e != e_prev))
    def _():
        def cast_in(r, carry):
            rows = pl.ds(pl.multiple_of(r * cast_rows, cast_rows), cast_rows)
            win_bf[rows, :] = win_ref[0, rows, :].astype(BF16)
            return carry

        def cast_out(r, carry):
            rows = pl.ds(pl.multiple_of(r * cast_rows, cast_rows), cast_rows)
            wout_bf[rows, :] = wout_ref[0, rows, :].astype(BF16)
            return carry

        lax.fori_loop(0, d // cast_rows, cast_in, 0)
        lax.fori_loop(0, f // cast_rows, cast_out, 0)

    @pl.when(i < nt_ref[0])
    def _():
        x = xs_ref[...]
        acc = None
        for c in range(f // f_chunk):
            lo = c * f_chunk
            zg = jnp.dot(x, win_bf[:, lo:lo + f_chunk], preferred_element_type=F32) + bin_ref[0, :, lo:lo + f_chunk]
            zl = (jnp.dot(x, win_bf[:, f + lo:f + lo + f_chunk], preferred_element_type=F32)
                  + bin_ref[0, :, f + lo:f + lo + f_chunk])
            glu = jnp.minimum(zg, SWIGLU_LIMIT)
            lin = jnp.clip(zl, -SWIGLU_LIMIT, SWIGLU_LIMIT)
            act = glu * jax.nn.sigmoid(SWIGLU_ALPHA * glu) * (lin + 1.0)
            part = jnp.dot(act.astype(BF16), wout_bf[lo:lo + f_chunk, :], preferred_element_type=F32)
            acc = part if acc is None else acc + part
        y_ref[...] = ((acc + bout_ref[0]) * p_ref[...]).astype(y_ref.dtype)


def _experts(tile_expert, n_tiles, xs, p_rows, w_e_in, b_e_in, w_e_out, b_e_out, tm):
    rows, d = xs.shape
    n_e, _, f2 = w_e_in.shape
    f = f2 // 2
    nt_max = rows // tm
    row_map = lambda i, te, nt: (jnp.minimum(i, nt[0] - 1), 0)
    exp_map = lambda i, te, nt: (te[i], 0, 0)
    vmem = 2 * (d * f2 + f * d) * 4 + (d * f2 + f * d) * 2 + 4 * tm * d * 2 + 10 * tm * d * 4
    grid_spec = pltpu.PrefetchScalarGridSpec(
        num_scalar_prefetch=2,
        grid=(nt_max,),
        in_specs=[pl.BlockSpec((tm, d), row_map), pl.BlockSpec((tm, 1), row_map),
                  pl.BlockSpec((1, d, f2), exp_map), pl.BlockSpec((1, 1, f2), exp_map),
                  pl.BlockSpec((1, f, d), exp_map), pl.BlockSpec((1, 1, d), exp_map)],
        out_specs=pl.BlockSpec((tm, d), row_map),
        scratch_shapes=[pltpu.VMEM((d, f2), BF16), pltpu.VMEM((f, d), BF16)],
    )
    return pl.pallas_call(
        functools.partial(_experts_body, f_chunk=512, cast_rows=128),
        grid_spec=grid_spec,
        out_shape=jax.ShapeDtypeStruct((rows, d), BF16),
        compiler_params=_cparams(1, vmem),
        name="experts",
    )(tile_expert, n_tiles, xs, p_rows, w_e_in, b_e_in.reshape(n_e, 1, f2), w_e_out, b_e_out.reshape(n_e, 1, d))


def _final_body(x1_ref, yk_ref, gt_ref, g_ref, o_ref):
    moe = yk_ref[0, 0].astype(F32)
    for k in range(1, yk_ref.shape[0]):
        moe = moe + yk_ref[k, 0].astype(F32)
    x2 = x1_ref[0] + gt_ref[0] * moe
    o_ref[0] = x2 * lax.rsqrt(jnp.mean(x2 * x2, axis=-1, keepdims=True) + EPS) * g_ref[...]


def _final(x1, yk, gt2, g_final, tm):
    bsz, length, d = x1.shape
    kk = yk.shape[0]
    return pl.pallas_call(
        _final_body,
        grid=(bsz, length // tm),
        in_specs=[pl.BlockSpec((1, tm, d), lambda b, i: (b, i, 0)),
                  pl.BlockSpec((kk, 1, tm, d), lambda b, i: (0, b, i, 0)),
                  pl.BlockSpec((1, 1, d), lambda b, i: (b, 0, 0)),
                  pl.BlockSpec((1, d), lambda b, i: (0, 0))],
        out_specs=pl.BlockSpec((1, tm, d), lambda b, i: (b, i, 0)),
        out_shape=jax.ShapeDtypeStruct((bsz, length, d), F32),
        compiler_params=_cparams(2, 2 * tm * d * (4 + kk * 2 + 4) + 6 * tm * d * 4),
        name="final",
    )(x1, yk, gt2, g_final.reshape(1, d))


def _routing(top_i, top_p, tm):
    n_tok, kk = top_i.shape
    n_asg = n_tok * kk
    nt_max = n_asg // tm + N_EXPERTS
    rows = nt_max * tm
    e_flat = top_i.reshape(-1)
    order = jnp.argsort(e_flat, stable=True).astype(jnp.int32)
    counts = jnp.sum(e_flat[:, None] == jnp.arange(N_EXPERTS, dtype=jnp.int32)[None, :], axis=0, dtype=jnp.int32)
    starts = jnp.cumsum(counts) - counts
    tiles = (counts + tm - 1) // tm
    tile_end = jnp.cumsum(tiles)
    tile_start = tile_end - tiles
    n_tiles = tile_end[-1]
    tile_ids = jnp.arange(nt_max, dtype=jnp.int32)
    te = jnp.searchsorted(tile_end, tile_ids, side='right').astype(jnp.int32)
    te_last = jnp.searchsorted(tile_end, n_tiles - 1, side='right').astype(jnp.int32)
    te = jnp.where(tile_ids < n_tiles, te, te_last)
    row = jnp.arange(rows, dtype=jnp.int32)
    e_row = te[row // tm]
    off = row - tile_start[e_row] * tm
    valid = (row // tm < n_tiles) & (off < counts[e_row])
    slot = order[jnp.clip(starts[e_row] + off, 0, n_asg - 1)]
    src_token = jnp.where(valid, slot // kk, 0)
    row_p = jnp.where(valid, top_p.reshape(-1)[slot], 0.0)[:, None]
    inv = jnp.zeros((n_asg,), jnp.int32).at[order].set(jnp.arange(n_asg, dtype=jnp.int32))
    pos = (tile_start[e_flat] * tm + inv - starts[e_flat]).reshape(n_tok, kk)
    return te, n_tiles.reshape(1), src_token, row_p, pos


def kernel(x, c, ctx, c_ctx, w_ada, b_ada, g_norm1, g_norm2, w_in, w_conv_qk, b_ifgate, g_mh, w_branch_m,
           s5_a_re, s5_a_im, s5_log_dt, s5_b_re, s5_b_im, s5_c_re, s5_c_im, s5_d, w_glu, b_glu, w_branch_s,
           b_merge_gate, w_o, w_router, b_router, w_e_in, b_e_in, w_e_out, b_e_out, g_final):
    bsz, length, d = x.shape
    l_ctx = ctx.shape[1]
    n_qk = 2 * N_HEADS * D_QK
    n_v = N_HEADS * D_V
    n_if = 4 * N_HEADS
    n_u = S5_GROUPS * S5_GROUP
    off_if = n_qk + n_v
    off_u = off_if + n_if
    off_o = off_u + n_u
    layer = 0

    pad_rows = -(bsz + 1) % 8
    c_rows = jnp.concatenate([c, c_ctx[None, :], jnp.zeros((pad_rows, d), F32)], axis=0)
    mod_all = _ada(c_rows, w_ada[layer], b_ada[layer])
    mod = mod_all[:bsz].reshape(bsz, 6, d)
    mod_c = mod_all[bsz, :2 * d].reshape(2, 1, 1, d)

    w_l = w_in[layer]
    w_state = jnp.concatenate([w_l[:, :off_if], w_l[:, off_u:off_o], w_l[:, off_if:off_u]], axis=1).astype(BF16)
    b_if = b_ifgate[layer].reshape(n_if)
    qk_c, v_c, u_c, gif_c = _proj(ctx, g_norm1[layer], mod_c[1], mod_c[0], w_state, b_if, n_qk, n_v, n_u, l_ctx)
    qk_l, v_l, u_l, gif_l = _proj(x, g_norm1[layer], mod[:, 1:2], mod[:, 0:1], w_state, b_if, n_qk, n_v, n_u, 512)

    w9 = w_conv_qk[layer].reshape(9, n_qk)
    col_scale = jnp.concatenate([jnp.full((1, n_qk // 2), D_QK ** -0.5, F32), jnp.ones((1, n_qk // 2), F32)], axis=1)
    qk_c = _conv(qk_c, w9, col_scale, l_ctx, l_ctx, 512)
    qk_l = _conv(qk_l, w9, col_scale, GRID_W, 512, 512)

    st0 = jnp.zeros((bsz, 2, N_HEADS, 2 * D_V, D_QK), F32)
    m0 = jnp.zeros((bsz, 2, N_HEADS, 1, 1), F32)
    _, _, st_c, m_c = _mlstm(qk_c, v_c, gif_c, st0, m0)
    h_f, h_r, _, _ = _mlstm(qk_l, v_l, gif_l, st_c, m_c)

    tables = _s5_tables(s5_a_re[layer], s5_a_im[layer], s5_log_dt[layer], s5_b_re[layer], s5_b_im[layer],
                        s5_c_re[layer], s5_c_im[layer], s5_d[layer])

    def to_chunks(u):
        n = u.shape[1] // S5_CHUNK
        u5 = u.reshape(bsz, n, S5_CHUNK, S5_GROUPS, S5_GROUP)
        return jnp.transpose(u5, (0, 3, 1, 2, 4)).reshape(bsz, S5_GROUPS, n, S5_CHUNK * S5_GROUP)

    hs0 = jnp.zeros((bsz, S5_GROUPS, 1, 4 * S5_STATE), F32)
    _, hs_c = _s5(to_chunks(u_c), *tables, hs0)
    y_g, _ = _s5(to_chunks(u_l), *tables, hs_c)
    n_ch = length // S5_CHUNK
    ys = jnp.transpose(y_g.reshape(bsz, S5_GROUPS, n_ch, S5_CHUNK, S5_GROUP), (0, 2, 3, 1, 4)).reshape(bsz, length, n_u)

    w_og = w_l[:, off_o:].astype(BF16)
    x1, h2, top_i, top_p = _mixout(
        x, h_f, h_r, ys, mod, g_norm1[layer], g_norm2[layer], g_mh[layer], w_og, w_branch_m[layer].astype(BF16),
        w_glu[layer].astype(BF16), b_glu[layer], w_branch_s[layer].astype(BF16), b_merge_gate[layer],
        w_o[layer].astype(BF16), w_router[layer].astype(BF16), b_router[layer], 512)

    tm_e = 512
    n_tok = bsz * length
    te, n_tiles, src_token, row_p, pos = _routing(top_i.reshape(n_tok, 8)[:, :TOP_K],
                                                  top_p.reshape(n_tok, 8)[:, :TOP_K], tm_e)
    xs = jnp.take(h2.reshape(n_tok, d), src_token, axis=0)
    y_rows = _experts(te, n_tiles, xs, row_p, w_e_in[layer], b_e_in[layer], w_e_out[layer], b_e_out[layer], tm_e)
    yk = jnp.take(y_rows, pos.T, axis=0).reshape(TOP_K, bsz, length, d)
    return _final(x1, yk, mod[:, 5:6], g_final, 512)
```

```python
import functools
import math

import jax
import jax.numpy as jnp
from jax import lax
from jax.experimental import pallas as pl
from jax.experimental.pallas import tpu as pltpu

F32 = jnp.float32
BF16 = jnp.bfloat16
EPS = 1e-6

N_HEADS = 8
D_QK = 64
D_V = 128
M_CHUNK = 128
GRID_W = 64
S5_GROUPS = 32
S5_GROUP = 16
S5_STATE = 64
S5_CHUNK = 16
N_EXPERTS = 32
TOP_K = 4
SWIGLU_LIMIT = 7.0
SWIGLU_ALPHA = 1.702

V7X_VMEM_BYTES = 64 * 1024 * 1024
_VMEM_CAP = V7X_VMEM_BYTES - 8 * 1024 * 1024


def _cparams(n_axes, vmem_bytes):
    limit = int(min(_VMEM_CAP, max(32 * 1024 * 1024, vmem_bytes)))
    return pltpu.CompilerParams(dimension_semantics=("arbitrary",) * n_axes, vmem_limit_bytes=limit)


def _silu(x):
    return x * jax.nn.sigmoid(x)


def _norm_mod(x, g, scale, shift):
    ms = jnp.mean(x * x, axis=-1, keepdims=True)
    return (x * lax.rsqrt(ms + EPS) * g) * (1.0 + scale) + shift


def _split3(x):
    hi = x.astype(BF16)
    r1 = x - hi.astype(F32)
    mid = r1.astype(BF16)
    lo = (r1 - mid.astype(F32)).astype(BF16)
    return hi, mid, lo


def _dot_nt(a, b):
    return lax.dot_general(a, b, (((1,), (1,)), ((), ())), preferred_element_type=F32)


def _dot_tn(a, b):
    return lax.dot_general(a, b, (((0,), (0,)), ((), ())), preferred_element_type=F32)


def _ada_body(c_ref, w_ref, b_ref, o_ref):
    s = _silu(c_ref[...])
    o_ref[...] = jnp.dot(s, w_ref[...], preferred_element_type=F32,
                         precision=lax.Precision.HIGHEST) + b_ref[...]


def _ada(c_rows, w_ada, b_ada):
    rows, d = c_rows.shape
    n = w_ada.shape[1]
    tn = 1024
    return pl.pallas_call(
        _ada_body,
        grid=(n // tn,),
        in_specs=[pl.BlockSpec((rows, d), lambda j: (0, 0)),
                  pl.BlockSpec((d, tn), lambda j: (0, j)),
                  pl.BlockSpec((1, tn), lambda j: (0, j))],
        out_specs=pl.BlockSpec((rows, tn), lambda j: (0, j)),
        out_shape=jax.ShapeDtypeStruct((rows, n), F32),
        compiler_params=_cparams(1, 4 * d * tn * 4),
        name="ada",
    )(c_rows, w_ada, b_ada.reshape(1, n))


def _proj_body(x_ref, g_ref, sc_ref, sh_ref, w_ref, bif_ref, qk_ref, v_ref, u_ref, gif_ref):
    h = _norm_mod(x_ref[0], g_ref[...], sc_ref[0], sh_ref[0]).astype(BF16)
    n_qk = qk_ref.shape[-1]
    n_v = v_ref.shape[-1]
    n_u = u_ref.shape[-1]
    qk_ref[0] = jnp.dot(h, w_ref[:, :n_qk], preferred_element_type=F32).astype(BF16)
    v_ref[0] = jnp.dot(h, w_ref[:, n_qk:n_qk + n_v], preferred_element_type=F32).astype(BF16)
    r = jnp.dot(h, w_ref[:, n_qk + n_v:], preferred_element_type=F32)
    u_ref[0] = r[:, :n_u].astype(BF16)
    gif_ref[0] = r[:, n_u:] + bif_ref[...]


def _proj(x, g, scale, shift, w_state, b_if, n_qk, n_v, n_u, tm):
    bsz, length, d = x.shape
    n_if = b_if.shape[-1]
    per_batch = scale.shape[0] == bsz
    mod_map = (lambda b, i: (b, 0, 0)) if per_batch else (lambda b, i: (0, 0, 0))
    cols = w_state.shape[1]
    vmem = 2 * (tm * d * 4 + d * cols * 2 + tm * (n_qk + n_v + n_u) * 2 + tm * 128 * 4) + 6 * tm * d * 4
    return pl.pallas_call(
        _proj_body,
        grid=(bsz, length // tm),
        in_specs=[pl.BlockSpec((1, tm, d), lambda b, i: (b, i, 0)),
                  pl.BlockSpec((1, d), lambda b, i: (0, 0)),
                  pl.BlockSpec((1, 1, d), mod_map),
                  pl.BlockSpec((1, 1, d), mod_map),
                  pl.BlockSpec((d, cols), lambda b, i: (0, 0)),
                  pl.BlockSpec((1, n_if), lambda b, i: (0, 0))],
        out_specs=[pl.BlockSpec((1, tm, n_qk), lambda b, i: (b, i, 0)),
                   pl.BlockSpec((1, tm, n_v), lambda b, i: (b, i, 0)),
                   pl.BlockSpec((1, tm, n_u), lambda b, i: (b, i, 0)),
                   pl.BlockSpec((1, tm, n_if), lambda b, i: (b, i, 0))],
        out_shape=[jax.ShapeDtypeStruct((bsz, length, n_qk), BF16),
                   jax.ShapeDtypeStruct((bsz, length, n_v), BF16),
                   jax.ShapeDtypeStruct((bsz, length, n_u), BF16),
                   jax.ShapeDtypeStruct((bsz, length, n_if), F32)],
        compiler_params=_cparams(2, vmem),
        name="proj",
    )(x, g.reshape(1, d), scale, shift, w_state, b_if.reshape(1, n_if))


def _conv_body(main_ref, prev_ref, next_ref, w_ref, scale_ref, o_ref, *, width):
    i = pl.program_id(1)
    last = pl.num_programs(1) - 1
    t = main_ref.shape[1]
    n = t + 2 * width
    main = main_ref[0].astype(F32)
    prev = jnp.where(i > 0, prev_ref[0].astype(F32), 0.0)
    nxt = jnp.where(i < last, next_ref[0].astype(F32), 0.0)
    ext = jnp.concatenate([prev, main, nxt], axis=0)
    col = lax.broadcasted_iota(jnp.int32, (t, 1), 0) % width
    acc = None
    for dx in (-1, 0, 1):
        shifted = ext if dx == 0 else pltpu.roll(ext, (-dx) % n, axis=0)
        part = None
        for dy in (-1, 0, 1):
            tap = w_ref[(dy + 1) * 3 + (dx + 1):(dy + 1) * 3 + (dx + 1) + 1, :]
            term = tap * shifted[width + dy * width:width + dy * width + t]
            part = term if part is None else part + term
        if dx == -1:
            part = jnp.where(col == 0, 0.0, part)
        elif dx == 1:
            part = jnp.where(col == width - 1, 0.0, part)
        acc = part if acc is None else acc + part
    o_ref[0] = (_silu(acc) * scale_ref[...]).astype(o_ref.dtype)


def _conv(qk_pre, w9, col_scale, width, t_block, c_block):
    bsz, length, ch = qk_pre.shape
    rpb = t_block // width
    n_rows = length // width
    vmem = 4 * (t_block + 2 * width) * c_block * 2 + 12 * (t_block + 2 * width) * c_block * 4
    return pl.pallas_call(
        functools.partial(_conv_body, width=width),
        grid=(bsz, length // t_block, ch // c_block),
        in_specs=[pl.BlockSpec((1, t_block, c_block), lambda b, i, c: (b, i, c)),
                  pl.BlockSpec((1, width, c_block), lambda b, i, c: (b, jnp.maximum(i * rpb - 1, 0), c)),
                  pl.BlockSpec((1, width, c_block), lambda b, i, c: (b, jnp.minimum((i + 1) * rpb, n_rows - 1), c)),
                  pl.BlockSpec((9, c_block), lambda b, i, c: (0, c)),
                  pl.BlockSpec((1, c_block), lambda b, i, c: (0, c))],
        out_specs=pl.BlockSpec((1, t_block, c_block), lambda b, i, c: (b, i, c)),
        out_shape=jax.ShapeDtypeStruct((bsz, length, ch), BF16),
        compiler_params=_cparams(3, vmem),
        name="conv",
    )(qk_pre, qk_pre, qk_pre, w9, col_scale)


def _log_sigmoid(x):
    return jnp.minimum(x, 0.0) - jnp.log1p(jnp.exp(-jnp.abs(x)))


def _exact_dot01(a01, x, nt=False):
    out = None
    for piece in _split3(x):
        p = _dot_nt(a01, piece) if nt else jnp.dot(a01, piece, preferred_element_type=F32)
        out = p if out is None else out + p
    return out


def _mlstm_dir(q_ref, k_ref, v_ref, g_ref, st_ref, m_ref, h_ref, d):
    t = q_ref.shape[1]
    gates = g_ref[0]
    li_cols = gates[:, d * 2 * N_HEADS:d * 2 * N_HEADS + N_HEADS]
    lf_cols = _log_sigmoid(gates[:, d * 2 * N_HEADS + N_HEADS:(d + 1) * 2 * N_HEADS])
    r_idx = lax.broadcasted_iota(jnp.int32, (t, t), 0)
    c_idx = lax.broadcasted_iota(jnp.int32, (t, t), 1)
    causal = (c_idx <= r_idx) if d == 0 else (c_idx >= r_idx)
    tri = jnp.where(causal, 1.0, 0.0).astype(BF16)
    b_cols = _exact_dot01(tri, lf_cols)
    eye = jnp.where(lax.broadcasted_iota(jnp.int32, (2 * N_HEADS, 2 * N_HEADS), 0)
                    == lax.broadcasted_iota(jnp.int32, (2 * N_HEADS, 2 * N_HEADS), 1), 1.0, 0.0).astype(BF16)
    rows = _exact_dot01(eye, jnp.concatenate([b_cols, li_cols], axis=1), nt=True)
    end = t - 1 if d == 0 else 0
    ones_blk = jnp.ones((t, D_V), BF16)
    for h in range(N_HEADS):
        q = q_ref[0, :, h * D_QK:(h + 1) * D_QK]
        k = k_ref[0, :, h * D_QK:(h + 1) * D_QK]
        v = v_ref[0, :, h * D_V:(h + 1) * D_V]
        state = st_ref[0, d, h]
        m_old = m_ref[0, d, h]
        b_col = b_cols[:, h:h + 1]
        li_col = li_cols[:, h:h + 1]
        b_row = rows[h:h + 1, :]
        li_row = rows[N_HEADS + h:N_HEADS + h + 1, :]
        logd = jnp.where(causal, b_col - b_row + li_row, -jnp.inf)
        inter = b_col + m_old
        mt = jnp.maximum(inter, jnp.max(logd, axis=-1, keepdims=True))
        dmat = jnp.exp(logd - mt)
        ie = jnp.exp(inter - mt)
        kc = jnp.concatenate([k, state.astype(BF16)], axis=0)
        r = _dot_nt(q, kc)
        s = r[:, :t] * dmat
        p = jnp.dot(s.astype(BF16), jnp.concatenate([v, ones_blk], axis=1),
                    preferred_element_type=F32)
        num = ie * r[:, t:t + D_V] + p[:, :D_V]
        den = ie * r[:, t + D_V:t + D_V + 1] + p[:, D_V:D_V + 1]
        hh = num / jnp.maximum(jnp.abs(den), jnp.exp(-mt))
        h_ref[0, :, h * D_V:(h + 1) * D_V] = hh.astype(h_ref.dtype)
        b_end = b_col[end:end + 1, :]
        w_col = b_end - b_col + li_col
        m_new = jnp.maximum(b_end + m_old, jnp.max(w_col, axis=0, keepdims=True))
        dec = jnp.exp(b_end + m_old - m_new)
        we = jnp.exp(w_col - m_new)
        upd = jnp.concatenate([we * v.astype(F32), jnp.broadcast_to(we, (t, D_V))], axis=1).astype(BF16)
        st_ref[0, d, h] = dec * state + _dot_tn(upd, k)
        m_ref[0, d, h] = m_new


def _mlstm_body(qf_ref, kf_ref, vf_ref, gf_ref, qr_ref, kr_ref, vr_ref, gr_ref, st0_ref, m0_ref,
                hf_ref, hr_ref, st_ref, m_ref):
    @pl.when(pl.program_id(1) == 0)
    def _():
        st_ref[...] = st0_ref[...]
        m_ref[...] = m0_ref[...]

    _mlstm_dir(qf_ref, kf_ref, vf_ref, gf_ref, st_ref, m_ref, hf_ref, 0)
    _mlstm_dir(qr_ref, kr_ref, vr_ref, gr_ref, st_ref, m_ref, hr_ref, 1)


def _mlstm(qk, v, gif, st0, m0):
    bsz, length, _ = v.shape
    t = M_CHUNK
    nc = length // t
    hq = N_HEADS * D_QK
    hv = N_HEADS * D_V
    ng = gif.shape[-1]
    fwd = lambda b, i: (b, i, 0)
    rev = lambda b, i: (b, nc - 1 - i, 0)
    fwd_k = lambda b, i: (b, i, 1)
    rev_k = lambda b, i: (b, nc - 1 - i, 1)
    st_spec = pl.BlockSpec((1, 2, N_HEADS, 2 * D_V, D_QK), lambda b, i: (b, 0, 0, 0, 0))
    m_spec = pl.BlockSpec((1, 2, N_HEADS, 1, 1), lambda b, i: (b, 0, 0, 0, 0))
    vmem = 24 * 1024 * 1024
    return pl.pallas_call(
        _mlstm_body,
        grid=(bsz, nc),
        in_specs=[pl.BlockSpec((1, t, hq), fwd), pl.BlockSpec((1, t, hq), fwd_k),
                  pl.BlockSpec((1, t, hv), fwd), pl.BlockSpec((1, t, ng), fwd),
                  pl.BlockSpec((1, t, hq), rev), pl.BlockSpec((1, t, hq), rev_k),
                  pl.BlockSpec((1, t, hv), rev), pl.BlockSpec((1, t, ng), rev),
                  st_spec, m_spec],
        out_specs=[pl.BlockSpec((1, t, hv), fwd), pl.BlockSpec((1, t, hv), rev), st_spec, m_spec],
        out_shape=[jax.ShapeDtypeStruct((bsz, length, hv), BF16),
                   jax.ShapeDtypeStruct((bsz, length, hv), BF16),
                   jax.ShapeDtypeStruct(st0.shape, F32),
                   jax.ShapeDtypeStruct(m0.shape, F32)],
        compiler_params=_cparams(2, vmem),
        name="mlstm",
    )(qk, qk, v, gif, qk, qk, v, gif, st0, m0)


def _s5_tables(a_re, a_im, log_dt, b_re, b_im, c_re, c_im, d_skip):
    hp = lax.Precision.HIGHEST
    t = S5_CHUNK
    n_dir, groups, p = a_re.shape
    cg = b_re.shape[-1]
    dt = jnp.exp(log_dt)[..., None]

    def lam_pow(n):
        mag = jnp.exp(n * (dt * a_re)[..., None])
        ang = n * (dt * a_im)[..., None]
        return mag * jnp.cos(ang), mag * jnp.sin(ang)

    ab_re, ab_im = (z[..., 0] for z in lam_pow(jnp.ones((1,), F32)))
    den = a_re * a_re + a_im * a_im
    xr = ab_re - 1.0
    cf_re = (xr * a_re + ab_im * a_im) / den
    cf_im = (ab_im * a_re - xr * a_im) / den
    bb_re = cf_re[..., None] * b_re - cf_im[..., None] * b_im
    bb_im = cf_re[..., None] * b_im + cf_im[..., None] * b_re
    jj = jnp.arange(t + 1, dtype=F32)
    lp_re, lp_im = lam_pow(jj)

    def w_dir(d, exps):
        lr = lp_re[d][:, :, exps]
        li = lp_im[d][:, :, exps]
        wr = lr[..., None] * bb_re[d][:, :, None, :] - li[..., None] * bb_im[d][:, :, None, :]
        wi = lr[..., None] * bb_im[d][:, :, None, :] + li[..., None] * bb_re[d][:, :, None, :]
        to_rows = lambda w: jnp.transpose(w, (0, 2, 3, 1)).reshape(groups, t * cg, p)
        return to_rows(wr), to_rows(wi)

    s_idx = jnp.arange(t)
    wf_re, wf_im = w_dir(0, t - 1 - s_idx)
    wr_re, wr_im = w_dir(1, s_idx)
    w_in = jnp.concatenate([wf_re, wf_im, wr_re, wr_im], axis=-1)

    def c_dir(d, exps):
        lr = lp_re[d][:, :, exps]
        li = lp_im[d][:, :, exps]
        cr = jnp.transpose(c_re[d], (0, 2, 1))
        ci = jnp.transpose(c_im[d], (0, 2, 1))
        o_re = cr[:, :, None, :] * lr[..., None] - ci[:, :, None, :] * li[..., None]
        o_im = cr[:, :, None, :] * li[..., None] + ci[:, :, None, :] * lr[..., None]
        return o_re.reshape(groups, p, t * cg), (-o_im).reshape(groups, p, t * cg)

    cf_r, cf_i = c_dir(0, s_idx + 1)
    cr_r, cr_i = c_dir(1, t - s_idx)
    c_out = jnp.concatenate([cf_r, cf_i, cr_r, cr_i], axis=1)

    def k_dir(d):
        lr = lp_re[d][:, :, :t]
        li = lp_im[d][:, :, :t]
        clr = c_re[d][:, :, :, None] * lr[:, None] - c_im[d][:, :, :, None] * li[:, None]
        cli = c_re[d][:, :, :, None] * li[:, None] + c_im[d][:, :, :, None] * lr[:, None]
        return (jnp.einsum('gqpj,gpc->gjqc', clr, bb_re[d], precision=hp)
                - jnp.einsum('gqpj,gpc->gjqc', cli, bb_im[d], precision=hp))

    kf = k_dir(0)
    kr = k_dir(1)
    lag = s_idx[None, :] - s_idx[:, None]
    resp_f = jnp.where((lag >= 0)[None, :, :, None, None], kf[:, jnp.clip(lag, 0, t - 1)], 0.0)
    resp_r = jnp.where((lag <= 0)[None, :, :, None, None], kr[:, jnp.clip(-lag, 0, t - 1)], 0.0)
    skip = (jnp.eye(t, dtype=F32)[None, :, :, None, None] * jnp.eye(cg, dtype=F32)[None, None, None]
            * d_skip.reshape(groups, 1, 1, cg, 1))
    m_tz = jnp.transpose(resp_f + resp_r + skip, (0, 1, 4, 2, 3)).reshape(groups, t * cg, t * cg)

    n_pow = 8
    kk = (t * (2 ** jnp.arange(n_pow))).astype(F32)
    mp_re, mp_im = lam_pow(kk)
    mp_re = jnp.transpose(mp_re, (0, 1, 3, 2))
    mp_im = jnp.transpose(mp_im, (0, 1, 3, 2))
    mu_a = jnp.concatenate([mp_re[0], mp_re[0], mp_re[1], mp_re[1]], axis=-1)
    mu_b = jnp.concatenate([-mp_im[0], mp_im[0], -mp_im[1], mp_im[1]], axis=-1)
    return w_in.astype(BF16), m_tz.astype(BF16), c_out.astype(BF16), mu_a, mu_b


def _s5_body(x_ref, win_ref, mtz_ref, cout_ref, mua_ref, mub_ref, h0_ref, y_ref, hout_ref, *, n_steps):
    x = x_ref[0, 0]
    nch = x.shape[0]
    p2 = 2 * S5_STATE
    local = jnp.dot(x, win_ref[0], preferred_element_type=F32)
    h0 = h0_ref[0, 0]
    row = lax.broadcasted_iota(jnp.int32, (nch, 1), 0)
    lf = local[:, :p2]
    lr = local[:, p2:]
    zf = jnp.where(row == 0, h0[:, :p2], pltpu.roll(lf, 1, axis=0))
    zr = jnp.where(row == nch - 1, h0[:, p2:], pltpu.roll(lr, nch - 1, axis=0))
    for k in range(n_steps):
        sft = 1 << k
        a_f = mua_ref[0, k:k + 1, :p2]
        b_f = mub_ref[0, k:k + 1, :p2]
        a_r = mua_ref[0, k:k + 1, p2:]
        b_r = mub_ref[0, k:k + 1, p2:]
        pf = jnp.where(row >= sft, pltpu.roll(zf, sft, axis=0), 0.0)
        pr = jnp.where(row < nch - sft, pltpu.roll(zr, nch - sft, axis=0), 0.0)
        zf = zf + a_f * pf + b_f * pltpu.roll(pf, S5_STATE, axis=1)
        zr = zr + a_r * pr + b_r * pltpu.roll(pr, S5_STATE, axis=1)
    z = jnp.concatenate([zf, zr], axis=1)
    y = jnp.dot(x, mtz_ref[0], preferred_element_type=F32)
    y = y + jnp.dot(z.astype(BF16), cout_ref[0], preferred_element_type=F32)
    y_ref[0, 0] = jax.nn.gelu(y).astype(y_ref.dtype)
    a1 = mua_ref[0, 0:1, :]
    b1 = mub_ref[0, 0:1, :]
    zf_l = zf[nch - 1:nch, :]
    zr_l = zr[0:1, :]
    ends = jnp.concatenate([zf_l, zr_l], axis=1)
    ends_sw = jnp.concatenate([pltpu.roll(zf_l, S5_STATE, axis=1), pltpu.roll(zr_l, S5_STATE, axis=1)], axis=1)
    loc_end = jnp.concatenate([lf[nch - 1:nch, :], lr[0:1, :]], axis=1)
    hout_ref[0, 0] = a1 * ends + b1 * ends_sw + loc_end


def _s5(xg, w_in, m_tz, c_out, mu_a, mu_b, h0):
    bsz, groups, nch, tc = xg.shape
    p4 = 4 * S5_STATE
    n_steps = max(1, (nch - 1).bit_length())
    per_g = lambda g, b: (g, 0, 0)
    per_bg = lambda g, b: (b, g, 0, 0)
    return pl.pallas_call(
        functools.partial(_s5_body, n_steps=n_steps),
        grid=(groups, bsz),
        in_specs=[pl.BlockSpec((1, 1, nch, tc), per_bg),
                  pl.BlockSpec((1, tc, p4), per_g),
                  pl.BlockSpec((1, tc, tc), per_g),
                  pl.BlockSpec((1, p4, tc), per_g),
                  pl.BlockSpec((1, mu_a.shape[1], p4), per_g),
                  pl.BlockSpec((1, mu_b.shape[1], p4), per_g),
                  pl.BlockSpec((1, 1, 1, p4), per_bg)],
        out_specs=[pl.BlockSpec((1, 1, nch, tc), per_bg),
                   pl.BlockSpec((1, 1, 1, p4), per_bg)],
        out_shape=[jax.ShapeDtypeStruct((bsz, groups, nch, tc), BF16),
                   jax.ShapeDtypeStruct((bsz, groups, 1, p4), F32)],
        compiler_params=_cparams(2, 16 * 1024 * 1024),
        name="s5",
    )(xg, w_in, m_tz, c_out, mu_a, mu_b, h0)


def _mixout_body(x_ref, hf_ref, hr_ref, ys_ref, mod_ref, g1_ref, g2_ref, gmh_ref, wog_ref, wa_ref, wglu_ref,
                 bglu_ref, wb_ref, bgate_ref, wo_ref, wr_ref, br_ref,
                 x1_ref, h2_ref, ti_ref, tp_ref, rk_ref, cnt_ref, base_ref):
    x = x_ref[0]
    d = x.shape[-1]
    mod = mod_ref[0]
    h = _norm_mod(x, g1_ref[...], mod[1:2], mod[0:1]).astype(BF16)
    og = jnp.dot(h, wog_ref[...], preferred_element_type=F32)
    hm = hf_ref[0].astype(F32) + hr_ref[0].astype(F32)
    heads = []
    for hd in range(N_HEADS):
        blk = hm[:, hd * D_V:(hd + 1) * D_V]
        heads.append(blk * lax.rsqrt(jnp.mean(blk * blk, axis=-1, keepdims=True) + EPS))
    hn = jnp.concatenate(heads, axis=1) * gmh_ref[...]
    y_a = jnp.dot((hn * jax.nn.sigmoid(og[:, :d])).astype(BF16), wa_ref[...], preferred_element_type=F32)
    ys = ys_ref[0]
    glu = jax.nn.sigmoid(jnp.dot(ys, wglu_ref[...], preferred_element_type=F32) + bglu_ref[...])
    y_b = jnp.dot((ys.astype(F32) * glu).astype(BF16), wb_ref[...], preferred_element_type=F32)
    gates = jax.nn.sigmoid(og[:, d:] + bgate_ref[...])
    merged = gates[:, :d] * y_a + gates[:, d:] * y_b
    x1 = x + mod[2:3] * jnp.dot(merged.astype(BF16), wo_ref[...], preferred_element_type=F32)
    x1_ref[0] = x1
    h2f = _norm_mod(x1, g2_ref[...], mod[4:5], mod[3:4])
    h2_ref[0] = h2f
    logits = jnp.dot(h2f.astype(BF16), wr_ref[...], preferred_element_type=F32) + br_ref[...]
    tm, n_e = logits.shape
    e_iota = lax.broadcasted_iota(jnp.int32, (tm, n_e), 1)
    lane = lax.broadcasted_iota(jnp.int32, (tm, ti_ref.shape[-1]), 1)
    ti = jnp.zeros(lane.shape, jnp.int32)
    tv = jnp.zeros(lane.shape, F32)
    top = None
    chosen = []
    for k in range(TOP_K):
        mx = jnp.max(logits, axis=-1, keepdims=True)
        idx = jnp.min(jnp.where(logits == mx, e_iota, n_e), axis=-1, keepdims=True)
        top = mx if top is None else top
        ti = jnp.where(lane == k, idx, ti)
        tv = jnp.where(lane == k, jnp.exp(mx - top), tv)
        chosen.append(e_iota == idx)
        logits = jnp.where(chosen[-1], -jnp.inf, logits)
    ti_ref[0] = ti
    tp_ref[0] = tv / jnp.sum(tv, axis=-1, keepdims=True)

    @pl.when((pl.program_id(0) == 0) & (pl.program_id(1) == 0))
    def _():
        base_ref[...] = jnp.zeros(base_ref.shape, F32)

    onehot = jnp.zeros((tm, n_e), F32)
    for sel in chosen:
        onehot = onehot + jnp.where(sel, 1.0, 0.0)
    below = (lax.broadcasted_iota(jnp.int32, (tm, tm), 1) < lax.broadcasted_iota(jnp.int32, (tm, tm), 0))
    before = jnp.dot(jnp.where(below, 1.0, 0.0).astype(BF16), onehot.astype(BF16),
                     preferred_element_type=F32) + base_ref[...]
    rk = jnp.zeros(lane.shape, jnp.int32)
    for k, sel in enumerate(chosen):
        rank = jnp.sum(jnp.where(sel, before, 0.0), axis=-1, keepdims=True)
        rk = jnp.where(lane == k, rank.astype(jnp.int32), rk)
    rk_ref[0] = rk
    base_ref[...] = base_ref[...] + jnp.sum(onehot, axis=0, keepdims=True)
    cnt_ref[...] = base_ref[...].astype(jnp.int32)


def _mixout(x, h_f, h_r, ys, mod, g1, g2, g_mh, w_og, w_a, w_glu, b_glu, w_b, b_gate, w_o, w_r, b_r, tm):
    bsz, length, d = x.shape
    sw = ys.shape[-1]
    n_e = w_r.shape[-1]
    tok = lambda b, i: (b, i, 0)
    const2 = lambda b, i: (0, 0)

    def wspec(w):
        return pl.BlockSpec(w.shape, const2, pipeline_mode=pl.Buffered(1))

    weights = (w_og, w_a, w_glu, w_b, w_o, w_r)
    w_bytes = sum(int(w.size) * w.dtype.itemsize for w in weights)
    vmem = w_bytes + 2 * tm * (d * 4 + 2 * d * 2 + sw * 2 + d * 4 + d * 2) + 14 * tm * d * 4
    return pl.pallas_call(
        _mixout_body,
        grid=(bsz, length // tm),
        in_specs=[pl.BlockSpec((1, tm, d), tok), pl.BlockSpec((1, tm, d), tok), pl.BlockSpec((1, tm, d), tok),
                  pl.BlockSpec((1, tm, sw), tok), pl.BlockSpec((1, 6, d), lambda b, i: (b, 0, 0)),
                  pl.BlockSpec((1, d), const2), pl.BlockSpec((1, d), const2), pl.BlockSpec((1, d), const2),
                  wspec(w_og), wspec(w_a), wspec(w_glu), pl.BlockSpec((1, sw), const2), wspec(w_b),
                  pl.BlockSpec((1, 2 * d), const2), wspec(w_o), wspec(w_r), pl.BlockSpec((1, n_e), const2)],
        out_specs=[pl.BlockSpec((1, tm, d), tok), pl.BlockSpec((1, tm, d), tok),
                   pl.BlockSpec((1, tm, 8), tok), pl.BlockSpec((1, tm, 8), tok), pl.BlockSpec((1, tm, 8), tok),
                   pl.BlockSpec((1, n_e), const2)],
        out_shape=[jax.ShapeDtypeStruct((bsz, length, d), F32),
                   jax.ShapeDtypeStruct((bsz, length, d), F32),
                   jax.ShapeDtypeStruct((bsz, length, 8), jnp.int32),
                   jax.ShapeDtypeStruct((bsz, length, 8), F32),
                   jax.ShapeDtypeStruct((bsz, length, 8), jnp.int32),
                   jax.ShapeDtypeStruct((1, n_e), jnp.int32)],
        scratch_shapes=[pltpu.VMEM((1, n_e), F32)],
        compiler_params=_cparams(2, vmem),
        name="mixout",
    )(x, h_f, h_r, ys, mod, g1.reshape(1, d), g2.reshape(1, d), g_mh.reshape(1, d), w_og, w_a, w_glu,
      b_glu.reshape(1, sw), w_b, b_gate.reshape(1, 2 * d), w_o, w_r, b_r.reshape(1, n_e))


def _rowpos_body(ti_ref, rk_ref, rs_ref, pos_ref):
    ti = ti_ref[...]
    rk = rk_ref[...]
    n, w = ti.shape
    n_e = rs_ref.shape[-1]
    e_iota = lax.broadcasted_iota(jnp.int32, (n, n_e), 1)
    lane = lax.broadcasted_iota(jnp.int32, (n, w), 1)
    pos = jnp.zeros((n, w), jnp.int32)
    for k in range(TOP_K):
        start = jnp.sum(jnp.where(e_iota == ti[:, k:k + 1], rs_ref[...], 0), axis=-1, keepdims=True)
        pos = jnp.where(lane == k, start + rk[:, k:k + 1], pos)
    pos_ref[...] = pos


def _rowpos(top_i, rank, row_start, tn):
    n, w = top_i.shape
    n_e = row_start.shape[-1]
    return pl.pallas_call(
        _rowpos_body,
        grid=(n // tn,),
        in_specs=[pl.BlockSpec((tn, w), lambda i: (i, 0)), pl.BlockSpec((tn, w), lambda i: (i, 0)),
                  pl.BlockSpec((1, n_e), lambda i: (0, 0))],
        out_specs=pl.BlockSpec((tn, w), lambda i: (i, 0)),
        out_shape=jax.ShapeDtypeStruct((n, w), jnp.int32),
        compiler_params=_cparams(1, 16 * tn * 128 * 4),
        name="rowpos",
    )(top_i, rank, row_start)


def _dispatch_body(pos_ref, h_ref, xs_hbm, sem):
    tm = h_ref.shape[0]

    def issue(t, carry):
        for k in range(TOP_K):
            row = pos_ref[t * TOP_K + k]
            pltpu.make_async_copy(h_ref.at[pl.ds(t, 1), :], xs_hbm.at[pl.ds(row, 1), :], sem).start(priority=k % 2)
        return carry

    lax.fori_loop(0, tm, issue, 0, unroll=4)
    for k in range(TOP_K):
        pltpu.make_async_copy(h_ref, xs_hbm.at[pl.ds(0, tm), :], sem).wait()


def _dispatch(pos_flat, h_rows, n_rows, tm):
    n, w = h_rows.shape
    return pl.pallas_call(
        _dispatch_body,
        grid=(n // tm,),
        in_specs=[pl.BlockSpec((tm * TOP_K,), lambda i: (i,), memory_space=pltpu.SMEM),
                  pl.BlockSpec((tm, w), lambda i: (i, 0))],
        out_specs=pl.BlockSpec(memory_space=pl.ANY),
        out_shape=jax.ShapeDtypeStruct((n_rows, w), h_rows.dtype),
        scratch_shapes=[pltpu.SemaphoreType.DMA(())],
        compiler_params=_cparams(1, 8 * tm * w * 4),
        name="dispatch",
    )(pos_flat, h_rows)


def _experts_body(te_ref, tv_ref, nt_ref, xs_ref, win_ref, bin_ref, wout_ref, bout_ref, y_ref, win_bf, wout_bf,
                  *, f_chunk, cast_rows):
    i = pl.program_id(0)
    e = te_ref[i]
    e_prev = te_ref[jnp.maximum(i - 1, 0)]
    d, f2 = win_bf.shape
    f = f2 // 2

    @pl.when((i == 0) | (e != e_prev))
    def _():
        def cast_in(r, carry):
            rows = pl.ds(pl.multiple_of(r * cast_rows, cast_rows), cast_rows)
            win_bf[rows, :] = win_ref[0, rows, :].astype(BF16)
            return carry

        def cast_out(r, carry):
            rows = pl.ds(pl.multiple_of(r * cast_rows, cast_rows), cast_rows)
            wout_bf[rows, :] = wout_ref[0, rows, :].astype(BF16)
            return carry

        lax.fori_loop(0, d // cast_rows, cast_in, 0)
        lax.fori_loop(0, f // cast_rows, cast_out, 0)

    @pl.when(i < nt_ref[0])
    def _():
        live = lax.broadcasted_iota(jnp.int32, xs_ref.shape, 0) < tv_ref[i]
        x = jnp.where(live, xs_ref[...], 0.0).astype(BF16)
        acc = None
        for c in range(f // f_chunk):
            lo = c * f_chunk
            zg = jnp.dot(x, win_bf[:, lo:lo + f_chunk], preferred_element_type=F32) + bin_ref[0, :, lo:lo + f_chunk]
            zl = (jnp.dot(x, win_bf[:, f + lo:f + lo + f_chunk], preferred_element_type=F32)
                  + bin_ref[0, :, f + lo:f + lo + f_chunk])
            glu = jnp.minimum(zg, SWIGLU_LIMIT)
            lin = jnp.clip(zl, -SWIGLU_LIMIT, SWIGLU_LIMIT)
            act = glu * jax.nn.sigmoid(SWIGLU_ALPHA * glu) * (lin + 1.0)
            part = jnp.dot(act.astype(BF16), wout_bf[lo:lo + f_chunk, :], preferred_element_type=F32)
            acc = part if acc is None else acc + part
        y_ref[...] = acc + bout_ref[0]

    @pl.when(i >= nt_ref[0])
    def _():
        y_ref[...] = jnp.zeros(y_ref.shape, y_ref.dtype)


def _experts(tile_expert, tile_valid, n_tiles, xs, w_e_in, b_e_in, w_e_out, b_e_out, tm):
    rows, d = xs.shape
    n_e, _, f2 = w_e_in.shape
    f = f2 // 2
    nt_max = rows // tm
    row_map = lambda i, te, tv, nt: (jnp.minimum(i, nt[0] - 1), 0)
    exp_map = lambda i, te, tv, nt: (te[i], 0, 0)
    vmem = 2 * (d * f2 + f * d) * 4 + (d * f2 + f * d) * 2 + 4 * tm * d * 2 + 10 * tm * d * 4
    grid_spec = pltpu.PrefetchScalarGridSpec(
        num_scalar_prefetch=3,
        grid=(nt_max,),
        in_specs=[pl.BlockSpec((tm, d), row_map),
                  pl.BlockSpec((1, d, f2), exp_map), pl.BlockSpec((1, 1, f2), exp_map),
                  pl.BlockSpec((1, f, d), exp_map), pl.BlockSpec((1, 1, d), exp_map)],
        out_specs=pl.BlockSpec((tm, d), lambda i, te, tv, nt: (i, 0)),
        scratch_shapes=[pltpu.VMEM((d, f2), BF16), pltpu.VMEM((f, d), BF16)],
    )
    return pl.pallas_call(
        functools.partial(_experts_body, f_chunk=512, cast_rows=128),
        grid_spec=grid_spec,
        out_shape=jax.ShapeDtypeStruct((rows, d), F32),
        compiler_params=_cparams(1, vmem),
        name="experts",
    )(tile_expert, tile_valid, n_tiles, xs, w_e_in, b_e_in.reshape(n_e, 1, f2), w_e_out, b_e_out.reshape(n_e, 1, d))


def _final_body(pos_ref, posn_ref, x1_ref, tp_ref, gt_ref, g_ref, y_hbm, o_ref, buf, sem):
    i = pl.program_id(0)
    n = pl.num_programs(0)
    tm = x1_ref.shape[0]

    def gather(p_ref, slot):
        def issue(t, carry):
            for k in range(TOP_K):
                row = p_ref[t * TOP_K + k]
                pltpu.make_async_copy(y_hbm.at[pl.ds(row, 1), :], buf.at[slot, k, pl.ds(t, 1), :],
                                      sem.at[slot]).start(priority=k % 2)
            return carry

        lax.fori_loop(0, tm, issue, 0, unroll=4)

    @pl.when(i == 0)
    def _():
        gather(pos_ref, 0)

    @pl.when(i + 1 < n)
    def _():
        gather(posn_ref, (i + 1) % 2)

    slot = i % 2
    for k in range(TOP_K):
        pltpu.make_async_copy(y_hbm.at[pl.ds(0, tm), :], buf.at[slot, k], sem.at[slot]).wait()
    moe = None
    for k in range(TOP_K):
        term = tp_ref[:, k:k + 1] * buf[slot, k]
        moe = term if moe is None else moe + term
    x2 = x1_ref[...] + gt_ref[0] * moe
    o_ref[...] = x2 * lax.rsqrt(jnp.mean(x2 * x2, axis=-1, keepdims=True) + EPS) * g_ref[...]


def _final(pos_flat, x1, top_p, gt2, g_final, y_rows, tm):
    n_tok, d = x1.shape
    bsz = gt2.shape[0]
    per_b = n_tok // bsz // tm
    nxt = lambda i: (jnp.minimum(i + 1, n_tok // tm - 1),)
    return pl.pallas_call(
        _final_body,
        grid=(n_tok // tm,),
        in_specs=[pl.BlockSpec((tm * TOP_K,), lambda i: (i,), memory_space=pltpu.SMEM),
                  pl.BlockSpec((tm * TOP_K,), nxt, memory_space=pltpu.SMEM),
                  pl.BlockSpec((tm, d), lambda i: (i, 0)),
                  pl.BlockSpec((tm, top_p.shape[-1]), lambda i: (i, 0)),
                  pl.BlockSpec((1, 1, d), lambda i: (i // per_b, 0, 0)),
                  pl.BlockSpec((1, d), lambda i: (0, 0)),
                  pl.BlockSpec(memory_space=pl.ANY)],
        out_specs=pl.BlockSpec((tm, d), lambda i: (i, 0)),
        out_shape=jax.ShapeDtypeStruct((n_tok, d), F32),
        scratch_shapes=[pltpu.VMEM((2, TOP_K, tm, d), F32), pltpu.SemaphoreType.DMA((2,))],
        compiler_params=_cparams(1, 2 * TOP_K * tm * d * 4 + 4 * tm * d * 4 + 8 * tm * d * 4),
        name="final",
    )(pos_flat, pos_flat, x1, top_p, gt2, g_final.reshape(1, d), y_rows)


def _tile_table(counts, tm, nt_max):
    tiles = (counts + tm - 1) // tm
    tile_end = jnp.cumsum(tiles)
    tile_start = tile_end - tiles
    n_tiles = tile_end[-1]
    tile_ids = jnp.arange(nt_max, dtype=jnp.int32)
    last_e = jnp.sum((n_tiles - 1) >= tile_end).astype(jnp.int32)
    te = jnp.sum(tile_ids[:, None] >= tile_end[None, :], axis=1).astype(jnp.int32)
    te = jnp.where(tile_ids < n_tiles, te, last_e)
    sel = te[:, None] == jnp.arange(counts.shape[0], dtype=jnp.int32)[None, :]
    cnt_t = jnp.sum(jnp.where(sel, counts[None, :], 0), axis=1)
    start_t = jnp.sum(jnp.where(sel, tile_start[None, :], 0), axis=1)
    live = jnp.clip(cnt_t - (tile_ids - start_t) * tm, 0, tm).astype(jnp.int32)
    live = jnp.where(tile_ids < n_tiles, live, 0)
    return (tile_start * tm).astype(jnp.int32), te, live, n_tiles.reshape(1).astype(jnp.int32)


def kernel(x, c, ctx, c_ctx, w_ada, b_ada, g_norm1, g_norm2, w_in, w_conv_qk, b_ifgate, g_mh, w_branch_m,
           s5_a_re, s5_a_im, s5_log_dt, s5_b_re, s5_b_im, s5_c_re, s5_c_im, s5_d, w_glu, b_glu, w_branch_s,
           b_merge_gate, w_o, w_router, b_router, w_e_in, b_e_in, w_e_out, b_e_out, g_final):
    bsz, length, d = x.shape
    l_ctx = ctx.shape[1]
    n_qk = 2 * N_HEADS * D_QK
    n_v = N_HEADS * D_V
    n_if = 4 * N_HEADS
    n_u = S5_GROUPS * S5_GROUP
    off_if = n_qk + n_v
    off_u = off_if + n_if
    off_o = off_u + n_u
    layer = 0

    pad_rows = -(bsz + 1) % 8
    c_rows = jnp.concatenate([c, c_ctx[None, :], jnp.zeros((pad_rows, d), F32)], axis=0)
    mod_all = _ada(c_rows, w_ada[layer], b_ada[layer])
    mod = mod_all[:bsz].reshape(bsz, 6, d)
    mod_c = mod_all[bsz, :2 * d].reshape(2, 1, 1, d)

    w_l = w_in[layer]
    w_state = jnp.concatenate([w_l[:, :off_if], w_l[:, off_u:off_o], w_l[:, off_if:off_u]], axis=1).astype(BF16)
    b_if = b_ifgate[layer].reshape(n_if)
    qk_c, v_c, u_c, gif_c = _proj(ctx, g_norm1[layer], mod_c[1], mod_c[0], w_state, b_if, n_qk, n_v, n_u, l_ctx)
    qk_l, v_l, u_l, gif_l = _proj(x, g_norm1[layer], mod[:, 1:2], mod[:, 0:1], w_state, b_if, n_qk, n_v, n_u, 512)

    w9 = w_conv_qk[layer].reshape(9, n_qk)
    col_scale = jnp.concatenate([jnp.full((1, n_qk // 2), D_QK ** -0.5, F32), jnp.ones((1, n_qk // 2), F32)], axis=1)
    qk_c = _conv(qk_c, w9, col_scale, l_ctx, l_ctx, 512)
    qk_l = _conv(qk_l, w9, col_scale, GRID_W, 512, 512)

    st0 = jnp.zeros((bsz, 2, N_HEADS, 2 * D_V, D_QK), F32)
    m0 = jnp.zeros((bsz, 2, N_HEADS, 1, 1), F32)
    _, _, st_c, m_c = _mlstm(qk_c, v_c, gif_c, st0, m0)
    h_f, h_r, _, _ = _mlstm(qk_l, v_l, gif_l, st_c, m_c)

    tables = _s5_tables(s5_a_re[layer], s5_a_im[layer], s5_log_dt[layer], s5_b_re[layer], s5_b_im[layer],
                        s5_c_re[layer], s5_c_im[layer], s5_d[layer])

    def to_chunks(u):
        n = u.shape[1] // S5_CHUNK
        u5 = u.reshape(bsz, n, S5_CHUNK, S5_GROUPS, S5_GROUP)
        return jnp.transpose(u5, (0, 3, 1, 2, 4)).reshape(bsz, S5_GROUPS, n, S5_CHUNK * S5_GROUP)

    hs0 = jnp.zeros((bsz, S5_GROUPS, 1, 4 * S5_STATE), F32)
    _, hs_c = _s5(to_chunks(u_c), *tables, hs0)
    y_g, _ = _s5(to_chunks(u_l), *tables, hs_c)
    n_ch = length // S5_CHUNK
    ys = jnp.transpose(y_g.reshape(bsz, S5_GROUPS, n_ch, S5_CHUNK, S5_GROUP), (0, 2, 3, 1, 4)).reshape(bsz, length, n_u)

    w_og = w_l[:, off_o:].astype(BF16)
    x1, h2, top_i, top_p, rank, counts = _mixout(
        x, h_f, h_r, ys, mod, g_norm1[layer], g_norm2[layer], g_mh[layer], w_og, w_branch_m[layer].astype(BF16),
        w_glu[layer].astype(BF16), b_glu[layer], w_branch_s[layer].astype(BF16), b_merge_gate[layer],
        w_o[layer].astype(BF16), w_router[layer].astype(BF16), b_router[layer], 512)

    tm_e = 512
    n_tok = bsz * length
    nt_max = n_tok * TOP_K // tm_e + N_EXPERTS
    row_start, tile_expert, tile_live, n_tiles = _tile_table(counts.reshape(N_EXPERTS), tm_e, nt_max)
    pos = _rowpos(top_i.reshape(n_tok, 8), rank.reshape(n_tok, 8), row_start.reshape(1, N_EXPERTS), min(4096, n_tok))
    pos_flat = pos[:, :TOP_K].reshape(n_tok * TOP_K)
    xs = _dispatch(pos_flat, h2.reshape(n_tok, d), nt_max * tm_e, 512)
    y_rows = _experts(tile_expert, tile_live, n_tiles, xs, w_e_in[layer], b_e_in[layer], w_e_out[layer],
                      b_e_out[layer], tm_e)
    out = _final(pos_flat, x1.reshape(n_tok, d), top_p.reshape(n_tok, 8), mod[:, 5:6], g_final, y_rows, 256)
    return out.reshape(bsz, length, d)
```

```python
import functools
import math

import jax
import jax.numpy as jnp
from jax import lax
from jax.experimental import pallas as pl
from jax.experimental.pallas import tpu as pltpu

F32 = jnp.float32
BF16 = jnp.bfloat16
EPS = 1e-6

N_HEADS = 8
D_QK = 64
D_V = 128
M_CHUNK = 128
GRID_W = 64
S5_GROUPS = 32
S5_GROUP = 16
S5_STATE = 64
S5_CHUNK = 16
N_EXPERTS = 32
TOP_K = 4
SWIGLU_LIMIT = 7.0
SWIGLU_ALPHA = 1.702

V7X_VMEM_BYTES = 64 * 1024 * 1024
_VMEM_CAP = V7X_VMEM_BYTES - 8 * 1024 * 1024


def _cparams(n_axes, vmem_bytes):
    limit = int(min(_VMEM_CAP, max(32 * 1024 * 1024, vmem_bytes)))
    return pltpu.CompilerParams(dimension_semantics=("arbitrary",) * n_axes, vmem_limit_bytes=limit)


def _silu(x):
    return x * jax.nn.sigmoid(x)


def _norm_mod(x, g, scale, shift):
    ms = jnp.mean(x * x, axis=-1, keepdims=True)
    return (x * lax.rsqrt(ms + EPS) * g) * (1.0 + scale) + shift


def _split3(x):
    hi = x.astype(BF16)
    r1 = x - hi.astype(F32)
    mid = r1.astype(BF16)
    lo = (r1 - mid.astype(F32)).astype(BF16)
    return hi, mid, lo


def _dot_nt(a, b):
    return lax.dot_general(a, b, (((1,), (1,)), ((), ())), preferred_element_type=F32)


def _dot_tn(a, b):
    return lax.dot_general(a, b, (((0,), (0,)), ((), ())), preferred_element_type=F32)


def _ada_body(c_ref, w_ref, b_ref, o_ref):
    s = _silu(c_ref[...])
    o_ref[...] = jnp.dot(s, w_ref[...], preferred_element_type=F32,
                         precision=lax.Precision.HIGHEST) + b_ref[...]


def _ada(c_rows, w_ada, b_ada):
    rows, d = c_rows.shape
    n = w_ada.shape[1]
    tn = 1024
    return pl.pallas_call(
        _ada_body,
        grid=(n // tn,),
        in_specs=[pl.BlockSpec((rows, d), lambda j: (0, 0)),
                  pl.BlockSpec((d, tn), lambda j: (0, j)),
                  pl.BlockSpec((1, tn), lambda j: (0, j))],
        out_specs=pl.BlockSpec((rows, tn), lambda j: (0, j)),
        out_shape=jax.ShapeDtypeStruct((rows, n), F32),
        compiler_params=_cparams(1, 4 * d * tn * 4),
        name="ada",
    )(c_rows, w_ada, b_ada.reshape(1, n))


def _proj_body(x_ref, g_ref, sc_ref, sh_ref, w_ref, wif_ref, bif_ref, qk_ref, v_ref, u_ref, gif_ref):
    h = _norm_mod(x_ref[0], g_ref[...], sc_ref[0], sh_ref[0]).astype(BF16)
    n_qk = qk_ref.shape[-1]
    n_v = v_ref.shape[-1]
    qk_ref[0] = jnp.dot(h, w_ref[:, :n_qk], preferred_element_type=F32).astype(BF16)
    v_ref[0] = jnp.dot(h, w_ref[:, n_qk:n_qk + n_v], preferred_element_type=F32).astype(BF16)
    u_ref[0] = jnp.dot(h, w_ref[:, n_qk + n_v:], preferred_element_type=F32).astype(BF16)
    gif_ref[0] = _dot_nt(wif_ref[...], h) + bif_ref[...]


def _proj(x, g, scale, shift, w_state, w_if_t, b_if, n_qk, n_v, n_u, tm):
    bsz, length, d = x.shape
    n_if = b_if.shape[-1]
    per_batch = scale.shape[0] == bsz
    mod_map = (lambda b, i: (b, 0, 0)) if per_batch else (lambda b, i: (0, 0, 0))
    cols = w_state.shape[1]
    vmem = 2 * (tm * d * 4 + d * cols * 2 + tm * (n_qk + n_v + n_u) * 2 + tm * 128 * 4) + 6 * tm * d * 4
    return pl.pallas_call(
        _proj_body,
        grid=(bsz, length // tm),
        in_specs=[pl.BlockSpec((1, tm, d), lambda b, i: (b, i, 0)),
                  pl.BlockSpec((1, d), lambda b, i: (0, 0)),
                  pl.BlockSpec((1, 1, d), mod_map),
                  pl.BlockSpec((1, 1, d), mod_map),
                  pl.BlockSpec((d, cols), lambda b, i: (0, 0)),
                  pl.BlockSpec((n_if, d), lambda b, i: (0, 0)),
                  pl.BlockSpec((n_if, 1), lambda b, i: (0, 0))],
        out_specs=[pl.BlockSpec((1, tm, n_qk), lambda b, i: (b, i, 0)),
                   pl.BlockSpec((1, tm, n_v), lambda b, i: (b, i, 0)),
                   pl.BlockSpec((1, tm, n_u), lambda b, i: (b, i, 0)),
                   pl.BlockSpec((1, n_if, tm), lambda b, i: (b, 0, i))],
        out_shape=[jax.ShapeDtypeStruct((bsz, length, n_qk), BF16),
                   jax.ShapeDtypeStruct((bsz, length, n_v), BF16),
                   jax.ShapeDtypeStruct((bsz, length, n_u), BF16),
                   jax.ShapeDtypeStruct((bsz, n_if, length), F32)],
        compiler_params=_cparams(2, vmem),
        name="proj",
    )(x, g.reshape(1, d), scale, shift, w_state, w_if_t, b_if.reshape(n_if, 1))


def _conv_body(main_ref, prev_ref, next_ref, w_ref, q_ref, kt_ref, *, width, q_scale):
    i = pl.program_id(1)
    last = pl.num_programs(1) - 1
    t = main_ref.shape[1]
    n = t + 2 * width
    main = main_ref[0].astype(F32)
    prev = jnp.where(i > 0, prev_ref[0].astype(F32), 0.0)
    nxt = jnp.where(i < last, next_ref[0].astype(F32), 0.0)
    ext = jnp.concatenate([prev, main, nxt], axis=0)
    col = lax.broadcasted_iota(jnp.int32, (t, 1), 0) % width
    acc = None
    for dx in (-1, 0, 1):
        shifted = ext if dx == 0 else pltpu.roll(ext, (-dx) % n, axis=0)
        part = None
        for dy in (-1, 0, 1):
            tap = w_ref[(dy + 1) * 3 + (dx + 1):(dy + 1) * 3 + (dx + 1) + 1, :]
            term = tap * shifted[width + dy * width:width + dy * width + t]
            part = term if part is None else part + term
        if dx == -1:
            part = jnp.where(col == 0, 0.0, part)
        elif dx == 1:
            part = jnp.where(col == width - 1, 0.0, part)
        acc = part if acc is None else acc + part
    y = _silu(acc)

    @pl.when(pl.program_id(2) == 0)
    def _():
        q_ref[0] = (y * q_scale).astype(q_ref.dtype)

    @pl.when(pl.program_id(2) == 1)
    def _():
        kt_ref[0] = y.T.astype(kt_ref.dtype)


def _conv(qk_pre, w9, q_scale, width, t_block):
    bsz, length, ch2 = qk_pre.shape
    ch = ch2 // 2
    rpb = t_block // width
    n_rows = length // width
    vmem = 4 * (t_block + 2 * width) * ch * 2 + 14 * (t_block + 2 * width) * ch * 4
    return pl.pallas_call(
        functools.partial(_conv_body, width=width, q_scale=q_scale),
        grid=(bsz, length // t_block, 2),
        in_specs=[pl.BlockSpec((1, t_block, ch), lambda b, i, c: (b, i, c)),
                  pl.BlockSpec((1, width, ch), lambda b, i, c: (b, jnp.maximum(i * rpb - 1, 0), c)),
                  pl.BlockSpec((1, width, ch), lambda b, i, c: (b, jnp.minimum((i + 1) * rpb, n_rows - 1), c)),
                  pl.BlockSpec((9, ch), lambda b, i, c: (0, c))],
        out_specs=[pl.BlockSpec((1, t_block, ch), lambda b, i, c: (b, i, 0)),
                   pl.BlockSpec((1, ch, t_block), lambda b, i, c: (b, 0, i))],
        out_shape=[jax.ShapeDtypeStruct((bsz, length, ch), BF16),
                   jax.ShapeDtypeStruct((bsz, ch, length), BF16)],
        compiler_params=_cparams(3, vmem),
        name="conv",
    )(qk_pre, qk_pre, qk_pre, w9)


def _log_sigmoid(x):
    return jnp.minimum(x, 0.0) - jnp.log1p(jnp.exp(-jnp.abs(x)))


def _cumsum_lanes_exact(x, reverse):
    t = x.shape[-1]
    r_idx = lax.broadcasted_iota(jnp.int32, (t, t), 0)
    c_idx = lax.broadcasted_iota(jnp.int32, (t, t), 1)
    u01 = jnp.where((r_idx >= c_idx) if reverse else (r_idx <= c_idx), 1.0, 0.0).astype(BF16)
    out = None
    for piece in _split3(x):
        p = jnp.dot(piece, u01, preferred_element_type=F32)
        out = p if out is None else out + p
    return out


def _cummax_lanes(x, reverse):
    t = x.shape[-1]
    lane = lax.broadcasted_iota(jnp.int32, x.shape, 1)
    sh = 1
    while sh < t:
        if reverse:
            cand = jnp.where(lane < t - sh, pltpu.roll(x, t - sh, axis=1), -jnp.inf)
        else:
            cand = jnp.where(lane >= sh, pltpu.roll(x, sh, axis=1), -jnp.inf)
        x = jnp.maximum(x, cand)
        sh *= 2
    return x


def _mlstm_gate_rows(g_ref, m_col, d):
    t = g_ref.shape[-1]
    base = d * 2 * N_HEADS
    li = g_ref[0, base:base + N_HEADS, :]
    lf = _log_sigmoid(g_ref[0, base + N_HEADS:base + 2 * N_HEADS, :])
    b = _cumsum_lanes_exact(lf, reverse=(d == 1))
    g = li - b
    a = jnp.maximum(_cummax_lanes(g, reverse=(d == 1)), m_col)
    end = 0 if d == 1 else t - 1
    a_end = a[:, end:end + 1]
    return dict(g=g, a=a, ie=jnp.exp(m_col - a), emt=jnp.exp(-b - a), we=jnp.exp(g - a_end),
                dec=jnp.exp(m_col - a_end), m_new=b[:, end:end + 1] + a_end)


def _mlstm_body(qf_ref, kf_ref, vf_ref, gf_ref, qr_ref, kr_ref, vr_ref, gr_ref, st0_ref, m0_ref,
                hf_ref, hr_ref, st_ref, m_ref):
    @pl.when(pl.program_id(1) == 0)
    def _():
        st_ref[...] = st0_ref[...]
        m_ref[...] = m0_ref[...]

    t = qf_ref.shape[1]
    m_all = m_ref[0]
    rows = [_mlstm_gate_rows(g_ref, m_all[d * N_HEADS:(d + 1) * N_HEADS], d) for d, g_ref in ((0, gf_ref), (1, gr_ref))]
    both = lambda name: jnp.concatenate([rows[0][name], rows[1][name]], axis=0)
    n_hd = 2 * N_HEADS
    pad = jnp.zeros((128 - 3 * n_hd, t), F32)
    cols = jnp.concatenate([both('a'), both('ie'), both('emt'), pad], axis=0).T
    m_ref[0] = both('m_new')
    r_idx = lax.broadcasted_iota(jnp.int32, (t, t), 0)
    c_idx = lax.broadcasted_iota(jnp.int32, (t, t), 1)
    ones_blk = jnp.ones((t, D_V), BF16)
    for d, (q_ref, kt_ref, v_ref, h_ref) in enumerate(((qf_ref, kf_ref, vf_ref, hf_ref), (qr_ref, kr_ref, vr_ref, hr_ref))):
        causal = (c_idx >= r_idx) if d == 1 else (c_idx <= r_idx)
        for h in range(N_HEADS):
            j = d * N_HEADS + h
            q = q_ref[0, :, h * D_QK:(h + 1) * D_QK]
            kt = kt_ref[0, h * D_QK:(h + 1) * D_QK, :]
            v1 = jnp.concatenate([v_ref[0, :, h * D_V:(h + 1) * D_V], ones_blk], axis=1)
            state = st_ref[0, j]
            a_col = cols[:, j:j + 1]
            ie_col = cols[:, n_hd + j:n_hd + j + 1]
            emt_col = cols[:, 2 * n_hd + j:2 * n_hd + j + 1]
            dmat = jnp.exp(jnp.where(causal, rows[d]['g'][h:h + 1, :] - a_col, -jnp.inf))
            s = (jnp.dot(q, kt, preferred_element_type=F32) * dmat).astype(BF16)
            z = jnp.dot(q, state.astype(BF16), preferred_element_type=F32)
            p = jnp.dot(s, v1, preferred_element_type=F32)
            num = ie_col * z[:, :D_V] + p[:, :D_V]
            den = ie_col * z[:, D_V:] + p[:, D_V:]
            h_ref[0, :, h * D_V:(h + 1) * D_V] = (num / jnp.maximum(jnp.abs(den), emt_col)).astype(h_ref.dtype)
            kw = (kt.astype(F32) * rows[d]['we'][h:h + 1, :]).astype(BF16)
            st_ref[0, j] = rows[d]['dec'][h:h + 1, :] * state + jnp.dot(kw, v1, preferred_element_type=F32)


def _mlstm(q, k_t, v, gif_t, st0, m0):
    bsz, length, hv = v.shape
    t = M_CHUNK
    nc = length // t
    hq = N_HEADS * D_QK
    ng = gif_t.shape[1]
    fwd = lambda b, i: (b, i, 0)
    rev = lambda b, i: (b, nc - 1 - i, 0)
    fwd_t = lambda b, i: (b, 0, i)
    rev_t = lambda b, i: (b, 0, nc - 1 - i)
    st_spec = pl.BlockSpec((1,) + st0.shape[1:], lambda b, i: (b, 0, 0, 0))
    m_spec = pl.BlockSpec((1,) + m0.shape[1:], lambda b, i: (b, 0, 0))
    vmem = 24 * 1024 * 1024
    return pl.pallas_call(
        _mlstm_body,
        grid=(bsz, nc),
        in_specs=[pl.BlockSpec((1, t, hq), fwd), pl.BlockSpec((1, hq, t), fwd_t),
                  pl.BlockSpec((1, t, hv), fwd), pl.BlockSpec((1, ng, t), fwd_t),
                  pl.BlockSpec((1, t, hq), rev), pl.BlockSpec((1, hq, t), rev_t),
                  pl.BlockSpec((1, t, hv), rev), pl.BlockSpec((1, ng, t), rev_t),
                  st_spec, m_spec],
        out_specs=[pl.BlockSpec((1, t, hv), fwd), pl.BlockSpec((1, t, hv), rev), st_spec, m_spec],
        out_shape=[jax.ShapeDtypeStruct((bsz, length, hv), BF16),
                   jax.ShapeDtypeStruct((bsz, length, hv), BF16),
                   jax.ShapeDtypeStruct(st0.shape, F32),
                   jax.ShapeDtypeStruct(m0.shape, F32)],
        compiler_params=_cparams(2, vmem),
        name="mlstm",
    )(q, k_t, v, gif_t, q, k_t, v, gif_t, st0, m0)


def _s5_tables(a_re, a_im, log_dt, b_re, b_im, c_re, c_im, d_skip):
    hp = lax.Precision.HIGHEST
    t = S5_CHUNK
    n_dir, groups, p = a_re.shape
    cg = b_re.shape[-1]
    dt = jnp.exp(log_dt)[..., None]

    def lam_pow(n):
        mag = jnp.exp(n * (dt * a_re)[..., None])
        ang = n * (dt * a_im)[..., None]
        return mag * jnp.cos(ang), mag * jnp.sin(ang)

    ab_re, ab_im = (z[..., 0] for z in lam_pow(jnp.ones((1,), F32)))
    den = a_re * a_re + a_im * a_im
    xr = ab_re - 1.0
    cf_re = (xr * a_re + ab_im * a_im) / den
    cf_im = (ab_im * a_re - xr * a_im) / den
    bb_re = cf_re[..., None] * b_re - cf_im[..., None] * b_im
    bb_im = cf_re[..., None] * b_im + cf_im[..., None] * b_re
    jj = jnp.arange(t + 1, dtype=F32)
    lp_re, lp_im = lam_pow(jj)

    def w_dir(d, exps):
        lr = lp_re[d][:, :, exps]
        li = lp_im[d][:, :, exps]
        wr = lr[..., None] * bb_re[d][:, :, None, :] - li[..., None] * bb_im[d][:, :, None, :]
        wi = lr[..., None] * bb_im[d][:, :, None, :] + li[..., None] * bb_re[d][:, :, None, :]
        to_rows = lambda w: jnp.transpose(w, (0, 2, 3, 1)).reshape(groups, t * cg, p)
        return to_rows(wr), to_rows(wi)

    s_idx = jnp.arange(t)
    wf_re, wf_im = w_dir(0, t - 1 - s_idx)
    wr_re, wr_im = w_dir(1, s_idx)
    w_in = jnp.concatenate([wf_re, wf_im, wr_re, wr_im], axis=-1)

    def c_dir(d, exps):
        lr = lp_re[d][:, :, exps]
        li = lp_im[d][:, :, exps]
        cr = jnp.transpose(c_re[d], (0, 2, 1))
        ci = jnp.transpose(c_im[d], (0, 2, 1))
        o_re = cr[:, :, None, :] * lr[..., None] - ci[:, :, None, :] * li[..., None]
        o_im = cr[:, :, None, :] * li[..., None] + ci[:, :, None, :] * lr[..., None]
        return o_re.reshape(groups, p, t * cg), (-o_im).reshape(groups, p, t * cg)

    cf_r, cf_i = c_dir(0, s_idx + 1)
    cr_r, cr_i = c_dir(1, t - s_idx)
    c_out = jnp.concatenate([cf_r, cf_i, cr_r, cr_i], axis=1)

    def k_dir(d):
        lr = lp_re[d][:, :, :t]
        li = lp_im[d][:, :, :t]
        clr = c_re[d][:, :, :, None] * lr[:, None] - c_im[d][:, :, :, None] * li[:, None]
        cli = c_re[d][:, :, :, None] * li[:, None] + c_im[d][:, :, :, None] * lr[:, None]
        return (jnp.einsum('gqpj,gpc->gjqc', clr, bb_re[d], precision=hp)
                - jnp.einsum('gqpj,gpc->gjqc', cli, bb_im[d], precision=hp))

    kf = k_dir(0)
    kr = k_dir(1)
    lag = s_idx[None, :] - s_idx[:, None]
    resp_f = jnp.where((lag >= 0)[None, :, :, None, None], kf[:, jnp.clip(lag, 0, t - 1)], 0.0)
    resp_r = jnp.where((lag <= 0)[None, :, :, None, None], kr[:, jnp.clip(-lag, 0, t - 1)], 0.0)
    skip = (jnp.eye(t, dtype=F32)[None, :, :, None, None] * jnp.eye(cg, dtype=F32)[None, None, None]
            * d_skip.reshape(groups, 1, 1, cg, 1))
    m_tz = jnp.transpose(resp_f + resp_r + skip, (0, 1, 4, 2, 3)).reshape(groups, t * cg, t * cg)

    n_pow = 8
    kk = (t * (2 ** jnp.arange(n_pow))).astype(F32)
    mp_re, mp_im = lam_pow(kk)
    mp_re = jnp.transpose(mp_re, (0, 1, 3, 2))
    mp_im = jnp.transpose(mp_im, (0, 1, 3, 2))
    mu_a = jnp.concatenate([mp_re[0], mp_re[0], mp_re[1], mp_re[1]], axis=-1)
    mu_b = jnp.concatenate([-mp_im[0], mp_im[0], -mp_im[1], mp_im[1]], axis=-1)
    return w_in.astype(BF16), m_tz.astype(BF16), c_out.astype(BF16), mu_a, mu_b


def _s5_body(x_ref, win_ref, mtz_ref, cout_ref, mua_ref, mub_ref, h0_ref, y_ref, hout_ref, *, n_steps):
    x = x_ref[0, 0]
    nch = x.shape[0]
    p2 = 2 * S5_STATE
    local = jnp.dot(x, win_ref[0], preferred_element_type=F32)
    h0 = h0_ref[0, 0]
    row = lax.broadcasted_iota(jnp.int32, (nch, 1), 0)
    lf = local[:, :p2]
    lr = local[:, p2:]
    zf = jnp.where(row == 0, h0[:, :p2], pltpu.roll(lf, 1, axis=0))
    zr = jnp.where(row == nch - 1, h0[:, p2:], pltpu.roll(lr, nch - 1, axis=0))
    for k in range(n_steps):
        sft = 1 << k
        a_f = mua_ref[0, k:k + 1, :p2]
        b_f = mub_ref[0, k:k + 1, :p2]
        a_r = mua_ref[0, k:k + 1, p2:]
        b_r = mub_ref[0, k:k + 1, p2:]
        pf = jnp.where(row >= sft, pltpu.roll(zf, sft, axis=0), 0.0)
        pr = jnp.where(row < nch - sft, pltpu.roll(zr, nch - sft, axis=0), 0.0)
        zf = zf + a_f * pf + b_f * pltpu.roll(pf, S5_STATE, axis=1)
        zr = zr + a_r * pr + b_r * pltpu.roll(pr, S5_STATE, axis=1)
    z = jnp.concatenate([zf, zr], axis=1)
    y = jnp.dot(x, mtz_ref[0], preferred_element_type=F32)
    y = y + jnp.dot(z.astype(BF16), cout_ref[0], preferred_element_type=F32)
    y_ref[0, 0] = jax.nn.gelu(y).astype(y_ref.dtype)
    a1 = mua_ref[0, 0:1, :]
    b1 = mub_ref[0, 0:1, :]
    zf_l = zf[nch - 1:nch, :]
    zr_l = zr[0:1, :]
    ends = jnp.concatenate([zf_l, zr_l], axis=1)
    ends_sw = jnp.concatenate([pltpu.roll(zf_l, S5_STATE, axis=1), pltpu.roll(zr_l, S5_STATE, axis=1)], axis=1)
    loc_end = jnp.concatenate([lf[nch - 1:nch, :], lr[0:1, :]], axis=1)
    hout_ref[0, 0] = a1 * ends + b1 * ends_sw + loc_end


def _s5(xg, w_in, m_tz, c_out, mu_a, mu_b, h0):
    bsz, groups, nch, tc = xg.shape
    p4 = 4 * S5_STATE
    n_steps = max(1, (nch - 1).bit_length())
    per_g = lambda g, b: (g, 0, 0)
    per_bg = lambda g, b: (b, g, 0, 0)
    return pl.pallas_call(
        functools.partial(_s5_body, n_steps=n_steps),
        grid=(groups, bsz),
        in_specs=[pl.BlockSpec((1, 1, nch, tc), per_bg),
                  pl.BlockSpec((1, tc, p4), per_g),
                  pl.BlockSpec((1, tc, tc), per_g),
                  pl.BlockSpec((1, p4, tc), per_g),
                  pl.BlockSpec((1, mu_a.shape[1], p4), per_g),
                  pl.BlockSpec((1, mu_b.shape[1], p4), per_g),
                  pl.BlockSpec((1, 1, 1, p4), per_bg)],
        out_specs=[pl.BlockSpec((1, 1, nch, tc), per_bg),
                   pl.BlockSpec((1, 1, 1, p4), per_bg)],
        out_shape=[jax.ShapeDtypeStruct((bsz, groups, nch, tc), BF16),
                   jax.ShapeDtypeStruct((bsz, groups, 1, p4), F32)],
        compiler_params=_cparams(2, 16 * 1024 * 1024),
        name="s5",
    )(xg, w_in, m_tz, c_out, mu_a, mu_b, h0)


def _mixout_body(x_ref, hf_ref, hr_ref, ys_ref, mod_ref, g1_ref, g2_ref, gmh_ref, wog_ref, wa_ref, wglu_ref,
                 bglu_ref, wb_ref, bgate_ref, wo_ref, wr_ref, br_ref,
                 x1_ref, h2_ref, ti_ref, tp_ref, rk_ref, cnt_ref, base_ref):
    x = x_ref[0]
    d = x.shape[-1]
    mod = mod_ref[0]
    h = _norm_mod(x, g1_ref[...], mod[1:2], mod[0:1]).astype(BF16)
    og = jnp.dot(h, wog_ref[...], preferred_element_type=F32)
    hm = hf_ref[0].astype(F32) + hr_ref[0].astype(F32)
    heads = []
    for hd in range(N_HEADS):
        blk = hm[:, hd * D_V:(hd + 1) * D_V]
        heads.append(blk * lax.rsqrt(jnp.mean(blk * blk, axis=-1, keepdims=True) + EPS))
    hn = jnp.concatenate(heads, axis=1) * gmh_ref[...]
    y_a = jnp.dot((hn * jax.nn.sigmoid(og[:, :d])).astype(BF16), wa_ref[...], preferred_element_type=F32)
    ys = ys_ref[0]
    glu = jax.nn.sigmoid(jnp.dot(ys, wglu_ref[...], preferred_element_type=F32) + bglu_ref[...])
    y_b = jnp.dot((ys.astype(F32) * glu).astype(BF16), wb_ref[...], preferred_element_type=F32)
    gates = jax.nn.sigmoid(og[:, d:] + bgate_ref[...])
    merged = gates[:, :d] * y_a + gates[:, d:] * y_b
    x1 = x + mod[2:3] * jnp.dot(merged.astype(BF16), wo_ref[...], preferred_element_type=F32)
    x1_ref[0] = x1
    h2f = _norm_mod(x1, g2_ref[...], mod[4:5], mod[3:4])
    h2_ref[0] = h2f
    logits = jnp.dot(h2f.astype(BF16), wr_ref[...], preferred_element_type=F32) + br_ref[...]
    tm, n_e = logits.shape
    e_iota = lax.broadcasted_iota(jnp.int32, (tm, n_e), 1)
    lane = lax.broadcasted_iota(jnp.int32, (tm, ti_ref.shape[-1]), 1)
    ti = jnp.zeros(lane.shape, jnp.int32)
    tv = jnp.zeros(lane.shape, F32)
    top = None
    chosen = []
    for k in range(TOP_K):
        mx = jnp.max(logits, axis=-1, keepdims=True)
        idx = jnp.min(jnp.where(logits == mx, e_iota, n_e), axis=-1, keepdims=True)
        top = mx if top is None else top
        ti = jnp.where(lane == k, idx, ti)
        tv = jnp.where(lane == k, jnp.exp(mx - top), tv)
        chosen.append(e_iota == idx)
        logits = jnp.where(chosen[-1], -jnp.inf, logits)
    ti_ref[0] = ti
    tp_ref[0] = tv / jnp.sum(tv, axis=-1, keepdims=True)

    @pl.when((pl.program_id(0) == 0) & (pl.program_id(1) == 0))
    def _():
        base_ref[...] = jnp.zeros(base_ref.shape, F32)

    onehot = jnp.zeros((tm, n_e), F32)
    for sel in chosen:
        onehot = onehot + jnp.where(sel, 1.0, 0.0)
    below = (lax.broadcasted_iota(jnp.int32, (tm, tm), 1) < lax.broadcasted_iota(jnp.int32, (tm, tm), 0))
    before = jnp.dot(jnp.where(below, 1.0, 0.0).astype(BF16), onehot.astype(BF16),
                     preferred_element_type=F32) + base_ref[...]
    rk = jnp.zeros(lane.shape, jnp.int32)
    for k, sel in enumerate(chosen):
        rank = jnp.sum(jnp.where(sel, before, 0.0), axis=-1, keepdims=True)
        rk = jnp.where(lane == k, rank.astype(jnp.int32), rk)
    rk_ref[0] = rk
    base_ref[...] = base_ref[...] + jnp.sum(onehot, axis=0, keepdims=True)
    cnt_ref[...] = base_ref[...].astype(jnp.int32)


def _mixout(x, h_f, h_r, ys, mod, g1, g2, g_mh, w_og, w_a, w_glu, b_glu, w_b, b_gate, w_o, w_r, b_r, tm):
    bsz, length, d = x.shape
    sw = ys.shape[-1]
    n_e = w_r.shape[-1]
    tok = lambda b, i: (b, i, 0)
    const2 = lambda b, i: (0, 0)

    def wspec(w):
        return pl.BlockSpec(w.shape, const2, pipeline_mode=pl.Buffered(1))

    weights = (w_og, w_a, w_glu, w_b, w_o, w_r)
    w_bytes = sum(int(w.size) * w.dtype.itemsize for w in weights)
    vmem = w_bytes + 2 * tm * (d * 4 + 2 * d * 2 + sw * 2 + d * 4 + d * 2) + 14 * tm * d * 4
    return pl.pallas_call(
        _mixout_body,
        grid=(bsz, length // tm),
        in_specs=[pl.BlockSpec((1, tm, d), tok), pl.BlockSpec((1, tm, d), tok), pl.BlockSpec((1, tm, d), tok),
                  pl.BlockSpec((1, tm, sw), tok), pl.BlockSpec((1, 6, d), lambda b, i: (b, 0, 0)),
                  pl.BlockSpec((1, d), const2), pl.BlockSpec((1, d), const2), pl.BlockSpec((1, d), const2),
                  wspec(w_og), wspec(w_a), wspec(w_glu), pl.BlockSpec((1, sw), const2), wspec(w_b),
                  pl.BlockSpec((1, 2 * d), const2), wspec(w_o), wspec(w_r), pl.BlockSpec((1, n_e), const2)],
        out_specs=[pl.BlockSpec((1, tm, d), tok), pl.BlockSpec((1, tm, d), tok),
                   pl.BlockSpec((1, tm, 8), tok), pl.BlockSpec((1, tm, 8), tok), pl.BlockSpec((1, tm, 8), tok),
                   pl.BlockSpec((1, n_e), const2)],
        out_shape=[jax.ShapeDtypeStruct((bsz, length, d), F32),
                   jax.ShapeDtypeStruct((bsz, length, d), F32),
                   jax.ShapeDtypeStruct((bsz, length, 8), jnp.int32),
                   jax.ShapeDtypeStruct((bsz, length, 8), F32),
                   jax.ShapeDtypeStruct((bsz, length, 8), jnp.int32),
                   jax.ShapeDtypeStruct((1, n_e), jnp.int32)],
        scratch_shapes=[pltpu.VMEM((1, n_e), F32)],
        compiler_params=_cparams(2, vmem),
        name="mixout",
    )(x, h_f, h_r, ys, mod, g1.reshape(1, d), g2.reshape(1, d), g_mh.reshape(1, d), w_og, w_a, w_glu,
      b_glu.reshape(1, sw), w_b, b_gate.reshape(1, 2 * d), w_o, w_r, b_r.reshape(1, n_e))


def _rowpos_body(ti_ref, rk_ref, rs_ref, pos_ref):
    ti = ti_ref[...]
    rk = rk_ref[...]
    n, w = ti.shape
    n_e = rs_ref.shape[-1]
    e_iota = lax.broadcasted_iota(jnp.int32, (n, n_e), 1)
    lane = lax.broadcasted_iota(jnp.int32, (n, w), 1)
    pos = jnp.zeros((n, w), jnp.int32)
    for k in range(TOP_K):
        start = jnp.sum(jnp.where(e_iota == ti[:, k:k + 1], rs_ref[...], 0), axis=-1, keepdims=True)
        pos = jnp.where(lane == k, start + rk[:, k:k + 1], pos)
    pos_ref[...] = pos


def _rowpos(top_i, rank, row_start, tn):
    n, w = top_i.shape
    n_e = row_start.shape[-1]
    return pl.pallas_call(
        _rowpos_body,
        grid=(n // tn,),
        in_specs=[pl.BlockSpec((tn, w), lambda i: (i, 0)), pl.BlockSpec((tn, w), lambda i: (i, 0)),
                  pl.BlockSpec((1, n_e), lambda i: (0, 0))],
        out_specs=pl.BlockSpec((tn, w), lambda i: (i, 0)),
        out_shape=jax.ShapeDtypeStruct((n, w), jnp.int32),
        compiler_params=_cparams(1, 16 * tn * 128 * 4),
        name="rowpos",
    )(top_i, rank, row_start)


def _dispatch_body(pos_ref, h_ref, xs_hbm, sem):
    tm = h_ref.shape[0]

    def issue(t, carry):
        for k in range(TOP_K):
            row = pos_ref[t * TOP_K + k]
            pltpu.make_async_copy(h_ref.at[pl.ds(t, 1), :], xs_hbm.at[pl.ds(row, 1), :], sem).start(priority=k % 2)
        return carry

    lax.fori_loop(0, tm, issue, 0, unroll=4)
    for k in range(TOP_K):
        pltpu.make_async_copy(h_ref, xs_hbm.at[pl.ds(0, tm), :], sem).wait()


def _dispatch(pos_flat, h_rows, n_rows, tm):
    n, w = h_rows.shape
    return pl.pallas_call(
        _dispatch_body,
        grid=(n // tm,),
        in_specs=[pl.BlockSpec((tm * TOP_K,), lambda i: (i,), memory_space=pltpu.SMEM),
                  pl.BlockSpec((tm, w), lambda i: (i, 0))],
        out_specs=pl.BlockSpec(memory_space=pl.ANY),
        out_shape=jax.ShapeDtypeStruct((n_rows, w), h_rows.dtype),
        scratch_shapes=[pltpu.SemaphoreType.DMA(())],
        compiler_params=_cparams(1, 8 * tm * w * 4),
        name="dispatch",
    )(pos_flat, h_rows)


def _experts_body(te_ref, tv_ref, nt_ref, xs_ref, win_ref, bin_ref, wout_ref, bout_ref, y_ref, win_bf, wout_bf,
                  *, f_chunk, cast_rows):
    i = pl.program_id(0)
    e = te_ref[i]
    e_prev = te_ref[jnp.maximum(i - 1, 0)]
    d, f2 = win_bf.shape
    f = f2 // 2

    @pl.when((i == 0) | (e != e_prev))
    def _():
        def cast_in(r, carry):
            rows = pl.ds(pl.multiple_of(r * cast_rows, cast_rows), cast_rows)
            win_bf[rows, :] = win_ref[0, rows, :].astype(BF16)
            return carry

        def cast_out(r, carry):
            rows = pl.ds(pl.multiple_of(r * cast_rows, cast_rows), cast_rows)
            wout_bf[rows, :] = wout_ref[0, rows, :].astype(BF16)
            return carry

        lax.fori_loop(0, d // cast_rows, cast_in, 0)
        lax.fori_loop(0, f // cast_rows, cast_out, 0)

    @pl.when(i < nt_ref[0])
    def _():
        live = lax.broadcasted_iota(jnp.int32, xs_ref.shape, 0) < tv_ref[i]
        x = jnp.where(live, xs_ref[...], 0.0).astype(BF16)
        acc = None
        for c in range(f // f_chunk):
            lo = c * f_chunk
            zg = jnp.dot(x, win_bf[:, lo:lo + f_chunk], preferred_element_type=F32) + bin_ref[0, :, lo:lo + f_chunk]
            zl = (jnp.dot(x, win_bf[:, f + lo:f + lo + f_chunk], preferred_element_type=F32)
                  + bin_ref[0, :, f + lo:f + lo + f_chunk])
            glu = jnp.minimum(zg, SWIGLU_LIMIT)
            lin = jnp.clip(zl, -SWIGLU_LIMIT, SWIGLU_LIMIT)
            act = glu * jax.nn.sigmoid(SWIGLU_ALPHA * glu) * (lin + 1.0)
            part = jnp.dot(act.astype(BF16), wout_bf[lo:lo + f_chunk, :], preferred_element_type=F32)
            acc = part if acc is None else acc + part
        y_ref[...] = acc + bout_ref[0]

    @pl.when(i >= nt_ref[0])
    def _():
        y_ref[...] = jnp.zeros(y_ref.shape, y_ref.dtype)


def _experts(tile_expert, tile_valid, n_tiles, xs, w_e_in, b_e_in, w_e_out, b_e_out, tm):
    rows, d = xs.shape
    n_e, _, f2 = w_e_in.shape
    f = f2 // 2
    nt_max = rows // tm
    row_map = lambda i, te, tv, nt: (jnp.minimum(i, nt[0] - 1), 0)
    exp_map = lambda i, te, tv, nt: (te[i], 0, 0)
    vmem = 2 * (d * f2 + f * d) * 4 + (d * f2 + f * d) * 2 + 4 * tm * d * 2 + 10 * tm * d * 4
    grid_spec = pltpu.PrefetchScalarGridSpec(
        num_scalar_prefetch=3,
        grid=(nt_max,),
        in_specs=[pl.BlockSpec((tm, d), row_map),
                  pl.BlockSpec((1, d, f2), exp_map), pl.BlockSpec((1, 1, f2), exp_map),
                  pl.BlockSpec((1, f, d), exp_map), pl.BlockSpec((1, 1, d), exp_map)],
        out_specs=pl.BlockSpec((tm, d), lambda i, te, tv, nt: (i, 0)),
        scratch_shapes=[pltpu.VMEM((d, f2), BF16), pltpu.VMEM((f, d), BF16)],
    )
    return pl.pallas_call(
        functools.partial(_experts_body, f_chunk=512, cast_rows=128),
        grid_spec=grid_spec,
        out_shape=jax.ShapeDtypeStruct((rows, d), F32),
        compiler_params=_cparams(1, vmem),
        name="experts",
    )(tile_expert, tile_valid, n_tiles, xs, w_e_in, b_e_in.reshape(n_e, 1, f2), w_e_out, b_e_out.reshape(n_e, 1, d))


def _final_body(pos_ref, posn_ref, x1_ref, tp_ref, gt_ref, g_ref, y_hbm, o_ref, buf, sem):
    i = pl.program_id(0)
    n = pl.num_programs(0)
    tm = x1_ref.shape[0]

    def gather(p_ref, slot):
        def issue(t, carry):
            for k in range(TOP_K):
                row = p_ref[t * TOP_K + k]
                pltpu.make_async_copy(y_hbm.at[pl.ds(row, 1), :], buf.at[slot, k, pl.ds(t, 1), :],
                                      sem.at[slot]).start(priority=k % 2)
            return carry

        lax.fori_loop(0, tm, issue, 0, unroll=4)

    @pl.when(i == 0)
    def _():
        gather(pos_ref, 0)

    @pl.when(i + 1 < n)
    def _():
        gather(posn_ref, (i + 1) % 2)

    slot = i % 2
    for k in range(TOP_K):
        pltpu.make_async_copy(y_hbm.at[pl.ds(0, tm), :], buf.at[slot, k], sem.at[slot]).wait()
    moe = None
    for k in range(TOP_K):
        term = tp_ref[:, k:k + 1] * buf[slot, k]
        moe = term if moe is None else moe + term
    x2 = x1_ref[...] + gt_ref[0] * moe
    o_ref[...] = x2 * lax.rsqrt(jnp.mean(x2 * x2, axis=-1, keepdims=True) + EPS) * g_ref[...]


def _final(pos_flat, x1, top_p, gt2, g_final, y_rows, tm):
    n_tok, d = x1.shape
    bsz = gt2.shape[0]
    per_b = n_tok // bsz // tm
    nxt = lambda i: (jnp.minimum(i + 1, n_tok // tm - 1),)
    return pl.pallas_call(
        _final_body,
        grid=(n_tok // tm,),
        in_specs=[pl.BlockSpec((tm * TOP_K,), lambda i: (i,), memory_space=pltpu.SMEM),
                  pl.BlockSpec((tm * TOP_K,), nxt, memory_space=pltpu.SMEM),
                  pl.BlockSpec((tm, d), lambda i: (i, 0)),
                  pl.BlockSpec((tm, top_p.shape[-1]), lambda i: (i, 0)),
                  pl.BlockSpec((1, 1, d), lambda i: (i // per_b, 0, 0)),
                  pl.BlockSpec((1, d), lambda i: (0, 0)),
                  pl.BlockSpec(memory_space=pl.ANY)],
        out_specs=pl.BlockSpec((tm, d), lambda i: (i, 0)),
        out_shape=jax.ShapeDtypeStruct((n_tok, d), F32),
        scratch_shapes=[pltpu.VMEM((2, TOP_K, tm, d), F32), pltpu.SemaphoreType.DMA((2,))],
        compiler_params=_cparams(1, 2 * TOP_K * tm * d * 4 + 4 * tm * d * 4 + 8 * tm * d * 4),
        name="final",
    )(pos_flat, pos_flat, x1, top_p, gt2, g_final.reshape(1, d), y_rows)


def _tile_table(counts, tm, nt_max):
    tiles = (counts + tm - 1) // tm
    tile_end = jnp.cumsum(tiles)
    tile_start = tile_end - tiles
    n_tiles = tile_end[-1]
    tile_ids = jnp.arange(nt_max, dtype=jnp.int32)
    last_e = jnp.sum((n_tiles - 1) >= tile_end).astype(jnp.int32)
    te = jnp.sum(tile_ids[:, None] >= tile_end[None, :], axis=1).astype(jnp.int32)
    te = jnp.where(tile_ids < n_tiles, te, last_e)
    sel = te[:, None] == jnp.arange(counts.shape[0], dtype=jnp.int32)[None, :]
    cnt_t = jnp.sum(jnp.where(sel, counts[None, :], 0), axis=1)
    start_t = jnp.sum(jnp.where(sel, tile_start[None, :], 0), axis=1)
    live = jnp.clip(cnt_t - (tile_ids - start_t) * tm, 0, tm).astype(jnp.int32)
    live = jnp.where(tile_ids < n_tiles, live, 0)
    return (tile_start * tm).astype(jnp.int32), te, live, n_tiles.reshape(1).astype(jnp.int32)


def kernel(x, c, ctx, c_ctx, w_ada, b_ada, g_norm1, g_norm2, w_in, w_conv_qk, b_ifgate, g_mh, w_branch_m,
           s5_a_re, s5_a_im, s5_log_dt, s5_b_re, s5_b_im, s5_c_re, s5_c_im, s5_d, w_glu, b_glu, w_branch_s,
           b_merge_gate, w_o, w_router, b_router, w_e_in, b_e_in, w_e_out, b_e_out, g_final):
    bsz, length, d = x.shape
    l_ctx = ctx.shape[1]
    n_qk = 2 * N_HEADS * D_QK
    n_v = N_HEADS * D_V
    n_if = 4 * N_HEADS
    n_u = S5_GROUPS * S5_GROUP
    off_if = n_qk + n_v
    off_u = off_if + n_if
    off_o = off_u + n_u
    layer = 0

    pad_rows = -(bsz + 1) % 8
    c_rows = jnp.concatenate([c, c_ctx[None, :], jnp.zeros((pad_rows, d), F32)], axis=0)
    mod_all = _ada(c_rows, w_ada[layer], b_ada[layer])
    mod = mod_all[:bsz].reshape(bsz, 6, d)
    mod_c = mod_all[bsz, :2 * d].reshape(2, 1, 1, d)

    w_l = w_in[layer]
    w_state = jnp.concatenate([w_l[:, :off_if], w_l[:, off_u:off_o]], axis=1).astype(BF16)
    w_if_t = w_l[:, off_if:off_u].T.astype(BF16)
    b_if = b_ifgate[layer].reshape(n_if)
    proj_c = _proj(ctx, g_norm1[layer], mod_c[1], mod_c[0], w_state, w_if_t, b_if, n_qk, n_v, n_u, l_ctx)
    proj_l = _proj(x, g_norm1[layer], mod[:, 1:2], mod[:, 0:1], w_state, w_if_t, b_if, n_qk, n_v, n_u, 512)
    qk_c, v_c, u_c, gif_c = proj_c
    qk_l, v_l, u_l, gif_l = proj_l

    w9 = w_conv_qk[layer].reshape(9, n_qk)
    q_c, kt_c = _conv(qk_c, w9, D_QK ** -0.5, l_ctx, l_ctx)
    q_l, kt_l = _conv(qk_l, w9, D_QK ** -0.5, GRID_W, 512)

    st0 = jnp.zeros((bsz, 2 * N_HEADS, D_QK, 2 * D_V), F32)
    m0 = jnp.zeros((bsz, 2 * N_HEADS, 1), F32)
    _, _, st_c, m_c = _mlstm(q_c, kt_c, v_c, gif_c, st0, m0)
    h_f, h_r, _, _ = _mlstm(q_l, kt_l, v_l, gif_l, st_c, m_c)

    tables = _s5_tables(s5_a_re[layer], s5_a_im[layer], s5_log_dt[layer], s5_b_re[layer], s5_b_im[layer],
                        s5_c_re[layer], s5_c_im[layer], s5_d[layer])

    def to_chunks(u):
        n = u.shape[1] // S5_CHUNK
        u5 = u.reshape(bsz, n, S5_CHUNK, S5_GROUPS, S5_GROUP)
        return jnp.transpose(u5, (0, 3, 1, 2, 4)).reshape(bsz, S5_GROUPS, n, S5_CHUNK * S5_GROUP)

    hs0 = jnp.zeros((bsz, S5_GROUPS, 1, 4 * S5_STATE), F32)
    _, hs_c = _s5(to_chunks(u_c), *tables, hs0)
    y_g, _ = _s5(to_chunks(u_l), *tables, hs_c)
    n_ch = length // S5_CHUNK
    ys = jnp.transpose(y_g.reshape(bsz, S5_GROUPS, n_ch, S5_CHUNK, S5_GROUP), (0, 2, 3, 1, 4)).reshape(bsz, length, n_u)

    w_og = w_l[:, off_o:].astype(BF16)
    x1, h2, top_i, top_p, rank, counts = _mixout(
        x, h_f, h_r, ys, mod, g_norm1[layer], g_norm2[layer], g_mh[layer], w_og, w_branch_m[layer].astype(BF16),
        w_glu[layer].astype(BF16), b_glu[layer], w_branch_s[layer].astype(BF16), b_merge_gate[layer],
        w_o[layer].astype(BF16), w_router[layer].astype(BF16), b_router[layer], 512)

    tm_e = 512
    n_tok = bsz * length
    nt_max = n_tok * TOP_K // tm_e + N_EXPERTS
    row_start, tile_expert, tile_live, n_tiles = _tile_table(counts.reshape(N_EXPERTS), tm_e, nt_max)
    pos = _rowpos(top_i.reshape(n_tok, 8), rank.reshape(n_tok, 8), row_start.reshape(1, N_EXPERTS), min(4096, n_tok))
    pos_flat = pos[:, :TOP_K].reshape(n_tok * TOP_K)
    xs = _dispatch(pos_flat, h2.reshape(n_tok, d), nt_max * tm_e, 512)
    y_rows = _experts(tile_expert, tile_live, n_tiles, xs, w_e_in[layer], b_e_in[layer], w_e_out[layer],
                      b_e_out[layer], tm_e)
    out = _final(pos_flat, x1.reshape(n_tok, d), top_p.reshape(n_tok, 8), mod[:, 5:6], g_final, y_rows, 256)
    return out.reshape(bsz, length, d)
```

```python
import functools
import math

import jax
import jax.numpy as jnp
from jax import lax
from jax.experimental import pallas as pl
from jax.experimental.pallas import tpu as pltpu

F32 = jnp.float32
BF16 = jnp.bfloat16
EPS = 1e-6

N_HEADS = 8
D_QK = 64
D_V = 128
M_CHUNK = 128
GRID_W = 64
S5_GROUPS = 32
S5_GROUP = 16
S5_STATE = 64
S5_CHUNK = 16
N_EXPERTS = 32
TOP_K = 4
SWIGLU_LIMIT = 7.0
SWIGLU_ALPHA = 1.702

LANES = 128
V7X_VMEM_BYTES = 64 * 1024 * 1024
_VMEM_CAP = V7X_VMEM_BYTES - 8 * 1024 * 1024


def _cparams(n_axes, vmem_bytes):
    limit = int(min(_VMEM_CAP, max(32 * 1024 * 1024, vmem_bytes)))
    return pltpu.CompilerParams(dimension_semantics=("arbitrary",) * n_axes, vmem_limit_bytes=limit)


def _silu(x):
    return x * jax.nn.sigmoid(x)


def _norm_mod(x, g, scale, shift):
    ms = jnp.mean(x * x, axis=-1, keepdims=True)
    return (x * lax.rsqrt(ms + EPS) * g) * (1.0 + scale) + shift


def _split3(x):
    hi = x.astype(BF16)
    r1 = x - hi.astype(F32)
    mid = r1.astype(BF16)
    lo = (r1 - mid.astype(F32)).astype(BF16)
    return hi, mid, lo


def _dot_nt(a, b):
    return lax.dot_general(a, b, (((1,), (1,)), ((), ())), preferred_element_type=F32)


def _dot_tn(a, b):
    return lax.dot_general(a, b, (((0,), (0,)), ((), ())), preferred_element_type=F32)


def _ada_body(c_ref, w_ref, b_ref, o_ref):
    s = _silu(c_ref[...])
    o_ref[...] = jnp.dot(s, w_ref[...], preferred_element_type=F32,
                         precision=lax.Precision.HIGHEST) + b_ref[...]


def _ada(c_rows, w_ada, b_ada):
    rows, d = c_rows.shape
    n = w_ada.shape[1]
    tn = 1024
    return pl.pallas_call(
        _ada_body,
        grid=(n // tn,),
        in_specs=[pl.BlockSpec((rows, d), lambda j: (0, 0)),
                  pl.BlockSpec((d, tn), lambda j: (0, j)),
                  pl.BlockSpec((1, tn), lambda j: (0, j))],
        out_specs=pl.BlockSpec((rows, tn), lambda j: (0, j)),
        out_shape=jax.ShapeDtypeStruct((rows, n), F32),
        compiler_params=_cparams(1, 4 * d * tn * 4),
        name="ada",
    )(c_rows, w_ada, b_ada.reshape(1, n))


def _proj_body(x_ref, g_ref, sc_ref, sh_ref, w_ref, wif_ref, bif_ref, qk_ref, v_ref, u_ref, gif_ref):
    h = _norm_mod(x_ref[0], g_ref[...], sc_ref[0], sh_ref[0]).astype(BF16)
    n_qk = qk_ref.shape[-1]
    n_v = v_ref.shape[-1]
    qk_ref[0] = jnp.dot(h, w_ref[:, :n_qk], preferred_element_type=F32).astype(BF16)
    v_ref[0] = jnp.dot(h, w_ref[:, n_qk:n_qk + n_v], preferred_element_type=F32).astype(BF16)
    u = jnp.dot(h, w_ref[:, n_qk + n_v:], preferred_element_type=F32)
    for j in range(u_ref.shape[1]):
        u_ref[0, j] = u[:, j * LANES:(j + 1) * LANES]
    gif_ref[0] = _dot_nt(wif_ref[...], h) + bif_ref[...]


def _proj(x, g, scale, shift, w_state, w_if_t, b_if, n_qk, n_v, n_u, tm):
    bsz, length, d = x.shape
    n_if = b_if.shape[-1]
    per_batch = scale.shape[0] == bsz
    mod_map = (lambda b, i: (b, 0, 0)) if per_batch else (lambda b, i: (0, 0, 0))
    cols = w_state.shape[1]
    vmem = 2 * (tm * d * 4 + d * cols * 2 + tm * (n_qk + n_v + n_u) * 2 + tm * 128 * 4) + 6 * tm * d * 4
    return pl.pallas_call(
        _proj_body,
        grid=(bsz, length // tm),
        in_specs=[pl.BlockSpec((1, tm, d), lambda b, i: (b, i, 0)),
                  pl.BlockSpec((1, d), lambda b, i: (0, 0)),
                  pl.BlockSpec((1, 1, d), mod_map),
                  pl.BlockSpec((1, 1, d), mod_map),
                  pl.BlockSpec((d, cols), lambda b, i: (0, 0)),
                  pl.BlockSpec((n_if, d), lambda b, i: (0, 0)),
                  pl.BlockSpec((n_if, 1), lambda b, i: (0, 0))],
        out_specs=[pl.BlockSpec((1, tm, n_qk), lambda b, i: (b, i, 0)),
                   pl.BlockSpec((1, tm, n_v), lambda b, i: (b, i, 0)),
                   pl.BlockSpec((1, n_u // LANES, tm, LANES), lambda b, i: (b, 0, i, 0)),
                   pl.BlockSpec((1, n_if, tm), lambda b, i: (b, 0, i))],
        out_shape=[jax.ShapeDtypeStruct((bsz, length, n_qk), BF16),
                   jax.ShapeDtypeStruct((bsz, length, n_v), BF16),
                   jax.ShapeDtypeStruct((bsz, n_u // LANES, length, LANES), F32),
                   jax.ShapeDtypeStruct((bsz, n_if, length), F32)],
        compiler_params=_cparams(2, vmem),
        name="proj",
    )(x, g.reshape(1, d), scale, shift, w_state, w_if_t, b_if.reshape(n_if, 1))


def _conv_body(main_ref, prev_ref, next_ref, w_ref, q_ref, kt_ref, *, width, q_scale):
    i = pl.program_id(1)
    last = pl.num_programs(1) - 1
    t = main_ref.shape[1]
    n = t + 2 * width
    main = main_ref[0].astype(F32)
    prev = jnp.where(i > 0, prev_ref[0].astype(F32), 0.0)
    nxt = jnp.where(i < last, next_ref[0].astype(F32), 0.0)
    ext = jnp.concatenate([prev, main, nxt], axis=0)
    col = lax.broadcasted_iota(jnp.int32, (t, 1), 0) % width
    acc = None
    for dx in (-1, 0, 1):
        shifted = ext if dx == 0 else pltpu.roll(ext, (-dx) % n, axis=0)
        part = None
        for dy in (-1, 0, 1):
            tap = w_ref[(dy + 1) * 3 + (dx + 1):(dy + 1) * 3 + (dx + 1) + 1, :]
            term = tap * shifted[width + dy * width:width + dy * width + t]
            part = term if part is None else part + term
        if dx == -1:
            part = jnp.where(col == 0, 0.0, part)
        elif dx == 1:
            part = jnp.where(col == width - 1, 0.0, part)
        acc = part if acc is None else acc + part
    y = _silu(acc)

    @pl.when(pl.program_id(2) == 0)
    def _():
        q_ref[0] = (y * q_scale).astype(q_ref.dtype)

    @pl.when(pl.program_id(2) == 1)
    def _():
        kt_ref[0] = y.T.astype(kt_ref.dtype)


def _conv(qk_pre, w9, q_scale, width, t_block):
    bsz, length, ch2 = qk_pre.shape
    ch = ch2 // 2
    rpb = t_block // width
    n_rows = length // width
    vmem = 4 * (t_block + 2 * width) * ch * 2 + 14 * (t_block + 2 * width) * ch * 4
    return pl.pallas_call(
        functools.partial(_conv_body, width=width, q_scale=q_scale),
        grid=(bsz, length // t_block, 2),
        in_specs=[pl.BlockSpec((1, t_block, ch), lambda b, i, c: (b, i, c)),
                  pl.BlockSpec((1, width, ch), lambda b, i, c: (b, jnp.maximum(i * rpb - 1, 0), c)),
                  pl.BlockSpec((1, width, ch), lambda b, i, c: (b, jnp.minimum((i + 1) * rpb, n_rows - 1), c)),
                  pl.BlockSpec((9, ch), lambda b, i, c: (0, c))],
        out_specs=[pl.BlockSpec((1, t_block, ch), lambda b, i, c: (b, i, 0)),
                   pl.BlockSpec((1, ch, t_block), lambda b, i, c: (b, 0, i))],
        out_shape=[jax.ShapeDtypeStruct((bsz, length, ch), BF16),
                   jax.ShapeDtypeStruct((bsz, ch, length), BF16)],
        compiler_params=_cparams(3, vmem),
        name="conv",
    )(qk_pre, qk_pre, qk_pre, w9)


def _log_sigmoid(x):
    return jnp.minimum(x, 0.0) - jnp.log1p(jnp.exp(-jnp.abs(x)))


def _cumsum_lanes_exact(x, reverse):
    t = x.shape[-1]
    r_idx = lax.broadcasted_iota(jnp.int32, (t, t), 0)
    c_idx = lax.broadcasted_iota(jnp.int32, (t, t), 1)
    u01 = jnp.where((r_idx >= c_idx) if reverse else (r_idx <= c_idx), 1.0, 0.0).astype(BF16)
    out = None
    for piece in _split3(x):
        p = jnp.dot(piece, u01, preferred_element_type=F32)
        out = p if out is None else out + p
    return out


def _cummax_lanes(x, reverse):
    t = x.shape[-1]
    lane = lax.broadcasted_iota(jnp.int32, x.shape, 1)
    sh = 1
    while sh < t:
        if reverse:
            cand = jnp.where(lane < t - sh, pltpu.roll(x, t - sh, axis=1), -jnp.inf)
        else:
            cand = jnp.where(lane >= sh, pltpu.roll(x, sh, axis=1), -jnp.inf)
        x = jnp.maximum(x, cand)
        sh *= 2
    return x


def _mlstm_gate_rows(g_ref, m_col, d):
    t = g_ref.shape[-1]
    base = d * 2 * N_HEADS
    li = g_ref[0, base:base + N_HEADS, :]
    lf = _log_sigmoid(g_ref[0, base + N_HEADS:base + 2 * N_HEADS, :])
    b = _cumsum_lanes_exact(lf, reverse=(d == 1))
    g = li - b
    a = jnp.maximum(_cummax_lanes(g, reverse=(d == 1)), m_col)
    end = 0 if d == 1 else t - 1
    a_end = a[:, end:end + 1]
    return dict(g=g, a=a, ie=jnp.exp(m_col - a), emt=jnp.exp(-b - a), we=jnp.exp(g - a_end),
                dec=jnp.exp(m_col - a_end), m_new=b[:, end:end + 1] + a_end)


def _mlstm_body(qf_ref, kf_ref, vf_ref, gf_ref, qr_ref, kr_ref, vr_ref, gr_ref, st0_ref, m0_ref,
                hf_ref, hr_ref, st_ref, m_ref):
    @pl.when(pl.program_id(1) == 0)
    def _():
        st_ref[...] = st0_ref[...]
        m_ref[...] = m0_ref[...]

    t = qf_ref.shape[1]
    m_all = m_ref[0]
    rows = [_mlstm_gate_rows(g_ref, m_all[d * N_HEADS:(d + 1) * N_HEADS], d) for d, g_ref in ((0, gf_ref), (1, gr_ref))]
    both = lambda name: jnp.concatenate([rows[0][name], rows[1][name]], axis=0)
    n_hd = 2 * N_HEADS
    pad = jnp.zeros((128 - 3 * n_hd, t), F32)
    cols = jnp.concatenate([both('a'), both('ie'), both('emt'), pad], axis=0).T
    m_ref[0] = both('m_new')
    r_idx = lax.broadcasted_iota(jnp.int32, (t, t), 0)
    c_idx = lax.broadcasted_iota(jnp.int32, (t, t), 1)
    ones_blk = jnp.ones((t, D_V), BF16)
    for d, (q_ref, kt_ref, v_ref, h_ref) in enumerate(((qf_ref, kf_ref, vf_ref, hf_ref), (qr_ref, kr_ref, vr_ref, hr_ref))):
        causal = (c_idx >= r_idx) if d == 1 else (c_idx <= r_idx)
        for h in range(N_HEADS):
            j = d * N_HEADS + h
            q = q_ref[0, :, h * D_QK:(h + 1) * D_QK]
            kt = kt_ref[0, h * D_QK:(h + 1) * D_QK, :]
            v1 = jnp.concatenate([v_ref[0, :, h * D_V:(h + 1) * D_V], ones_blk], axis=1)
            state = st_ref[0, j]
            a_col = cols[:, j:j + 1]
            ie_col = cols[:, n_hd + j:n_hd + j + 1]
            emt_col = cols[:, 2 * n_hd + j:2 * n_hd + j + 1]
            dmat = jnp.exp(jnp.where(causal, rows[d]['g'][h:h + 1, :] - a_col, -jnp.inf))
            s = (jnp.dot(q, kt, preferred_element_type=F32) * dmat).astype(BF16)
            z = jnp.dot(q, state.astype(BF16), preferred_element_type=F32)
            p = jnp.dot(s, v1, preferred_element_type=F32)
            num = ie_col * z[:, :D_V] + p[:, :D_V]
            den = ie_col * z[:, D_V:] + p[:, D_V:]
            h_ref[0, :, h * D_V:(h + 1) * D_V] = (num / jnp.maximum(jnp.abs(den), emt_col)).astype(h_ref.dtype)
            kw = (kt.astype(F32) * rows[d]['we'][h:h + 1, :]).astype(BF16)
            st_ref[0, j] = rows[d]['dec'][h:h + 1, :] * state + jnp.dot(kw, v1, preferred_element_type=F32)


def _mlstm(q, k_t, v, gif_t, st0, m0):
    bsz, length, hv = v.shape
    t = M_CHUNK
    nc = length // t
    hq = N_HEADS * D_QK
    ng = gif_t.shape[1]
    fwd = lambda b, i: (b, i, 0)
    rev = lambda b, i: (b, nc - 1 - i, 0)
    fwd_t = lambda b, i: (b, 0, i)
    rev_t = lambda b, i: (b, 0, nc - 1 - i)
    st_spec = pl.BlockSpec((1,) + st0.shape[1:], lambda b, i: (b, 0, 0, 0))
    m_spec = pl.BlockSpec((1,) + m0.shape[1:], lambda b, i: (b, 0, 0))
    vmem = 24 * 1024 * 1024
    return pl.pallas_call(
        _mlstm_body,
        grid=(bsz, nc),
        in_specs=[pl.BlockSpec((1, t, hq), fwd), pl.BlockSpec((1, hq, t), fwd_t),
                  pl.BlockSpec((1, t, hv), fwd), pl.BlockSpec((1, ng, t), fwd_t),
                  pl.BlockSpec((1, t, hq), rev), pl.BlockSpec((1, hq, t), rev_t),
                  pl.BlockSpec((1, t, hv), rev), pl.BlockSpec((1, ng, t), rev_t),
                  st_spec, m_spec],
        out_specs=[pl.BlockSpec((1, t, hv), fwd), pl.BlockSpec((1, t, hv), rev), st_spec, m_spec],
        out_shape=[jax.ShapeDtypeStruct((bsz, length, hv), BF16),
                   jax.ShapeDtypeStruct((bsz, length, hv), BF16),
                   jax.ShapeDtypeStruct(st0.shape, F32),
                   jax.ShapeDtypeStruct(m0.shape, F32)],
        compiler_params=_cparams(2, vmem),
        name="mlstm",
    )(q, k_t, v, gif_t, q, k_t, v, gif_t, st0, m0)


def _s5_tables(a_re, a_im, log_dt, b_re, b_im, c_re, c_im, d_skip):
    hp = lax.Precision.HIGHEST
    t = S5_CHUNK
    n_dir, groups, p = a_re.shape
    cg = b_re.shape[-1]
    dt = jnp.exp(log_dt)[..., None]

    def lam_pow(n):
        mag = jnp.exp(n * (dt * a_re)[..., None])
        ang = n * (dt * a_im)[..., None]
        return mag * jnp.cos(ang), mag * jnp.sin(ang)

    ab_re, ab_im = (z[..., 0] for z in lam_pow(jnp.ones((1,), F32)))
    den = a_re * a_re + a_im * a_im
    xr = ab_re - 1.0
    cf_re = (xr * a_re + ab_im * a_im) / den
    cf_im = (ab_im * a_re - xr * a_im) / den
    bb_re = cf_re[..., None] * b_re - cf_im[..., None] * b_im
    bb_im = cf_re[..., None] * b_im + cf_im[..., None] * b_re
    jj = jnp.arange(t + 1, dtype=F32)
    lp_re, lp_im = lam_pow(jj)

    def w_dir(d, exps):
        lr = lp_re[d][:, :, exps]
        li = lp_im[d][:, :, exps]
        wr = lr[..., None] * bb_re[d][:, :, None, :] - li[..., None] * bb_im[d][:, :, None, :]
        wi = lr[..., None] * bb_im[d][:, :, None, :] + li[..., None] * bb_re[d][:, :, None, :]
        to_rows = lambda w: jnp.transpose(w, (0, 2, 3, 1)).reshape(groups, t * cg, p)
        return to_rows(wr), to_rows(wi)

    s_idx = jnp.arange(t)
    wf_re, wf_im = w_dir(0, t - 1 - s_idx)
    wr_re, wr_im = w_dir(1, s_idx)
    w_in = jnp.concatenate([wf_re, wf_im, wr_re, wr_im], axis=-1)

    def c_dir(d, exps):
        lr = lp_re[d][:, :, exps]
        li = lp_im[d][:, :, exps]
        cr = jnp.transpose(c_re[d], (0, 2, 1))
        ci = jnp.transpose(c_im[d], (0, 2, 1))
        o_re = cr[:, :, None, :] * lr[..., None] - ci[:, :, None, :] * li[..., None]
        o_im = cr[:, :, None, :] * li[..., None] + ci[:, :, None, :] * lr[..., None]
        return o_re.reshape(groups, p, t * cg), (-o_im).reshape(groups, p, t * cg)

    cf_r, cf_i = c_dir(0, s_idx + 1)
    cr_r, cr_i = c_dir(1, t - s_idx)
    c_out = jnp.concatenate([cf_r, cf_i, cr_r, cr_i], axis=1)

    def k_dir(d):
        lr = lp_re[d][:, :, :t]
        li = lp_im[d][:, :, :t]
        clr = c_re[d][:, :, :, None] * lr[:, None] - c_im[d][:, :, :, None] * li[:, None]
        cli = c_re[d][:, :, :, None] * li[:, None] + c_im[d][:, :, :, None] * lr[:, None]
        return (jnp.einsum('gqpj,gpc->gjqc', clr, bb_re[d], precision=hp)
                - jnp.einsum('gqpj,gpc->gjqc', cli, bb_im[d], precision=hp))

    kf = k_dir(0)
    kr = k_dir(1)
    lag = s_idx[None, :] - s_idx[:, None]
    resp_f = jnp.where((lag >= 0)[None, :, :, None, None], kf[:, jnp.clip(lag, 0, t - 1)], 0.0)
    resp_r = jnp.where((lag <= 0)[None, :, :, None, None], kr[:, jnp.clip(-lag, 0, t - 1)], 0.0)
    skip = (jnp.eye(t, dtype=F32)[None, :, :, None, None] * jnp.eye(cg, dtype=F32)[None, None, None]
            * d_skip.reshape(groups, 1, 1, cg, 1))
    m_tz = jnp.transpose(resp_f + resp_r + skip, (0, 1, 4, 2, 3)).reshape(groups, t * cg, t * cg)

    n_pow = 8
    kk = (t * (2 ** jnp.arange(n_pow))).astype(F32)
    mp_re, mp_im = lam_pow(kk)
    mp_re = jnp.transpose(mp_re, (0, 1, 3, 2))
    mp_im = jnp.transpose(mp_im, (0, 1, 3, 2))
    mu_a = jnp.concatenate([mp_re[0], mp_re[0], mp_re[1], mp_re[1]], axis=-1)
    mu_b = jnp.concatenate([-mp_im[0], mp_im[0], -mp_im[1], mp_im[1]], axis=-1)
    tr = lambda m: jnp.transpose(m, (0, 2, 1)).astype(BF16)
    return tr(w_in), tr(m_tz), tr(c_out), mu_a, mu_b


def _pad_rows(x, rows):
    return x if x.shape[0] == rows else jnp.concatenate([x, jnp.zeros((rows - x.shape[0],) + x.shape[1:], x.dtype)], axis=0)


def _s5_body(*refs, nch, n_steps, with_output):
    if with_output:
        (u_ref, h0_ref, wint_ref, mtzt_ref, coutt_ref, mua_ref, mub_ref, wglut_ref, bglu_ref, wbt_ref,
         yb_ref, hout_ref, xt_scr, yt_scr) = refs
    else:
        u_ref, h0_ref, wint_ref, mua_ref, mub_ref, hout_ref, xt_scr = refs
    t_c, cg, groups = S5_CHUNK, S5_GROUP, S5_GROUPS
    p2 = 2 * S5_STATE
    p4 = 2 * p2
    ncp = xt_scr.shape[-1]
    for t in range(t_c):
        ut = jnp.concatenate([u_ref[0, j, pl.ds(t, nch, stride=t_c), :] for j in range(u_ref.shape[1])], axis=1)
        ut = _pad_rows(ut, ncp)
        xt_scr[:, t] = ut.T.astype(BF16).reshape(groups, cg, ncp)

    h0t = _pad_rows(h0_ref[0], 128).T
    g_lane = lax.broadcasted_iota(jnp.int32, (p4, 128), 1)
    lane = lax.broadcasted_iota(jnp.int32, (p2, ncp), 1)
    swap = lambda z: jnp.concatenate([z[S5_STATE:], z[:S5_STATE]], axis=0)

    def group(g, hcol):
        x = xt_scr[g].reshape(t_c * cg, ncp)
        local = jnp.dot(wint_ref[g], x, preferred_element_type=F32)
        h0c = jnp.sum(jnp.where(g_lane == g, h0t, 0.0), axis=1, keepdims=True)
        mua = _pad_rows(mua_ref[g], 128).T
        mub = _pad_rows(mub_ref[g], 128).T
        lf, lr = local[:p2], local[p2:]
        zf = jnp.where(lane == 0, h0c[:p2], pltpu.roll(lf, 1, axis=1))
        zr = jnp.where(lane == nch - 1, h0c[p2:], jnp.where(lane < nch - 1, pltpu.roll(lr, ncp - 1, axis=1), 0.0))
        for k in range(n_steps):
            sft = 1 << k
            pf = jnp.where(lane >= sft, pltpu.roll(zf, sft, axis=1), 0.0)
            pr = jnp.where(lane < ncp - sft, pltpu.roll(zr, ncp - sft, axis=1), 0.0)
            zf = zf + mua[:p2, k:k + 1] * pf + mub[:p2, k:k + 1] * swap(pf)
            zr = zr + mua[p2:, k:k + 1] * pr + mub[p2:, k:k + 1] * swap(pr)
        if with_output:
            z = jnp.concatenate([zf, zr], axis=0).astype(BF16)
            y = (jnp.dot(mtzt_ref[g], x, preferred_element_type=F32)
                 + jnp.dot(coutt_ref[g], z, preferred_element_type=F32))
            yt_scr[g] = jax.nn.gelu(y).astype(BF16).reshape(t_c, cg, ncp)
        zf_l, zr_l = zf[:, nch - 1:nch], zr[:, 0:1]
        ends = jnp.concatenate([zf_l, zr_l], axis=0)
        ends_sw = jnp.concatenate([swap(zf_l), swap(zr_l)], axis=0)
        loc_end = jnp.concatenate([lf[:, nch - 1:nch], lr[:, 0:1]], axis=0)
        hend = mua[:, 0:1] * ends + mub[:, 0:1] * ends_sw + loc_end
        return jnp.where(g_lane == g, hend, hcol)

    hcol = lax.fori_loop(0, groups, group, jnp.zeros((p4, 128), F32))
    hout_ref[0] = hcol.T[:groups]

    if with_output:
        def tail(t, carry):
            ys = yt_scr[:, t].reshape(groups * cg, ncp)
            glu = jax.nn.sigmoid(jnp.dot(wglut_ref[...], ys, preferred_element_type=F32) + bglu_ref[...])
            ybt = jnp.dot(wbt_ref[...], (ys.astype(F32) * glu).astype(BF16), preferred_element_type=F32)
            yb_ref[0, t] = ybt.T[:nch].astype(yb_ref.dtype)
            return carry

        lax.fori_loop(0, t_c, tail, 0)


def _s5(u, h0, w_in_t, m_tz_t, c_out_t, mu_a, mu_b, w_glu_t=None, b_glu=None, w_b_t=None):
    bsz, n_blk, length, _ = u.shape
    width = n_blk * LANES
    with_output = w_glu_t is not None
    nch = length // S5_CHUNK
    ncp = -(-nch // 128) * 128
    p4 = 4 * S5_STATE
    n_steps = max(1, (nch - 1).bit_length())
    tc = S5_CHUNK * S5_GROUP
    one = pl.Buffered(1)
    full = lambda a: pl.BlockSpec(a.shape, lambda b: (0,) * a.ndim, pipeline_mode=one)
    in_specs = [pl.BlockSpec((1, n_blk, length, LANES), lambda b: (b, 0, 0, 0), pipeline_mode=one),
                pl.BlockSpec((1, S5_GROUPS, p4), lambda b: (b, 0, 0)), full(w_in_t)]
    args = [u, h0, w_in_t]
    scratch = [pltpu.VMEM((S5_GROUPS, S5_CHUNK, S5_GROUP, ncp), BF16)]
    out_specs = [pl.BlockSpec((1, S5_GROUPS, p4), lambda b: (b, 0, 0))]
    out_shape = [jax.ShapeDtypeStruct((bsz, S5_GROUPS, p4), F32)]
    vmem = length * width * 4 + 3 * S5_GROUPS * tc * tc * 2 + 2 * S5_GROUPS * tc * ncp * 2 + 8 * 1024 * 1024
    if with_output:
        d_out = w_b_t.shape[0]
        in_specs += [full(m_tz_t), full(c_out_t), full(mu_a), full(mu_b), full(w_glu_t),
                     pl.BlockSpec((width, 1), lambda b: (0, 0)), full(w_b_t)]
        args += [m_tz_t, c_out_t, mu_a, mu_b, w_glu_t, b_glu.reshape(width, 1), w_b_t]
        scratch.append(pltpu.VMEM((S5_GROUPS, S5_CHUNK, S5_GROUP, ncp), BF16))
        out_specs.insert(0, pl.BlockSpec((1, S5_CHUNK, nch, d_out), lambda b: (b, 0, 0, 0)))
        out_shape.insert(0, jax.ShapeDtypeStruct((bsz, S5_CHUNK, nch, d_out), BF16))
        vmem += 2 * S5_CHUNK * nch * d_out * 2 + (width * width + width * d_out) * 2
    else:
        in_specs += [full(mu_a), full(mu_b)]
        args += [mu_a, mu_b]
    return pl.pallas_call(
        functools.partial(_s5_body, nch=nch, n_steps=n_steps, with_output=with_output),
        grid=(bsz,),
        in_specs=in_specs,
        out_specs=out_specs,
        out_shape=out_shape,
        scratch_shapes=scratch,
        compiler_params=_cparams(1, vmem),
        name="s5",
    )(*args)


def _mixout_body(x_ref, hf_ref, hr_ref, yb_ref, mod_ref, g1_ref, g2_ref, gmh_ref, wog_ref, wa_ref,
                 bgate_ref, wo_ref, wr_ref, br_ref,
                 x1_ref, h2_ref, ti_ref, tp_ref, rk_ref, cnt_ref, base_ref, yb_scr):
    x = x_ref[0]
    d = x.shape[-1]
    mod = mod_ref[0]
    h = _norm_mod(x, g1_ref[...], mod[1:2], mod[0:1]).astype(BF16)
    og = jnp.dot(h, wog_ref[...], preferred_element_type=F32)
    hm = hf_ref[0].astype(F32) + hr_ref[0].astype(F32)
    heads = []
    for hd in range(N_HEADS):
        blk = hm[:, hd * D_V:(hd + 1) * D_V]
        heads.append(blk * lax.rsqrt(jnp.mean(blk * blk, axis=-1, keepdims=True) + EPS))
    hn = jnp.concatenate(heads, axis=1) * gmh_ref[...]
    y_a = jnp.dot((hn * jax.nn.sigmoid(og[:, :d])).astype(BF16), wa_ref[...], preferred_element_type=F32)
    n_pos = yb_ref.shape[1]
    for t in range(n_pos):
        blk = yb_ref[0, t].astype(F32)
        for j in range(yb_scr.shape[0]):
            yb_scr[j, pl.ds(t, yb_ref.shape[2], stride=n_pos), :] = blk[:, j * LANES:(j + 1) * LANES]
    y_b = jnp.concatenate([yb_scr[j] for j in range(yb_scr.shape[0])], axis=1)
    gates = jax.nn.sigmoid(og[:, d:] + bgate_ref[...])
    merged = gates[:, :d] * y_a + gates[:, d:] * y_b
    x1 = x + mod[2:3] * jnp.dot(merged.astype(BF16), wo_ref[...], preferred_element_type=F32)
    x1_ref[0] = x1
    h2f = _norm_mod(x1, g2_ref[...], mod[4:5], mod[3:4])
    h2_ref[0] = h2f
    logits = jnp.dot(h2f.astype(BF16), wr_ref[...], preferred_element_type=F32) + br_ref[...]
    tm, n_e = logits.shape
    e_iota = lax.broadcasted_iota(jnp.int32, (tm, n_e), 1)
    lane = lax.broadcasted_iota(jnp.int32, (tm, ti_ref.shape[-1]), 1)
    ti = jnp.zeros(lane.shape, jnp.int32)
    tv = jnp.zeros(lane.shape, F32)
    top = None
    chosen = []
    for k in range(TOP_K):
        mx = jnp.max(logits, axis=-1, keepdims=True)
        idx = jnp.min(jnp.where(logits == mx, e_iota, n_e), axis=-1, keepdims=True)
        top = mx if top is None else top
        ti = jnp.where(lane == k, idx, ti)
        tv = jnp.where(lane == k, jnp.exp(mx - top), tv)
        chosen.append(e_iota == idx)
        logits = jnp.where(chosen[-1], -jnp.inf, logits)
    ti_ref[0] = ti
    tp_ref[0] = tv / jnp.sum(tv, axis=-1, keepdims=True)

    @pl.when((pl.program_id(0) == 0) & (pl.program_id(1) == 0))
    def _():
        base_ref[...] = jnp.zeros(base_ref.shape, F32)

    onehot = jnp.zeros((tm, n_e), F32)
    for sel in chosen:
        onehot = onehot + jnp.where(sel, 1.0, 0.0)
    below = (lax.broadcasted_iota(jnp.int32, (tm, tm), 1) < lax.broadcasted_iota(jnp.int32, (tm, tm), 0))
    before = jnp.dot(jnp.where(below, 1.0, 0.0).astype(BF16), onehot.astype(BF16),
                     preferred_element_type=F32) + base_ref[...]
    rk = jnp.zeros(lane.shape, jnp.int32)
    for k, sel in enumerate(chosen):
        rank = jnp.sum(jnp.where(sel, before, 0.0), axis=-1, keepdims=True)
        rk = jnp.where(lane == k, rank.astype(jnp.int32), rk)
    rk_ref[0] = rk
    base_ref[...] = base_ref[...] + jnp.sum(onehot, axis=0, keepdims=True)
    cnt_ref[...] = base_ref[...].astype(jnp.int32)


def _mixout(x, h_f, h_r, y_b, mod, g1, g2, g_mh, w_og, w_a, b_gate, w_o, w_r, b_r, tm):
    bsz, length, d = x.shape
    n_pos = y_b.shape[1]
    n_e = w_r.shape[-1]
    tok = lambda b, i: (b, i, 0)
    const2 = lambda b, i: (0, 0)

    def wspec(w):
        return pl.BlockSpec(w.shape, const2, pipeline_mode=pl.Buffered(1))

    weights = (w_og, w_a, w_o, w_r)
    w_bytes = sum(int(w.size) * w.dtype.itemsize for w in weights)
    vmem = w_bytes + 2 * tm * (d * 4 + 3 * d * 2 + d * 4 + d * 4) + 15 * tm * d * 4
    return pl.pallas_call(
        _mixout_body,
        grid=(bsz, length // tm),
        in_specs=[pl.BlockSpec((1, tm, d), tok), pl.BlockSpec((1, tm, d), tok), pl.BlockSpec((1, tm, d), tok),
                  pl.BlockSpec((1, n_pos, tm // n_pos, d), lambda b, i: (b, 0, i, 0)),
                  pl.BlockSpec((1, 6, d), lambda b, i: (b, 0, 0)),
                  pl.BlockSpec((1, d), const2), pl.BlockSpec((1, d), const2), pl.BlockSpec((1, d), const2),
                  wspec(w_og), wspec(w_a),
                  pl.BlockSpec((1, 2 * d), const2), wspec(w_o), wspec(w_r), pl.BlockSpec((1, n_e), const2)],
        out_specs=[pl.BlockSpec((1, tm, d), tok), pl.BlockSpec((1, tm, d), tok),
                   pl.BlockSpec((1, tm, 8), tok), pl.BlockSpec((1, tm, 8), tok), pl.BlockSpec((1, tm, 8), tok),
                   pl.BlockSpec((1, n_e), const2)],
        out_shape=[jax.ShapeDtypeStruct((bsz, length, d), F32),
                   jax.ShapeDtypeStruct((bsz, length, d), F32),
                   jax.ShapeDtypeStruct((bsz, length, 8), jnp.int32),
                   jax.ShapeDtypeStruct((bsz, length, 8), F32),
                   jax.ShapeDtypeStruct((bsz, length, 8), jnp.int32),
                   jax.ShapeDtypeStruct((1, n_e), jnp.int32)],
        scratch_shapes=[pltpu.VMEM((1, n_e), F32), pltpu.VMEM((d // LANES, tm, LANES), F32)],
        compiler_params=_cparams(2, vmem),
        name="mixout",
    )(x, h_f, h_r, y_b, mod, g1.reshape(1, d), g2.reshape(1, d), g_mh.reshape(1, d), w_og, w_a,
      b_gate.reshape(1, 2 * d), w_o, w_r, b_r.reshape(1, n_e))


def _rowpos_body(ti_ref, rk_ref, rs_ref, pos_ref):
    ti = ti_ref[...]
    rk = rk_ref[...]
    n, w = ti.shape
    n_e = rs_ref.shape[-1]
    e_iota = lax.broadcasted_iota(jnp.int32, (n, n_e), 1)
    lane = lax.broadcasted_iota(jnp.int32, (n, w), 1)
    pos = jnp.zeros((n, w), jnp.int32)
    for k in range(TOP_K):
        start = jnp.sum(jnp.where(e_iota == ti[:, k:k + 1], rs_ref[...], 0), axis=-1, keepdims=True)
        pos = jnp.where(lane == k, start + rk[:, k:k + 1], pos)
    pos_ref[...] = pos


def _rowpos(top_i, rank, row_start, tn):
    n, w = top_i.shape
    n_e = row_start.shape[-1]
    return pl.pallas_call(
        _rowpos_body,
        grid=(n // tn,),
        in_specs=[pl.BlockSpec((tn, w), lambda i: (i, 0)), pl.BlockSpec((tn, w), lambda i: (i, 0)),
                  pl.BlockSpec((1, n_e), lambda i: (0, 0))],
        out_specs=pl.BlockSpec((tn, w), lambda i: (i, 0)),
        out_shape=jax.ShapeDtypeStruct((n, w), jnp.int32),
        compiler_params=_cparams(1, 16 * tn * 128 * 4),
        name="rowpos",
    )(top_i, rank, row_start)


def _dispatch_body(pos_ref, h_ref, xs_hbm, sem):
    tm = h_ref.shape[0]

    def issue(t, carry):
        for k in range(TOP_K):
            row = pos_ref[t * TOP_K + k]
            pltpu.make_async_copy(h_ref.at[pl.ds(t, 1), :], xs_hbm.at[pl.ds(row, 1), :], sem).start(priority=k % 2)
        return carry

    lax.fori_loop(0, tm, issue, 0, unroll=4)
    for k in range(TOP_K):
        pltpu.make_async_copy(h_ref, xs_hbm.at[pl.ds(0, tm), :], sem).wait()


def _dispatch(pos_flat, h_rows, n_rows, tm):
    n, w = h_rows.shape
    return pl.pallas_call(
        _dispatch_body,
        grid=(n // tm,),
        in_specs=[pl.BlockSpec((tm * TOP_K,), lambda i: (i,), memory_space=pltpu.SMEM),
                  pl.BlockSpec((tm, w), lambda i: (i, 0))],
        out_specs=pl.BlockSpec(memory_space=pl.ANY),
        out_shape=jax.ShapeDtypeStruct((n_rows, w), h_rows.dtype),
        scratch_shapes=[pltpu.SemaphoreType.DMA(())],
        compiler_params=_cparams(1, 8 * tm * w * 4),
        name="dispatch",
    )(pos_flat, h_rows)


def _experts_body(te_ref, tv_ref, nt_ref, xs_ref, win_ref, bin_ref, wout_ref, bout_ref, y_ref, win_bf, wout_bf,
                  *, f_chunk, cast_rows):
    i = pl.program_id(0)
    e = te_ref[i]
    e_prev = te_ref[jnp.maximum(i - 1, 0)]
    d, f2 = win_bf.shape
    f = f2 // 2

    @pl.when((i == 0) | (e != e_prev))
    def _():
        def cast_in(r, carry):
            rows = pl.ds(pl.multiple_of(r * cast_rows, cast_rows), cast_rows)
            win_bf[rows, :] = win_ref[0, rows, :].astype(BF16)
            return carry

        def cast_out(r, carry):
            rows = pl.ds(pl.multiple_of(r * cast_rows, cast_rows), cast_rows)
            wout_bf[rows, :] = wout_ref[0, rows, :].astype(BF16)
            return carry

        lax.fori_loop(0, d // cast_rows, cast_in, 0)
        lax.fori_loop(0, f // cast_rows, cast_out, 0)

    @pl.when(i < nt_ref[0])
    def _():
        live = lax.broadcasted_iota(jnp.int32, xs_ref.shape, 0) < tv_ref[i]
        x = jnp.where(live, xs_ref[...], 0.0).astype(BF16)
        acc = None
        for c in range(f // f_chunk):
            lo = c * f_chunk
            zg = jnp.dot(x, win_bf[:, lo:lo + f_chunk], preferred_element_type=F32) + bin_ref[0, :, lo:lo + f_chunk]
            zl = (jnp.dot(x, win_bf[:, f + lo:f + lo + f_chunk], preferred_element_type=F32)
                  + bin_ref[0, :, f + lo:f + lo + f_chunk])
            glu = jnp.minimum(zg, SWIGLU_LIMIT)
            lin = jnp.clip(zl, -SWIGLU_LIMIT, SWIGLU_LIMIT)
            act = glu * jax.nn.sigmoid(SWIGLU_ALPHA * glu) * (lin + 1.0)
            part = jnp.dot(act.astype(BF16), wout_bf[lo:lo + f_chunk, :], preferred_element_type=F32)
            acc = part if acc is None else acc + part
        y_ref[...] = acc + bout_ref[0]

    @pl.when(i >= nt_ref[0])
    def _():
        y_ref[...] = jnp.zeros(y_ref.shape, y_ref.dtype)


def _experts(tile_expert, tile_valid, n_tiles, xs, w_e_in, b_e_in, w_e_out, b_e_out, tm):
    rows, d = xs.shape
    n_e, _, f2 = w_e_in.shape
    f = f2 // 2
    nt_max = rows // tm
    row_map = lambda i, te, tv, nt: (jnp.minimum(i, nt[0] - 1), 0)
    exp_map = lambda i, te, tv, nt: (te[i], 0, 0)
    vmem = 2 * (d * f2 + f * d) * 4 + (d * f2 + f * d) * 2 + 4 * tm * d * 2 + 10 * tm * d * 4
    grid_spec = pltpu.PrefetchScalarGridSpec(
        num_scalar_prefetch=3,
        grid=(nt_max,),
        in_specs=[pl.BlockSpec((tm, d), row_map),
                  pl.BlockSpec((1, d, f2), exp_map), pl.BlockSpec((1, 1, f2), exp_map),
                  pl.BlockSpec((1, f, d), exp_map), pl.BlockSpec((1, 1, d), exp_map)],
        out_specs=pl.BlockSpec((tm, d), lambda i, te, tv, nt: (i, 0)),
        scratch_shapes=[pltpu.VMEM((d, f2), BF16), pltpu.VMEM((f, d), BF16)],
    )
    return pl.pallas_call(
        functools.partial(_experts_body, f_chunk=512, cast_rows=128),
        grid_spec=grid_spec,
        out_shape=jax.ShapeDtypeStruct((rows, d), F32),
        compiler_params=_cparams(1, vmem),
        name="experts",
    )(tile_expert, tile_valid, n_tiles, xs, w_e_in, b_e_in.reshape(n_e, 1, f2), w_e_out, b_e_out.reshape(n_e, 1, d))


def _final_body(pos_ref, posn_ref, x1_ref, tp_ref, gt_ref, g_ref, y_hbm, o_ref, buf, sem):
    i = pl.program_id(0)
    n = pl.num_programs(0)
    tm = x1_ref.shape[0]

    def gather(p_ref, slot):
        def issue(t, carry):
            for k in range(TOP_K):
                row = p_ref[t * TOP_K + k]
                pltpu.make_async_copy(y_hbm.at[pl.ds(row, 1), :], buf.at[slot, k, pl.ds(t, 1), :],
                                      sem.at[slot]).start(priority=k % 2)
            return carry

        lax.fori_loop(0, tm, issue, 0, unroll=4)

    @pl.when(i == 0)
    def _():
        gather(pos_ref, 0)

    @pl.when(i + 1 < n)
    def _():
        gather(posn_ref, (i + 1) % 2)

    slot = i % 2
    for k in range(TOP_K):
        pltpu.make_async_copy(y_hbm.at[pl.ds(0, tm), :], buf.at[slot, k], sem.at[slot]).wait()
    moe = None
    for k in range(TOP_K):
        term = tp_ref[:, k:k + 1] * buf[slot, k]
        moe = term if moe is None else moe + term
    x2 = x1_ref[...] + gt_ref[0] * moe
    o_ref[...] = x2 * lax.rsqrt(jnp.mean(x2 * x2, axis=-1, keepdims=True) + EPS) * g_ref[...]


def _final(pos_flat, x1, top_p, gt2, g_final, y_rows, tm):
    n_tok, d = x1.shape
    bsz = gt2.shape[0]
    per_b = n_tok // bsz // tm
    nxt = lambda i: (jnp.minimum(i + 1, n_tok // tm - 1),)
    return pl.pallas_call(
        _final_body,
        grid=(n_tok // tm,),
        in_specs=[pl.BlockSpec((tm * TOP_K,), lambda i: (i,), memory_space=pltpu.SMEM),
                  pl.BlockSpec((tm * TOP_K,), nxt, memory_space=pltpu.SMEM),
                  pl.BlockSpec((tm, d), lambda i: (i, 0)),
                  pl.BlockSpec((tm, top_p.shape[-1]), lambda i: (i, 0)),
                  pl.BlockSpec((1, 1, d), lambda i: (i // per_b, 0, 0)),
                  pl.BlockSpec((1, d), lambda i: (0, 0)),
                  pl.BlockSpec(memory_space=pl.ANY)],
        out_specs=pl.BlockSpec((tm, d), lambda i: (i, 0)),
        out_shape=jax.ShapeDtypeStruct((n_tok, d), F32),
        scratch_shapes=[pltpu.VMEM((2, TOP_K, tm, d), F32), pltpu.SemaphoreType.DMA((2,))],
        compiler_params=_cparams(1, 2 * TOP_K * tm * d * 4 + 4 * tm * d * 4 + 8 * tm * d * 4),
        name="final",
    )(pos_flat, pos_flat, x1, top_p, gt2, g_final.reshape(1, d), y_rows)


def _tile_table(counts, tm, nt_max):
    tiles = (counts + tm - 1) // tm
    tile_end = jnp.cumsum(tiles)
    tile_start = tile_end - tiles
    n_tiles = tile_end[-1]
    tile_ids = jnp.arange(nt_max, dtype=jnp.int32)
    last_e = jnp.sum((n_tiles - 1) >= tile_end).astype(jnp.int32)
    te = jnp.sum(tile_ids[:, None] >= tile_end[None, :], axis=1).astype(jnp.int32)
    te = jnp.where(tile_ids < n_tiles, te, last_e)
    sel = te[:, None] == jnp.arange(counts.shape[0], dtype=jnp.int32)[None, :]
    cnt_t = jnp.sum(jnp.where(sel, counts[None, :], 0), axis=1)
    start_t = jnp.sum(jnp.where(sel, tile_start[None, :], 0), axis=1)
    live = jnp.clip(cnt_t - (tile_ids - start_t) * tm, 0, tm).astype(jnp.int32)
    live = jnp.where(tile_ids < n_tiles, live, 0)
    return (tile_start * tm).astype(jnp.int32), te, live, n_tiles.reshape(1).astype(jnp.int32)


def kernel(x, c, ctx, c_ctx, w_ada, b_ada, g_norm1, g_norm2, w_in, w_conv_qk, b_ifgate, g_mh, w_branch_m,
           s5_a_re, s5_a_im, s5_log_dt, s5_b_re, s5_b_im, s5_c_re, s5_c_im, s5_d, w_glu, b_glu, w_branch_s,
           b_merge_gate, w_o, w_router, b_router, w_e_in, b_e_in, w_e_out, b_e_out, g_final):
    bsz, length, d = x.shape
    l_ctx = ctx.shape[1]
    n_qk = 2 * N_HEADS * D_QK
    n_v = N_HEADS * D_V
    n_if = 4 * N_HEADS
    n_u = S5_GROUPS * S5_GROUP
    off_if = n_qk + n_v
    off_u = off_if + n_if
    off_o = off_u + n_u
    layer = 0

    pad_rows = -(bsz + 1) % 8
    c_rows = jnp.concatenate([c, c_ctx[None, :], jnp.zeros((pad_rows, d), F32)], axis=0)
    mod_all = _ada(c_rows, w_ada[layer], b_ada[layer])
    mod = mod_all[:bsz].reshape(bsz, 6, d)
    mod_c = mod_all[bsz, :2 * d].reshape(2, 1, 1, d)

    w_l = w_in[layer]
    w_state = jnp.concatenate([w_l[:, :off_if], w_l[:, off_u:off_o]], axis=1).astype(BF16)
    w_if_t = w_l[:, off_if:off_u].T.astype(BF16)
    b_if = b_ifgate[layer].reshape(n_if)
    proj_c = _proj(ctx, g_norm1[layer], mod_c[1], mod_c[0], w_state, w_if_t, b_if, n_qk, n_v, n_u, l_ctx)
    proj_l = _proj(x, g_norm1[layer], mod[:, 1:2], mod[:, 0:1], w_state, w_if_t, b_if, n_qk, n_v, n_u, 512)
    qk_c, v_c, u_c, gif_c = proj_c
    qk_l, v_l, u_l, gif_l = proj_l

    w9 = w_conv_qk[layer].reshape(9, n_qk)
    q_c, kt_c = _conv(qk_c, w9, D_QK ** -0.5, l_ctx, l_ctx)
    q_l, kt_l = _conv(qk_l, w9, D_QK ** -0.5, GRID_W, 512)

    st0 = jnp.zeros((bsz, 2 * N_HEADS, D_QK, 2 * D_V), F32)
    m0 = jnp.zeros((bsz, 2 * N_HEADS, 1), F32)
    _, _, st_c, m_c = _mlstm(q_c, kt_c, v_c, gif_c, st0, m0)
    h_f, h_r, _, _ = _mlstm(q_l, kt_l, v_l, gif_l, st_c, m_c)

    w_in_t, m_tz_t, c_out_t, mu_a, mu_b = _s5_tables(
        s5_a_re[layer], s5_a_im[layer], s5_log_dt[layer], s5_b_re[layer], s5_b_im[layer],
        s5_c_re[layer], s5_c_im[layer], s5_d[layer])
    hs0 = jnp.zeros((bsz, S5_GROUPS, 4 * S5_STATE), F32)
    hs_c, = _s5(u_c, hs0, w_in_t, m_tz_t, c_out_t, mu_a, mu_b)
    y_b, _ = _s5(u_l, hs_c, w_in_t, m_tz_t, c_out_t, mu_a, mu_b,
                 w_glu[layer].T.astype(BF16), b_glu[layer], w_branch_s[layer].T.astype(BF16))

    w_og = w_l[:, off_o:].astype(BF16)
    x1, h2, top_i, top_p, rank, counts = _mixout(
        x, h_f, h_r, y_b, mod, g_norm1[layer], g_norm2[layer], g_mh[layer], w_og, w_branch_m[layer].astype(BF16),
        b_merge_gate[layer], w_o[layer].astype(BF16), w_router[layer].astype(BF16), b_router[layer], 512)

    tm_e = 512
    n_tok = bsz * length
    nt_max = n_tok * TOP_K // tm_e + N_EXPERTS
    row_start, tile_expert, tile_live, n_tiles = _tile_table(counts.reshape(N_EXPERTS), tm_e, nt_max)
    pos = _rowpos(top_i.reshape(n_tok, 8), rank.reshape(n_tok, 8), row_start.reshape(1, N_EXPERTS), min(4096, n_tok))
    pos_flat = pos[:, :TOP_K].reshape(n_tok * TOP_K)
    xs = _dispatch(pos_flat, h2.reshape(n_tok, d), nt_max * tm_e, 512)
    y_rows = _experts(tile_expert, tile_live, n_tiles, xs, w_e_in[layer], b_e_in[layer], w_e_out[layer],
                      b_e_out[layer], tm_e)
    out = _final(pos_flat, x1.reshape(n_tok, d), top_p.reshape(n_tok, 8), mod[:, 5:6], g_final, y_rows, 256)
    return out.reshape(bsz, length, d)
```

```python
import functools
import math

import jax
import jax.numpy as jnp
from jax import lax
from jax.experimental import pallas as pl
from jax.experimental.pallas import tpu as pltpu

F32 = jnp.float32
BF16 = jnp.bfloat16
EPS = 1e-6

N_HEADS = 8
D_QK = 64
D_V = 128
M_CHUNK = 128
GRID_W = 64
S5_GROUPS = 32
S5_GROUP = 16
S5_STATE = 64
S5_CHUNK = 16
N_EXPERTS = 32
TOP_K = 4
SWIGLU_LIMIT = 7.0
SWIGLU_ALPHA = 1.702

LANES = 128
SUBLANES = 8
V7X_VMEM_BYTES = 64 * 1024 * 1024
_VMEM_CAP = V7X_VMEM_BYTES - 8 * 1024 * 1024


def _cparams(n_axes, vmem_bytes):
    limit = int(min(_VMEM_CAP, max(32 * 1024 * 1024, vmem_bytes)))
    return pltpu.CompilerParams(dimension_semantics=("arbitrary",) * n_axes, vmem_limit_bytes=limit)


def _silu(x):
    return x * jax.nn.sigmoid(x)


def _norm_mod(x, g, scale, shift):
    ms = jnp.mean(x * x, axis=-1, keepdims=True)
    return (x * lax.rsqrt(ms + EPS) * g) * (1.0 + scale) + shift


def _split3(x):
    hi = x.astype(BF16)
    r1 = x - hi.astype(F32)
    mid = r1.astype(BF16)
    lo = (r1 - mid.astype(F32)).astype(BF16)
    return hi, mid, lo


def _dot_nt(a, b):
    return lax.dot_general(a, b, (((1,), (1,)), ((), ())), preferred_element_type=F32)


def _dot_tn(a, b):
    return lax.dot_general(a, b, (((0,), (0,)), ((), ())), preferred_element_type=F32)


def _ada_body(c_ref, w_ref, b_ref, o_ref):
    s = _silu(c_ref[...])
    o_ref[...] = jnp.dot(s, w_ref[...], preferred_element_type=F32,
                         precision=lax.Precision.HIGHEST) + b_ref[...]


def _ada(c_rows, w_ada, b_ada):
    rows, d = c_rows.shape
    n = w_ada.shape[1]
    tn = 1024
    return pl.pallas_call(
        _ada_body,
        grid=(n // tn,),
        in_specs=[pl.BlockSpec((rows, d), lambda j: (0, 0)),
                  pl.BlockSpec((d, tn), lambda j: (0, j)),
                  pl.BlockSpec((1, tn), lambda j: (0, j))],
        out_specs=pl.BlockSpec((rows, tn), lambda j: (0, j)),
        out_shape=jax.ShapeDtypeStruct((rows, n), F32),
        compiler_params=_cparams(1, 4 * d * tn * 4),
        name="ada",
    )(c_rows, w_ada, b_ada.reshape(1, n))


def _proj_body(x_ref, g_ref, sc_ref, sh_ref, w_ref, wif_ref, bif_ref, qk_ref, v_ref, u_ref, gif_ref):
    h = _norm_mod(x_ref[0], g_ref[...], sc_ref[0], sh_ref[0]).astype(BF16)
    n_qk = qk_ref.shape[-1]
    n_v = v_ref.shape[-1]
    qk_ref[0] = jnp.dot(h, w_ref[:, :n_qk], preferred_element_type=F32).astype(BF16)
    v_ref[0] = jnp.dot(h, w_ref[:, n_qk:n_qk + n_v], preferred_element_type=F32).astype(BF16)
    u = jnp.dot(h, w_ref[:, n_qk + n_v:], preferred_element_type=F32)
    for j in range(u_ref.shape[1]):
        u_ref[0, j] = u[:, j * LANES:(j + 1) * LANES]
    gif_ref[0] = _dot_nt(wif_ref[...], h) + bif_ref[...]


def _proj(x, g, scale, shift, w_state, w_if_t, b_if, n_qk, n_v, n_u, tm):
    bsz, length, d = x.shape
    n_if = b_if.shape[-1]
    per_batch = scale.shape[0] == bsz
    mod_map = (lambda b, i: (b, 0, 0)) if per_batch else (lambda b, i: (0, 0, 0))
    cols = w_state.shape[1]
    vmem = 2 * (tm * d * 4 + d * cols * 2 + tm * (n_qk + n_v + n_u) * 2 + tm * 128 * 4) + 6 * tm * d * 4
    return pl.pallas_call(
        _proj_body,
        grid=(bsz, length // tm),
        in_specs=[pl.BlockSpec((1, tm, d), lambda b, i: (b, i, 0)),
                  pl.BlockSpec((1, d), lambda b, i: (0, 0)),
                  pl.BlockSpec((1, 1, d), mod_map),
                  pl.BlockSpec((1, 1, d), mod_map),
                  pl.BlockSpec((d, cols), lambda b, i: (0, 0)),
                  pl.BlockSpec((n_if, d), lambda b, i: (0, 0)),
                  pl.BlockSpec((n_if, 1), lambda b, i: (0, 0))],
        out_specs=[pl.BlockSpec((1, tm, n_qk), lambda b, i: (b, i, 0)),
                   pl.BlockSpec((1, tm, n_v), lambda b, i: (b, i, 0)),
                   pl.BlockSpec((1, n_u // LANES, tm, LANES), lambda b, i: (b, 0, i, 0)),
                   pl.BlockSpec((1, n_if, tm), lambda b, i: (b, 0, i))],
        out_shape=[jax.ShapeDtypeStruct((bsz, length, n_qk), BF16),
                   jax.ShapeDtypeStruct((bsz, length, n_v), BF16),
                   jax.ShapeDtypeStruct((bsz, n_u // LANES, length, LANES), F32),
                   jax.ShapeDtypeStruct((bsz, n_if, length), F32)],
        compiler_params=_cparams(2, vmem),
        name="proj",
    )(x, g.reshape(1, d), scale, shift, w_state, w_if_t, b_if.reshape(n_if, 1))


def _conv_body(main_ref, prev_ref, next_ref, w_ref, q_ref, kt_ref, *, width, q_scale):
    i = pl.program_id(1)
    last = pl.num_programs(1) - 1
    t = main_ref.shape[1]
    n = t + 2 * width
    main = main_ref[0].astype(F32)
    prev = jnp.where(i > 0, prev_ref[0].astype(F32), 0.0)
    nxt = jnp.where(i < last, next_ref[0].astype(F32), 0.0)
    ext = jnp.concatenate([prev, main, nxt], axis=0)
    col = lax.broadcasted_iota(jnp.int32, (t, 1), 0) % width
    acc = None
    for dx in (-1, 0, 1):
        shifted = ext if dx == 0 else pltpu.roll(ext, (-dx) % n, axis=0)
        part = None
        for dy in (-1, 0, 1):
            tap = w_ref[(dy + 1) * 3 + (dx + 1):(dy + 1) * 3 + (dx + 1) + 1, :]
            term = tap * shifted[width + dy * width:width + dy * width + t]
            part = term if part is None else part + term
        if dx == -1:
            part = jnp.where(col == 0, 0.0, part)
        elif dx == 1:
            part = jnp.where(col == width - 1, 0.0, part)
        acc = part if acc is None else acc + part
    y = _silu(acc)

    @pl.when(pl.program_id(2) == 0)
    def _():
        q_ref[0] = (y * q_scale).astype(q_ref.dtype)

    @pl.when(pl.program_id(2) == 1)
    def _():
        kt_ref[0] = y.T.astype(kt_ref.dtype)


def _conv(qk_pre, w9, q_scale, width, t_block):
    bsz, length, ch2 = qk_pre.shape
    ch = ch2 // 2
    rpb = t_block // width
    n_rows = length // width
    vmem = 4 * (t_block + 2 * width) * ch * 2 + 14 * (t_block + 2 * width) * ch * 4
    return pl.pallas_call(
        functools.partial(_conv_body, width=width, q_scale=q_scale),
        grid=(bsz, length // t_block, 2),
        in_specs=[pl.BlockSpec((1, t_block, ch), lambda b, i, c: (b, i, c)),
                  pl.BlockSpec((1, width, ch), lambda b, i, c: (b, jnp.maximum(i * rpb - 1, 0), c)),
                  pl.BlockSpec((1, width, ch), lambda b, i, c: (b, jnp.minimum((i + 1) * rpb, n_rows - 1), c)),
                  pl.BlockSpec((9, ch), lambda b, i, c: (0, c))],
        out_specs=[pl.BlockSpec((1, t_block, ch), lambda b, i, c: (b, i, 0)),
                   pl.BlockSpec((1, ch, t_block), lambda b, i, c: (b, 0, i))],
        out_shape=[jax.ShapeDtypeStruct((bsz, length, ch), BF16),
                   jax.ShapeDtypeStruct((bsz, ch, length), BF16)],
        compiler_params=_cparams(3, vmem),
        name="conv",
    )(qk_pre, qk_pre, qk_pre, w9)


def _log_sigmoid(x):
    return jnp.minimum(x, 0.0) - jnp.log1p(jnp.exp(-jnp.abs(x)))


def _cumsum_lanes_exact(x, reverse):
    t = x.shape[-1]
    r_idx = lax.broadcasted_iota(jnp.int32, (t, t), 0)
    c_idx = lax.broadcasted_iota(jnp.int32, (t, t), 1)
    u01 = jnp.where((r_idx >= c_idx) if reverse else (r_idx <= c_idx), 1.0, 0.0).astype(BF16)
    out = None
    for piece in _split3(x):
        p = jnp.dot(piece, u01, preferred_element_type=F32)
        out = p if out is None else out + p
    return out


def _cummax_lanes(x, reverse):
    t = x.shape[-1]
    lane = lax.broadcasted_iota(jnp.int32, x.shape, 1)
    sh = 1
    while sh < t:
        if reverse:
            cand = jnp.where(lane < t - sh, pltpu.roll(x, t - sh, axis=1), -jnp.inf)
        else:
            cand = jnp.where(lane >= sh, pltpu.roll(x, sh, axis=1), -jnp.inf)
        x = jnp.maximum(x, cand)
        sh *= 2
    return x


def _mlstm_gate_rows(g_ref, m_col, d):
    t = g_ref.shape[-1]
    base = d * 2 * N_HEADS
    li = g_ref[0, base:base + N_HEADS, :]
    lf = _log_sigmoid(g_ref[0, base + N_HEADS:base + 2 * N_HEADS, :])
    b = _cumsum_lanes_exact(lf, reverse=(d == 1))
    g = li - b
    a = jnp.maximum(_cummax_lanes(g, reverse=(d == 1)), m_col)
    end = 0 if d == 1 else t - 1
    a_end = a[:, end:end + 1]
    return dict(g=g, a=a, ie=jnp.exp(m_col - a), emt=jnp.exp(-b - a), we=jnp.exp(g - a_end),
                dec=jnp.exp(m_col - a_end), m_new=b[:, end:end + 1] + a_end)


def _mlstm_body(qf_ref, kf_ref, vf_ref, gf_ref, qr_ref, kr_ref, vr_ref, gr_ref, st0_ref, m0_ref,
                hf_ref, hr_ref, st_ref, m_ref):
    @pl.when(pl.program_id(1) == 0)
    def _():
        st_ref[...] = st0_ref[...]
        m_ref[...] = m0_ref[...]

    t = qf_ref.shape[1]
    m_all = m_ref[0]
    rows = [_mlstm_gate_rows(g_ref, m_all[d * N_HEADS:(d + 1) * N_HEADS], d) for d, g_ref in ((0, gf_ref), (1, gr_ref))]
    both = lambda name: jnp.concatenate([rows[0][name], rows[1][name]], axis=0)
    n_hd = 2 * N_HEADS
    pad = jnp.zeros((128 - 3 * n_hd, t), F32)
    cols = jnp.concatenate([both('a'), both('ie'), both('emt'), pad], axis=0).T
    m_ref[0] = both('m_new')
    r_idx = lax.broadcasted_iota(jnp.int32, (t, t), 0)
    c_idx = lax.broadcasted_iota(jnp.int32, (t, t), 1)
    ones_blk = jnp.ones((t, D_V), BF16)
    for d, (q_ref, kt_ref, v_ref, h_ref) in enumerate(((qf_ref, kf_ref, vf_ref, hf_ref), (qr_ref, kr_ref, vr_ref, hr_ref))):
        causal = (c_idx >= r_idx) if d == 1 else (c_idx <= r_idx)
        for h in range(N_HEADS):
            j = d * N_HEADS + h
            q = q_ref[0, :, h * D_QK:(h + 1) * D_QK]
            kt = kt_ref[0, h * D_QK:(h + 1) * D_QK, :]
            v1 = jnp.concatenate([v_ref[0, :, h * D_V:(h + 1) * D_V], ones_blk], axis=1)
            state = st_ref[0, j]
            a_col = cols[:, j:j + 1]
            ie_col = cols[:, n_hd + j:n_hd + j + 1]
            emt_col = cols[:, 2 * n_hd + j:2 * n_hd + j + 1]
            dmat = jnp.exp(jnp.where(causal, rows[d]['g'][h:h + 1, :] - a_col, -jnp.inf))
            s = (jnp.dot(q, kt, preferred_element_type=F32) * dmat).astype(BF16)
            z = jnp.dot(q, state.astype(BF16), preferred_element_type=F32)
            p = jnp.dot(s, v1, preferred_element_type=F32)
            num = ie_col * z[:, :D_V] + p[:, :D_V]
            den = ie_col * z[:, D_V:] + p[:, D_V:]
            h_ref[0, :, h * D_V:(h + 1) * D_V] = (num / jnp.maximum(jnp.abs(den), emt_col)).astype(h_ref.dtype)
            kw = (kt.astype(F32) * rows[d]['we'][h:h + 1, :]).astype(BF16)
            st_ref[0, j] = rows[d]['dec'][h:h + 1, :] * state + jnp.dot(kw, v1, preferred_element_type=F32)


def _mlstm(q, k_t, v, gif_t, st0, m0):
    bsz, length, hv = v.shape
    t = M_CHUNK
    nc = length // t
    hq = N_HEADS * D_QK
    ng = gif_t.shape[1]
    fwd = lambda b, i: (b, i, 0)
    rev = lambda b, i: (b, nc - 1 - i, 0)
    fwd_t = lambda b, i: (b, 0, i)
    rev_t = lambda b, i: (b, 0, nc - 1 - i)
    st_spec = pl.BlockSpec((1,) + st0.shape[1:], lambda b, i: (b, 0, 0, 0))
    m_spec = pl.BlockSpec((1,) + m0.shape[1:], lambda b, i: (b, 0, 0))
    vmem = 24 * 1024 * 1024
    return pl.pallas_call(
        _mlstm_body,
        grid=(bsz, nc),
        in_specs=[pl.BlockSpec((1, t, hq), fwd), pl.BlockSpec((1, hq, t), fwd_t),
                  pl.BlockSpec((1, t, hv), fwd), pl.BlockSpec((1, ng, t), fwd_t),
                  pl.BlockSpec((1, t, hq), rev), pl.BlockSpec((1, hq, t), rev_t),
                  pl.BlockSpec((1, t, hv), rev), pl.BlockSpec((1, ng, t), rev_t),
                  st_spec, m_spec],
        out_specs=[pl.BlockSpec((1, t, hv), fwd), pl.BlockSpec((1, t, hv), rev), st_spec, m_spec],
        out_shape=[jax.ShapeDtypeStruct((bsz, length, hv), BF16),
                   jax.ShapeDtypeStruct((bsz, length, hv), BF16),
                   jax.ShapeDtypeStruct(st0.shape, F32),
                   jax.ShapeDtypeStruct(m0.shape, F32)],
        compiler_params=_cparams(2, vmem),
        name="mlstm",
    )(q, k_t, v, gif_t, q, k_t, v, gif_t, st0, m0)


def _s5_tables(a_re, a_im, log_dt, b_re, b_im, c_re, c_im, d_skip):
    hp = lax.Precision.HIGHEST
    t = S5_CHUNK
    n_dir, groups, p = a_re.shape
    cg = b_re.shape[-1]
    dt = jnp.exp(log_dt)[..., None]

    def lam_pow(n):
        mag = jnp.exp(n * (dt * a_re)[..., None])
        ang = n * (dt * a_im)[..., None]
        return mag * jnp.cos(ang), mag * jnp.sin(ang)

    ab_re, ab_im = (z[..., 0] for z in lam_pow(jnp.ones((1,), F32)))
    den = a_re * a_re + a_im * a_im
    xr = ab_re - 1.0
    cf_re = (xr * a_re + ab_im * a_im) / den
    cf_im = (ab_im * a_re - xr * a_im) / den
    bb_re = cf_re[..., None] * b_re - cf_im[..., None] * b_im
    bb_im = cf_re[..., None] * b_im + cf_im[..., None] * b_re
    jj = jnp.arange(t + 1, dtype=F32)
    lp_re, lp_im = lam_pow(jj)

    def w_dir(d, exps):
        lr = lp_re[d][:, :, exps]
        li = lp_im[d][:, :, exps]
        wr = lr[..., None] * bb_re[d][:, :, None, :] - li[..., None] * bb_im[d][:, :, None, :]
        wi = lr[..., None] * bb_im[d][:, :, None, :] + li[..., None] * bb_re[d][:, :, None, :]
        to_rows = lambda w: jnp.transpose(w, (0, 2, 3, 1)).reshape(groups, t * cg, p)
        return to_rows(wr), to_rows(wi)

    s_idx = jnp.arange(t)
    wf_re, wf_im = w_dir(0, t - 1 - s_idx)
    wr_re, wr_im = w_dir(1, s_idx)
    zw = jnp.zeros_like(wf_re)
    w_in = jnp.concatenate([jnp.concatenate([wf_re, zw, wf_im, zw], axis=-1),
                            jnp.concatenate([zw, wr_re, zw, wr_im], axis=-1)], axis=1)

    def c_dir(d, exps):
        lr = lp_re[d][:, :, exps]
        li = lp_im[d][:, :, exps]
        cr = jnp.transpose(c_re[d], (0, 2, 1))
        ci = jnp.transpose(c_im[d], (0, 2, 1))
        o_re = cr[:, :, None, :] * lr[..., None] - ci[:, :, None, :] * li[..., None]
        o_im = cr[:, :, None, :] * li[..., None] + ci[:, :, None, :] * lr[..., None]
        return o_re.reshape(groups, p, t * cg), (-o_im).reshape(groups, p, t * cg)

    cf_r, cf_i = c_dir(0, s_idx + 1)
    cr_r, cr_i = c_dir(1, t - s_idx)
    c_out = jnp.concatenate([cf_r, cr_r, cf_i, cr_i], axis=1)

    def k_dir(d):
        lr = lp_re[d][:, :, :t]
        li = lp_im[d][:, :, :t]
        clr = c_re[d][:, :, :, None] * lr[:, None] - c_im[d][:, :, :, None] * li[:, None]
        cli = c_re[d][:, :, :, None] * li[:, None] + c_im[d][:, :, :, None] * lr[:, None]
        return (jnp.einsum('gqpj,gpc->gjqc', clr, bb_re[d], precision=hp)
                - jnp.einsum('gqpj,gpc->gjqc', cli, bb_im[d], precision=hp))

    kf = k_dir(0)
    kr = k_dir(1)
    lag = s_idx[None, :] - s_idx[:, None]
    resp_f = jnp.where((lag >= 0)[None, :, :, None, None], kf[:, jnp.clip(lag, 0, t - 1)], 0.0)
    resp_r = jnp.where((lag <= 0)[None, :, :, None, None], kr[:, jnp.clip(-lag, 0, t - 1)], 0.0)
    skip = (jnp.eye(t, dtype=F32)[None, :, :, None, None] * jnp.eye(cg, dtype=F32)[None, None, None]
            * d_skip.reshape(groups, 1, 1, cg, 1))
    m_tz = jnp.transpose(resp_f + resp_r + skip, (0, 1, 4, 2, 3)).reshape(groups, t * cg, t * cg)

    n_pow = 8
    kk = (t * (2 ** jnp.arange(n_pow))).astype(F32)
    mp_re, mp_im = lam_pow(kk)
    mp_re = jnp.transpose(mp_re, (0, 1, 3, 2))
    mp_im = jnp.transpose(mp_im, (0, 1, 3, 2))
    mu_re = jnp.concatenate([mp_re[0], mp_re[1]], axis=-1)
    mu_im = jnp.concatenate([mp_im[0], mp_im[1]], axis=-1)
    tr = lambda m: jnp.transpose(m, (0, 2, 1)).astype(BF16)
    return w_in.astype(BF16), tr(m_tz), tr(c_out), mu_re, mu_im


def _pad_rows(x, rows):
    return x if x.shape[0] == rows else jnp.concatenate([x, jnp.zeros((rows - x.shape[0],) + x.shape[1:], x.dtype)], axis=0)


def _s5_body(*refs, nch, n_steps, with_output):
    if with_output:
        (u_ref, h0_ref, win_ref, mtzt_ref, coutt_ref, mure_ref, muim_ref, wglut_ref, bglu_ref, wbt_ref,
         yb_ref, hout_ref, xt_scr, yt_scr) = refs
    else:
        u_ref, h0_ref, win_ref, mure_ref, muim_ref, hout_ref, xt_scr = refs
    t_c, cg, groups = S5_CHUNK, S5_GROUP, S5_GROUPS
    p2 = 2 * S5_STATE
    ncp = xt_scr.shape[-1]
    for t in range(t_c):
        ut = jnp.concatenate([u_ref[0, j, pl.ds(t, nch, stride=t_c), :] for j in range(u_ref.shape[1])], axis=1)
        ut = _pad_rows(ut, ncp)
        xt_scr[:, t] = ut.T.astype(BF16).reshape(groups, cg, ncp)

    r_idx = lax.broadcasted_iota(jnp.int32, (ncp, ncp), 0)
    c_idx = lax.broadcasted_iota(jnp.int32, (ncp, ncp), 1)
    flip = jnp.where(r_idx + c_idx == nch - 1, 1.0, 0.0).astype(BF16)
    row = lax.broadcasted_iota(jnp.int32, (ncp, p2), 0)
    is_fwd = (lax.broadcasted_iota(jnp.int32, (ncp, 2 * p2), 1) % p2) < S5_STATE

    def group(g, carry):
        x = xt_scr[g].reshape(t_c * cg, ncp)
        x_rev = jnp.dot(x, flip, preferred_element_type=F32).astype(BF16)
        local = _dot_tn(jnp.concatenate([x, x_rev], axis=0), win_ref[g])
        l_re, l_im = local[:, :p2], local[:, p2:]
        h0 = h0_ref[0, pl.ds(g, 1), :]
        z_re = jnp.where(row == 0, h0[:, :p2], pltpu.roll(l_re, 1, axis=0))
        z_im = jnp.where(row == 0, h0[:, p2:], pltpu.roll(l_im, 1, axis=0))
        for k in range(n_steps):
            sft = 1 << k
            a_re = mure_ref[g, k:k + 1, :]
            a_im = muim_ref[g, k:k + 1, :]
            p_re = jnp.where(row >= sft, pltpu.roll(z_re, sft, axis=0), 0.0)
            p_im = jnp.where(row >= sft, pltpu.roll(z_im, sft, axis=0), 0.0)
            z_re, z_im = z_re + a_re * p_re - a_im * p_im, z_im + a_re * p_im + a_im * p_re
        if with_output:
            z = jnp.concatenate([z_re, z_im], axis=1).astype(BF16)
            z_flip = jnp.dot(flip, z, preferred_element_type=F32).astype(BF16)
            z_nat = jnp.where(is_fwd, z, z_flip)
            y = jnp.dot(mtzt_ref[g], x, preferred_element_type=F32) + _dot_nt(coutt_ref[g], z_nat)
            yt_scr[g] = jax.nn.gelu(y).astype(BF16).reshape(t_c, cg, ncp)
        a_re = mure_ref[g, 0:1, :]
        a_im = muim_ref[g, 0:1, :]
        e_re, e_im = z_re[nch - 1:nch], z_im[nch - 1:nch]
        hout_ref[0, pl.ds(g, 1), :] = jnp.concatenate(
            [a_re * e_re - a_im * e_im + l_re[nch - 1:nch], a_re * e_im + a_im * e_re + l_im[nch - 1:nch]], axis=1)
        return carry

    lax.fori_loop(0, groups, group, 0, unroll=4)

    if with_output:
        def tail(t, carry):
            ys = yt_scr[:, t].reshape(groups * cg, ncp)
            glu = jax.nn.sigmoid(jnp.dot(wglut_ref[...], ys, preferred_element_type=F32) + bglu_ref[...])
            ybt = jnp.dot(wbt_ref[...], (ys.astype(F32) * glu).astype(BF16), preferred_element_type=F32)
            yb_ref[0, t] = ybt.T[:nch].astype(yb_ref.dtype)
            return carry

        lax.fori_loop(0, t_c, tail, 0)


def _s5(u, h0, w_in_t, m_tz_t, c_out_t, mu_a, mu_b, w_glu_t=None, b_glu=None, w_b_t=None):
    bsz, n_blk, length, _ = u.shape
    width = n_blk * LANES
    with_output = w_glu_t is not None
    nch = length // S5_CHUNK
    ncp = -(-nch // 128) * 128
    p4 = 4 * S5_STATE
    n_steps = max(1, (nch - 1).bit_length())
    tc = S5_CHUNK * S5_GROUP
    one = pl.Buffered(1)
    full = lambda a: pl.BlockSpec(a.shape, lambda b: (0,) * a.ndim, pipeline_mode=one)
    in_specs = [pl.BlockSpec((1, n_blk, length, LANES), lambda b: (b, 0, 0, 0), pipeline_mode=one),
                pl.BlockSpec((1, S5_GROUPS, p4), lambda b: (b, 0, 0)), full(w_in_t)]
    args = [u, h0, w_in_t]
    scratch = [pltpu.VMEM((S5_GROUPS, S5_CHUNK, S5_GROUP, ncp), BF16)]
    out_specs = [pl.BlockSpec((1, S5_GROUPS, p4), lambda b: (b, 0, 0))]
    out_shape = [jax.ShapeDtypeStruct((bsz, S5_GROUPS, p4), F32)]
    vmem = length * width * 4 + 4 * S5_GROUPS * tc * tc * 2 + 2 * S5_GROUPS * tc * ncp * 2 + 8 * 1024 * 1024
    if with_output:
        d_out = w_b_t.shape[0]
        in_specs += [full(m_tz_t), full(c_out_t), full(mu_a), full(mu_b), full(w_glu_t),
                     pl.BlockSpec((width, 1), lambda b: (0, 0)), full(w_b_t)]
        args += [m_tz_t, c_out_t, mu_a, mu_b, w_glu_t, b_glu.reshape(width, 1), w_b_t]
        scratch.append(pltpu.VMEM((S5_GROUPS, S5_CHUNK, S5_GROUP, ncp), BF16))
        out_specs.insert(0, pl.BlockSpec((1, S5_CHUNK, nch, d_out), lambda b: (b, 0, 0, 0)))
        out_shape.insert(0, jax.ShapeDtypeStruct((bsz, S5_CHUNK, nch, d_out), BF16))
        vmem += 2 * S5_CHUNK * nch * d_out * 2 + (width * width + width * d_out) * 2
    else:
        in_specs += [full(mu_a), full(mu_b)]
        args += [mu_a, mu_b]
    return pl.pallas_call(
        functools.partial(_s5_body, nch=nch, n_steps=n_steps, with_output=with_output),
        grid=(bsz,),
        in_specs=in_specs,
        out_specs=out_specs,
        out_shape=out_shape,
        scratch_shapes=scratch,
        compiler_params=_cparams(1, vmem),
        name="s5",
    )(*args)


def _mixout_body(x_ref, hf_ref, hr_ref, yb_ref, mod_ref, g1_ref, g2_ref, gmh_ref, wog_ref, wa_ref,
                 bgate_ref, wo_ref, wr_ref, br_ref,
                 x1_ref, h2_ref, ti_ref, tp_ref, rk_ref, cnt_ref, base_ref, yb_scr):
    x = x_ref[0]
    d = x.shape[-1]
    mod = mod_ref[0]
    h = _norm_mod(x, g1_ref[...], mod[1:2], mod[0:1]).astype(BF16)
    og = jnp.dot(h, wog_ref[...], preferred_element_type=F32)
    hm = hf_ref[0].astype(F32) + hr_ref[0].astype(F32)
    heads = []
    for hd in range(N_HEADS):
        blk = hm[:, hd * D_V:(hd + 1) * D_V]
        heads.append(blk * lax.rsqrt(jnp.mean(blk * blk, axis=-1, keepdims=True) + EPS))
    hn = jnp.concatenate(heads, axis=1) * gmh_ref[...]
    y_a = jnp.dot((hn * jax.nn.sigmoid(og[:, :d])).astype(BF16), wa_ref[...], preferred_element_type=F32)
    n_pos = yb_ref.shape[1]
    for t in range(n_pos):
        blk = yb_ref[0, t].astype(F32)
        for j in range(yb_scr.shape[0]):
            yb_scr[j, pl.ds(t, yb_ref.shape[2], stride=n_pos), :] = blk[:, j * LANES:(j + 1) * LANES]
    y_b = jnp.concatenate([yb_scr[j] for j in range(yb_scr.shape[0])], axis=1)
    gates = jax.nn.sigmoid(og[:, d:] + bgate_ref[...])
    merged = gates[:, :d] * y_a + gates[:, d:] * y_b
    x1 = x + mod[2:3] * jnp.dot(merged.astype(BF16), wo_ref[...], preferred_element_type=F32)
    x1_ref[0] = x1
    h2f = _norm_mod(x1, g2_ref[...], mod[4:5], mod[3:4])
    h2_ref[0] = h2f
    logits = jnp.dot(h2f.astype(BF16), wr_ref[...], preferred_element_type=F32) + br_ref[...]
    tm, n_e = logits.shape
    e_iota = lax.broadcasted_iota(jnp.int32, (tm, n_e), 1)
    lane = lax.broadcasted_iota(jnp.int32, (tm, ti_ref.shape[-1]), 1)
    ti = jnp.zeros(lane.shape, jnp.int32)
    tv = jnp.zeros(lane.shape, F32)
    top = None
    chosen = []
    for k in range(TOP_K):
        mx = jnp.max(logits, axis=-1, keepdims=True)
        idx = jnp.min(jnp.where(logits == mx, e_iota, n_e), axis=-1, keepdims=True)
        top = mx if top is None else top
        ti = jnp.where(lane == k, idx, ti)
        tv = jnp.where(lane == k, jnp.exp(mx - top), tv)
        chosen.append(e_iota == idx)
        logits = jnp.where(chosen[-1], -jnp.inf, logits)
    ti_ref[0] = ti
    tp_ref[0] = tv / jnp.sum(tv, axis=-1, keepdims=True)

    @pl.when((pl.program_id(0) == 0) & (pl.program_id(1) == 0))
    def _():
        base_ref[...] = jnp.zeros(base_ref.shape, F32)

    onehot = jnp.zeros((tm, n_e), F32)
    for sel in chosen:
        onehot = onehot + jnp.where(sel, 1.0, 0.0)
    below = (lax.broadcasted_iota(jnp.int32, (tm, tm), 1) < lax.broadcasted_iota(jnp.int32, (tm, tm), 0))
    before = jnp.dot(jnp.where(below, 1.0, 0.0).astype(BF16), onehot.astype(BF16),
                     preferred_element_type=F32) + base_ref[...]
    rk = jnp.zeros(lane.shape, jnp.int32)
    for k, sel in enumerate(chosen):
        rank = jnp.sum(jnp.where(sel, before, 0.0), axis=-1, keepdims=True)
        rk = jnp.where(lane == k, rank.astype(jnp.int32), rk)
    rk_ref[0] = rk
    base_ref[...] = base_ref[...] + jnp.sum(onehot, axis=0, keepdims=True)
    cnt_ref[...] = base_ref[...].astype(jnp.int32)


def _mixout(x, h_f, h_r, y_b, mod, g1, g2, g_mh, w_og, w_a, b_gate, w_o, w_r, b_r, tm):
    bsz, length, d = x.shape
    n_pos = y_b.shape[1]
    n_e = w_r.shape[-1]
    tok = lambda b, i: (b, i, 0)
    const2 = lambda b, i: (0, 0)

    def wspec(w):
        return pl.BlockSpec(w.shape, const2, pipeline_mode=pl.Buffered(1))

    weights = (w_og, w_a, w_o, w_r)
    w_bytes = sum(int(w.size) * w.dtype.itemsize for w in weights)
    vmem = w_bytes + 2 * tm * (d * 4 + 3 * d * 2 + d * 4 + d * 4) + 15 * tm * d * 4
    return pl.pallas_call(
        _mixout_body,
        grid=(bsz, length // tm),
        in_specs=[pl.BlockSpec((1, tm, d), tok), pl.BlockSpec((1, tm, d), tok), pl.BlockSpec((1, tm, d), tok),
                  pl.BlockSpec((1, n_pos, tm // n_pos, d), lambda b, i: (b, 0, i, 0)),
                  pl.BlockSpec((1, 6, d), lambda b, i: (b, 0, 0)),
                  pl.BlockSpec((1, d), const2), pl.BlockSpec((1, d), const2), pl.BlockSpec((1, d), const2),
                  wspec(w_og), wspec(w_a),
                  pl.BlockSpec((1, 2 * d), const2), wspec(w_o), wspec(w_r), pl.BlockSpec((1, n_e), const2)],
        out_specs=[pl.BlockSpec((1, tm, d), tok), pl.BlockSpec((1, tm, d), tok),
                   pl.BlockSpec((1, tm, 8), tok), pl.BlockSpec((1, tm, 8), tok), pl.BlockSpec((1, tm, 8), tok),
                   pl.BlockSpec((1, n_e), const2)],
        out_shape=[jax.ShapeDtypeStruct((bsz, length, d), F32),
                   jax.ShapeDtypeStruct((bsz, length, d), F32),
                   jax.ShapeDtypeStruct((bsz, length, 8), jnp.int32),
                   jax.ShapeDtypeStruct((bsz, length, 8), F32),
                   jax.ShapeDtypeStruct((bsz, length, 8), jnp.int32),
                   jax.ShapeDtypeStruct((1, n_e), jnp.int32)],
        scratch_shapes=[pltpu.VMEM((1, n_e), F32), pltpu.VMEM((d // LANES, tm, LANES), F32)],
        compiler_params=_cparams(2, vmem),
        name="mixout",
    )(x, h_f, h_r, y_b, mod, g1.reshape(1, d), g2.reshape(1, d), g_mh.reshape(1, d), w_og, w_a,
      b_gate.reshape(1, 2 * d), w_o, w_r, b_r.reshape(1, n_e))


def _rowpos_body(ti_ref, rk_ref, rs_ref, pos_ref):
    ti = ti_ref[...]
    rk = rk_ref[...]
    n, w = ti.shape
    n_e = rs_ref.shape[-1]
    e_iota = lax.broadcasted_iota(jnp.int32, (n, n_e), 1)
    lane = lax.broadcasted_iota(jnp.int32, (n, w), 1)
    pos = jnp.zeros((n, w), jnp.int32)
    for k in range(TOP_K):
        start = jnp.sum(jnp.where(e_iota == ti[:, k:k + 1], rs_ref[...], 0), axis=-1, keepdims=True)
        pos = jnp.where(lane == k, start + rk[:, k:k + 1], pos)
    pos_ref[...] = pos


def _rowpos(top_i, rank, row_start, tn):
    n, w = top_i.shape
    n_e = row_start.shape[-1]
    return pl.pallas_call(
        _rowpos_body,
        grid=(n // tn,),
        in_specs=[pl.BlockSpec((tn, w), lambda i: (i, 0)), pl.BlockSpec((tn, w), lambda i: (i, 0)),
                  pl.BlockSpec((1, n_e), lambda i: (0, 0))],
        out_specs=pl.BlockSpec((tn, w), lambda i: (i, 0)),
        out_shape=jax.ShapeDtypeStruct((n, w), jnp.int32),
        compiler_params=_cparams(1, 16 * tn * 128 * 4),
        name="rowpos",
    )(top_i, rank, row_start)


def _dispatch_body(pos_ref, h_ref, xs_hbm, sem):
    tm = h_ref.shape[0]

    def issue(i, carry):
        base = pl.multiple_of(i * SUBLANES, SUBLANES)
        for s in range(SUBLANES):
            for k in range(TOP_K):
                row = pos_ref[(base + s) * TOP_K + k]
                pltpu.make_async_copy(h_ref.at[pl.ds(base + s, 1), :], xs_hbm.at[pl.ds(row, 1), :],
                                      sem).start(priority=k % 2)
        return carry

    lax.fori_loop(0, tm // SUBLANES, issue, 0)
    for k in range(TOP_K):
        pltpu.make_async_copy(h_ref, xs_hbm.at[pl.ds(0, tm), :], sem).wait()


def _dispatch(pos_flat, h_rows, n_rows, tm):
    n, w = h_rows.shape
    return pl.pallas_call(
        _dispatch_body,
        grid=(n // tm,),
        in_specs=[pl.BlockSpec((tm * TOP_K,), lambda i: (i,), memory_space=pltpu.SMEM),
                  pl.BlockSpec((tm, w), lambda i: (i, 0))],
        out_specs=pl.BlockSpec(memory_space=pl.ANY),
        out_shape=jax.ShapeDtypeStruct((n_rows, w), h_rows.dtype),
        scratch_shapes=[pltpu.SemaphoreType.DMA(())],
        compiler_params=_cparams(1, 8 * tm * w * 4),
        name="dispatch",
    )(pos_flat, h_rows)


def _experts_body(te_ref, tv_ref, nt_ref, xs_ref, win_ref, bin_ref, wout_ref, bout_ref, y_ref, win_bf, wout_bf,
                  *, f_chunk, cast_rows):
    i = pl.program_id(0)
    e = te_ref[i]
    e_prev = te_ref[jnp.maximum(i - 1, 0)]
    d, f2 = win_bf.shape
    f = f2 // 2

    @pl.when((i == 0) | (e != e_prev))
    def _():
        def cast_in(r, carry):
            rows = pl.ds(pl.multiple_of(r * cast_rows, cast_rows), cast_rows)
            win_bf[rows, :] = win_ref[0, rows, :].astype(BF16)
            return carry

        def cast_out(r, carry):
            rows = pl.ds(pl.multiple_of(r * cast_rows, cast_rows), cast_rows)
            wout_bf[rows, :] = wout_ref[0, rows, :].astype(BF16)
            return carry

        lax.fori_loop(0, d // cast_rows, cast_in, 0)
        lax.fori_loop(0, f // cast_rows, cast_out, 0)

    @pl.when(i < nt_ref[0])
    def _():
        live = lax.broadcasted_iota(jnp.int32, xs_ref.shape, 0) < tv_ref[i]
        x = jnp.where(live, xs_ref[...], 0.0).astype(BF16)
        acc = None
        for c in range(f // f_chunk):
            lo = c * f_chunk
            zg = jnp.dot(x, win_bf[:, lo:lo + f_chunk], preferred_element_type=F32) + bin_ref[0, :, lo:lo + f_chunk]
            zl = (jnp.dot(x, win_bf[:, f + lo:f + lo + f_chunk], preferred_element_type=F32)
                  + bin_ref[0, :, f + lo:f + lo + f_chunk])
            glu = jnp.minimum(zg, SWIGLU_LIMIT)
            lin = jnp.clip(zl, -SWIGLU_LIMIT, SWIGLU_LIMIT)
            act = glu * jax.nn.sigmoid(SWIGLU_ALPHA * glu) * (lin + 1.0)
            part = jnp.dot(act.astype(BF16), wout_bf[lo:lo + f_chunk, :], preferred_element_type=F32)
            acc = part if acc is None else acc + part
        y_ref[...] = acc + bout_ref[0]

    @pl.when(i >= nt_ref[0])
    def _():
        y_ref[...] = jnp.zeros(y_ref.shape, y_ref.dtype)


def _experts(tile_expert, tile_valid, n_tiles, xs, w_e_in, b_e_in, w_e_out, b_e_out, tm):
    rows, d = xs.shape
    n_e, _, f2 = w_e_in.shape
    f = f2 // 2
    nt_max = rows // tm
    row_map = lambda i, te, tv, nt: (jnp.minimum(i, nt[0] - 1), 0)
    exp_map = lambda i, te, tv, nt: (te[i], 0, 0)
    vmem = 2 * (d * f2 + f * d) * 4 + (d * f2 + f * d) * 2 + 4 * tm * d * 2 + 10 * tm * d * 4
    grid_spec = pltpu.PrefetchScalarGridSpec(
        num_scalar_prefetch=3,
        grid=(nt_max,),
        in_specs=[pl.BlockSpec((tm, d), row_map),
                  pl.BlockSpec((1, d, f2), exp_map), pl.BlockSpec((1, 1, f2), exp_map),
                  pl.BlockSpec((1, f, d), exp_map), pl.BlockSpec((1, 1, d), exp_map)],
        out_specs=pl.BlockSpec((tm, d), lambda i, te, tv, nt: (i, 0)),
        scratch_shapes=[pltpu.VMEM((d, f2), BF16), pltpu.VMEM((f, d), BF16)],
    )
    return pl.pallas_call(
        functools.partial(_experts_body, f_chunk=512, cast_rows=128),
        grid_spec=grid_spec,
        out_shape=jax.ShapeDtypeStruct((rows, d), F32),
        compiler_params=_cparams(1, vmem),
        name="experts",
    )(tile_expert, tile_valid, n_tiles, xs, w_e_in, b_e_in.reshape(n_e, 1, f2), w_e_out, b_e_out.reshape(n_e, 1, d))


def _final_body(pos_ref, posn_ref, x1_ref, tp_ref, gt_ref, g_ref, y_hbm, o_ref, buf, sem):
    i = pl.program_id(0)
    n = pl.num_programs(0)
    tm = x1_ref.shape[0]

    def gather(p_ref, slot):
        def issue(i, carry):
            base = pl.multiple_of(i * SUBLANES, SUBLANES)
            for s in range(SUBLANES):
                for k in range(TOP_K):
                    row = p_ref[(base + s) * TOP_K + k]
                    pltpu.make_async_copy(y_hbm.at[pl.ds(row, 1), :], buf.at[slot, k, pl.ds(base + s, 1), :],
                                          sem.at[slot]).start(priority=k % 2)
            return carry

        lax.fori_loop(0, tm // SUBLANES, issue, 0)

    @pl.when(i == 0)
    def _():
        gather(pos_ref, 0)

    @pl.when(i + 1 < n)
    def _():
        gather(posn_ref, (i + 1) % 2)

    slot = i % 2
    for k in range(TOP_K):
        pltpu.make_async_copy(y_hbm.at[pl.ds(0, tm), :], buf.at[slot, k], sem.at[slot]).wait()
    moe = None
    for k in range(TOP_K):
        term = tp_ref[:, k:k + 1] * buf[slot, k]
        moe = term if moe is None else moe + term
    x2 = x1_ref[...] + gt_ref[0] * moe
    o_ref[...] = x2 * lax.rsqrt(jnp.mean(x2 * x2, axis=-1, keepdims=True) + EPS) * g_ref[...]


def _final(pos_flat, x1, top_p, gt2, g_final, y_rows, tm):
    n_tok, d = x1.shape
    bsz = gt2.shape[0]
    per_b = n_tok // bsz // tm
    nxt = lambda i: (jnp.minimum(i + 1, n_tok // tm - 1),)
    return pl.pallas_call(
        _final_body,
        grid=(n_tok // tm,),
        in_specs=[pl.BlockSpec((tm * TOP_K,), lambda i: (i,), memory_space=pltpu.SMEM),
                  pl.BlockSpec((tm * TOP_K,), nxt, memory_space=pltpu.SMEM),
                  pl.BlockSpec((tm, d), lambda i: (i, 0)),
                  pl.BlockSpec((tm, top_p.shape[-1]), lambda i: (i, 0)),
                  pl.BlockSpec((1, 1, d), lambda i: (i // per_b, 0, 0)),
                  pl.BlockSpec((1, d), lambda i: (0, 0)),
                  pl.BlockSpec(memory_space=pl.ANY)],
        out_specs=pl.BlockSpec((tm, d), lambda i: (i, 0)),
        out_shape=jax.ShapeDtypeStruct((n_tok, d), F32),
        scratch_shapes=[pltpu.VMEM((2, TOP_K, tm, d), F32), pltpu.SemaphoreType.DMA((2,))],
        compiler_params=_cparams(1, 2 * TOP_K * tm * d * 4 + 4 * tm * d * 4 + 8 * tm * d * 4),
        name="final",
    )(pos_flat, pos_flat, x1, top_p, gt2, g_final.reshape(1, d), y_rows)


def _tile_table(counts, tm, nt_max):
    tiles = (counts + tm - 1) // tm
    tile_end = jnp.cumsum(tiles)
    tile_start = tile_end - tiles
    n_tiles = tile_end[-1]
    tile_ids = jnp.arange(nt_max, dtype=jnp.int32)
    last_e = jnp.sum((n_tiles - 1) >= tile_end).astype(jnp.int32)
    te = jnp.sum(tile_ids[:, None] >= tile_end[None, :], axis=1).astype(jnp.int32)
    te = jnp.where(tile_ids < n_tiles, te, last_e)
    sel = te[:, None] == jnp.arange(counts.shape[0], dtype=jnp.int32)[None, :]
    cnt_t = jnp.sum(jnp.where(sel, counts[None, :], 0), axis=1)
    start_t = jnp.sum(jnp.where(sel, tile_start[None, :], 0), axis=1)
    live = jnp.clip(cnt_t - (tile_ids - start_t) * tm, 0, tm).astype(jnp.int32)
    live = jnp.where(tile_ids < n_tiles, live, 0)
    return (tile_start * tm).astype(jnp.int32), te, live, n_tiles.reshape(1).astype(jnp.int32)


def kernel(x, c, ctx, c_ctx, w_ada, b_ada, g_norm1, g_norm2, w_in, w_conv_qk, b_ifgate, g_mh, w_branch_m,
           s5_a_re, s5_a_im, s5_log_dt, s5_b_re, s5_b_im, s5_c_re, s5_c_im, s5_d, w_glu, b_glu, w_branch_s,
           b_merge_gate, w_o, w_router, b_router, w_e_in, b_e_in, w_e_out, b_e_out, g_final):
    bsz, length, d = x.shape
    l_ctx = ctx.shape[1]
    n_qk = 2 * N_HEADS * D_QK
    n_v = N_HEADS * D_V
    n_if = 4 * N_HEADS
    n_u = S5_GROUPS * S5_GROUP
    off_if = n_qk + n_v
    off_u = off_if + n_if
    off_o = off_u + n_u
    layer = 0

    pad_rows = -(bsz + 1) % 8
    c_rows = jnp.concatenate([c, c_ctx[None, :], jnp.zeros((pad_rows, d), F32)], axis=0)
    mod_all = _ada(c_rows, w_ada[layer], b_ada[layer])
    mod = mod_all[:bsz].reshape(bsz, 6, d)
    mod_c = mod_all[bsz, :2 * d].reshape(2, 1, 1, d)

    w_l = w_in[layer]
    w_state = jnp.concatenate([w_l[:, :off_if], w_l[:, off_u:off_o]], axis=1).astype(BF16)
    w_if_t = w_l[:, off_if:off_u].T.astype(BF16)
    b_if = b_ifgate[layer].reshape(n_if)
    proj_c = _proj(ctx, g_norm1[layer], mod_c[1], mod_c[0], w_state, w_if_t, b_if, n_qk, n_v, n_u, l_ctx)
    proj_l = _proj(x, g_norm1[layer], mod[:, 1:2], mod[:, 0:1], w_state, w_if_t, b_if, n_qk, n_v, n_u, 512)
    qk_c, v_c, u_c, gif_c = proj_c
    qk_l, v_l, u_l, gif_l = proj_l

    w9 = w_conv_qk[layer].reshape(9, n_qk)
    q_c, kt_c = _conv(qk_c, w9, D_QK ** -0.5, l_ctx, l_ctx)
    q_l, kt_l = _conv(qk_l, w9, D_QK ** -0.5, GRID_W, 512)

    st0 = jnp.zeros((bsz, 2 * N_HEADS, D_QK, 2 * D_V), F32)
    m0 = jnp.zeros((bsz, 2 * N_HEADS, 1), F32)
    _, _, st_c, m_c = _mlstm(q_c, kt_c, v_c, gif_c, st0, m0)
    h_f, h_r, _, _ = _mlstm(q_l, kt_l, v_l, gif_l, st_c, m_c)

    w_in_t, m_tz_t, c_out_t, mu_a, mu_b = _s5_tables(
        s5_a_re[layer], s5_a_im[layer], s5_log_dt[layer], s5_b_re[layer], s5_b_im[layer],
        s5_c_re[layer], s5_c_im[layer], s5_d[layer])
    hs0 = jnp.zeros((bsz, S5_GROUPS, 4 * S5_STATE), F32)
    hs_c, = _s5(u_c, hs0, w_in_t, m_tz_t, c_out_t, mu_a, mu_b)
    y_b, _ = _s5(u_l, hs_c, w_in_t, m_tz_t, c_out_t, mu_a, mu_b,
                 w_glu[layer].T.astype(BF16), b_glu[layer], w_branch_s[layer].T.astype(BF16))

    w_og = w_l[:, off_o:].astype(BF16)
    x1, h2, top_i, top_p, rank, counts = _mixout(
        x, h_f, h_r, y_b, mod, g_norm1[layer], g_norm2[layer], g_mh[layer], w_og, w_branch_m[layer].astype(BF16),
        b_merge_gate[layer], w_o[layer].astype(BF16), w_router[layer].astype(BF16), b_router[layer], 512)

    tm_e = 512
    n_tok = bsz * length
    nt_max = n_tok * TOP_K // tm_e + N_EXPERTS
    row_start, tile_expert, tile_live, n_tiles = _tile_table(counts.reshape(N_EXPERTS), tm_e, nt_max)
    pos = _rowpos(top_i.reshape(n_tok, 8), rank.reshape(n_tok, 8), row_start.reshape(1, N_EXPERTS), min(4096, n_tok))
    pos_flat = pos[:, :TOP_K].reshape(n_tok * TOP_K)
    xs = _dispatch(pos_flat, h2.reshape(n_tok, d), nt_max * tm_e, 512)
    y_rows = _experts(tile_expert, tile_live, n_tiles, xs, w_e_in[layer], b_e_in[layer], w_e_out[layer],
                      b_e_out[layer], tm_e)
    out = _final(pos_flat, x1.reshape(n_tok, d), top_p.reshape(n_tok, 8), mod[:, 5:6], g_final, y_rows, 256)
    return out.reshape(bsz, length, d)
```

```python
import functools
import math

import jax
import jax.numpy as jnp
from jax import lax
from jax.experimental import pallas as pl
from jax.experimental.pallas import tpu as pltpu

F32 = jnp.float32
BF16 = jnp.bfloat16
EPS = 1e-6

N_HEADS = 8
D_QK = 64
D_V = 128
M_CHUNK = 128
GRID_W = 64
S5_GROUPS = 32
S5_GROUP = 16
S5_STATE = 64
S5_CHUNK = 16
N_EXPERTS = 32
TOP_K = 4
SWIGLU_LIMIT = 7.0
SWIGLU_ALPHA = 1.702

LANES = 128
SUBLANES = 8
V7X_VMEM_BYTES = 64 * 1024 * 1024
_VMEM_CAP = V7X_VMEM_BYTES - 8 * 1024 * 1024


def _cparams(n_axes, vmem_bytes):
    limit = int(min(_VMEM_CAP, max(32 * 1024 * 1024, vmem_bytes)))
    return pltpu.CompilerParams(dimension_semantics=("arbitrary",) * n_axes, vmem_limit_bytes=limit)


def _silu(x):
    return x * jax.nn.sigmoid(x)


def _norm_mod(x, g, scale, shift):
    ms = jnp.mean(x * x, axis=-1, keepdims=True)
    return (x * lax.rsqrt(ms + EPS) * g) * (1.0 + scale) + shift


def _split3(x):
    hi = x.astype(BF16)
    r1 = x - hi.astype(F32)
    mid = r1.astype(BF16)
    lo = (r1 - mid.astype(F32)).astype(BF16)
    return hi, mid, lo


def _dot_nt(a, b):
    return lax.dot_general(a, b, (((1,), (1,)), ((), ())), preferred_element_type=F32)


def _dot_tn(a, b):
    return lax.dot_general(a, b, (((0,), (0,)), ((), ())), preferred_element_type=F32)


def _ada_body(c_ref, w_ref, b_ref, o_ref):
    s = _silu(c_ref[...])
    o_ref[...] = jnp.dot(s, w_ref[...], preferred_element_type=F32,
                         precision=lax.Precision.HIGHEST) + b_ref[...]


def _ada(c_rows, w_ada, b_ada):
    rows, d = c_rows.shape
    n = w_ada.shape[1]
    tn = 1024
    return pl.pallas_call(
        _ada_body,
        grid=(n // tn,),
        in_specs=[pl.BlockSpec((rows, d), lambda j: (0, 0)),
                  pl.BlockSpec((d, tn), lambda j: (0, j)),
                  pl.BlockSpec((1, tn), lambda j: (0, j))],
        out_specs=pl.BlockSpec((rows, tn), lambda j: (0, j)),
        out_shape=jax.ShapeDtypeStruct((rows, n), F32),
        compiler_params=_cparams(1, 4 * d * tn * 4),
        name="ada",
    )(c_rows, w_ada, b_ada.reshape(1, n))


def _proj_body(x_ref, g_ref, sc_ref, sh_ref, w_ref, wif_ref, bif_ref, qk_ref, v_ref, u_ref, gif_ref):
    h = _norm_mod(x_ref[0], g_ref[...], sc_ref[0], sh_ref[0]).astype(BF16)
    n_qk = qk_ref.shape[-1]
    n_v = v_ref.shape[-1]
    qk_ref[0] = jnp.dot(h, w_ref[:, :n_qk], preferred_element_type=F32).astype(BF16)
    v_ref[0] = jnp.dot(h, w_ref[:, n_qk:n_qk + n_v], preferred_element_type=F32).astype(BF16)
    u = jnp.dot(h, w_ref[:, n_qk + n_v:], preferred_element_type=F32)
    for j in range(u_ref.shape[1]):
        u_ref[0, j] = u[:, j * LANES:(j + 1) * LANES]
    gif_ref[0] = _dot_nt(wif_ref[...], h) + bif_ref[...]


def _proj(x, g, scale, shift, w_state, w_if_t, b_if, n_qk, n_v, n_u, tm):
    bsz, length, d = x.shape
    n_if = b_if.shape[-1]
    per_batch = scale.shape[0] == bsz
    mod_map = (lambda b, i: (b, 0, 0)) if per_batch else (lambda b, i: (0, 0, 0))
    cols = w_state.shape[1]
    vmem = 2 * (tm * d * 4 + d * cols * 2 + tm * (n_qk + n_v + n_u) * 2 + tm * 128 * 4) + 6 * tm * d * 4
    return pl.pallas_call(
        _proj_body,
        grid=(bsz, length // tm),
        in_specs=[pl.BlockSpec((1, tm, d), lambda b, i: (b, i, 0)),
                  pl.BlockSpec((1, d), lambda b, i: (0, 0)),
                  pl.BlockSpec((1, 1, d), mod_map),
                  pl.BlockSpec((1, 1, d), mod_map),
                  pl.BlockSpec((d, cols), lambda b, i: (0, 0)),
                  pl.BlockSpec((n_if, d), lambda b, i: (0, 0)),
                  pl.BlockSpec((n_if, 1), lambda b, i: (0, 0))],
        out_specs=[pl.BlockSpec((1, tm, n_qk), lambda b, i: (b, i, 0)),
                   pl.BlockSpec((1, tm, n_v), lambda b, i: (b, i, 0)),
                   pl.BlockSpec((1, n_u // LANES, tm, LANES), lambda b, i: (b, 0, i, 0)),
                   pl.BlockSpec((1, n_if, tm), lambda b, i: (b, 0, i))],
        out_shape=[jax.ShapeDtypeStruct((bsz, length, n_qk), BF16),
                   jax.ShapeDtypeStruct((bsz, length, n_v), BF16),
                   jax.ShapeDtypeStruct((bsz, n_u // LANES, length, LANES), F32),
                   jax.ShapeDtypeStruct((bsz, n_if, length), F32)],
        compiler_params=_cparams(2, vmem),
        name="proj",
    )(x, g.reshape(1, d), scale, shift, w_state, w_if_t, b_if.reshape(n_if, 1))


def _conv_body(main_ref, prev_ref, next_ref, w_ref, q_ref, kt_ref, *, width, q_scale):
    i = pl.program_id(1)
    last = pl.num_programs(1) - 1
    t = main_ref.shape[1]
    n = t + 2 * width
    main = main_ref[0].astype(F32)
    prev = jnp.where(i > 0, prev_ref[0].astype(F32), 0.0)
    nxt = jnp.where(i < last, next_ref[0].astype(F32), 0.0)
    ext = jnp.concatenate([prev, main, nxt], axis=0)
    col = lax.broadcasted_iota(jnp.int32, (t, 1), 0) % width
    acc = None
    for dx in (-1, 0, 1):
        shifted = ext if dx == 0 else pltpu.roll(ext, (-dx) % n, axis=0)
        part = None
        for dy in (-1, 0, 1):
            tap = w_ref[(dy + 1) * 3 + (dx + 1):(dy + 1) * 3 + (dx + 1) + 1, :]
            term = tap * shifted[width + dy * width:width + dy * width + t]
            part = term if part is None else part + term
        if dx == -1:
            part = jnp.where(col == 0, 0.0, part)
        elif dx == 1:
            part = jnp.where(col == width - 1, 0.0, part)
        acc = part if acc is None else acc + part
    y = _silu(acc)

    @pl.when(pl.program_id(2) == 0)
    def _():
        q_ref[0] = (y * q_scale).astype(q_ref.dtype)

    @pl.when(pl.program_id(2) == 1)
    def _():
        kt_ref[0] = y.T.astype(kt_ref.dtype)


def _conv(qk_pre, w9, q_scale, width, t_block):
    bsz, length, ch2 = qk_pre.shape
    ch = ch2 // 2
    rpb = t_block // width
    n_rows = length // width
    vmem = 4 * (t_block + 2 * width) * ch * 2 + 14 * (t_block + 2 * width) * ch * 4
    return pl.pallas_call(
        functools.partial(_conv_body, width=width, q_scale=q_scale),
        grid=(bsz, length // t_block, 2),
        in_specs=[pl.BlockSpec((1, t_block, ch), lambda b, i, c: (b, i, c)),
                  pl.BlockSpec((1, width, ch), lambda b, i, c: (b, jnp.maximum(i * rpb - 1, 0), c)),
                  pl.BlockSpec((1, width, ch), lambda b, i, c: (b, jnp.minimum((i + 1) * rpb, n_rows - 1), c)),
                  pl.BlockSpec((9, ch), lambda b, i, c: (0, c))],
        out_specs=[pl.BlockSpec((1, t_block, ch), lambda b, i, c: (b, i, 0)),
                   pl.BlockSpec((1, ch, t_block), lambda b, i, c: (b, 0, i))],
        out_shape=[jax.ShapeDtypeStruct((bsz, length, ch), BF16),
                   jax.ShapeDtypeStruct((bsz, ch, length), BF16)],
        compiler_params=_cparams(3, vmem),
        name="conv",
    )(qk_pre, qk_pre, qk_pre, w9)


def _log_sigmoid(x):
    return jnp.minimum(x, 0.0) - jnp.log1p(jnp.exp(-jnp.abs(x)))


def _cumsum_lanes_exact(x, reverse):
    t = x.shape[-1]
    r_idx = lax.broadcasted_iota(jnp.int32, (t, t), 0)
    c_idx = lax.broadcasted_iota(jnp.int32, (t, t), 1)
    u01 = jnp.where((r_idx >= c_idx) if reverse else (r_idx <= c_idx), 1.0, 0.0).astype(BF16)
    out = None
    for piece in _split3(x):
        p = jnp.dot(piece, u01, preferred_element_type=F32)
        out = p if out is None else out + p
    return out


def _cummax_lanes(x, reverse):
    t = x.shape[-1]
    lane = lax.broadcasted_iota(jnp.int32, x.shape, 1)
    sh = 1
    while sh < t:
        if reverse:
            cand = jnp.where(lane < t - sh, pltpu.roll(x, t - sh, axis=1), -jnp.inf)
        else:
            cand = jnp.where(lane >= sh, pltpu.roll(x, sh, axis=1), -jnp.inf)
        x = jnp.maximum(x, cand)
        sh *= 2
    return x


def _mlstm_gate_rows(g_ref, m_col, d):
    t = g_ref.shape[-1]
    base = d * 2 * N_HEADS
    li = g_ref[0, base:base + N_HEADS, :]
    lf = _log_sigmoid(g_ref[0, base + N_HEADS:base + 2 * N_HEADS, :])
    b = _cumsum_lanes_exact(lf, reverse=(d == 1))
    g = li - b
    a = jnp.maximum(_cummax_lanes(g, reverse=(d == 1)), m_col)
    end = 0 if d == 1 else t - 1
    a_end = a[:, end:end + 1]
    return dict(g=g, a=a, ie=jnp.exp(m_col - a), emt=jnp.exp(-b - a), we=jnp.exp(g - a_end),
                dec=jnp.exp(m_col - a_end), m_new=b[:, end:end + 1] + a_end)


def _mlstm_body(qf_ref, kf_ref, vf_ref, gf_ref, qr_ref, kr_ref, vr_ref, gr_ref, st0_ref, m0_ref,
                hf_ref, hr_ref, st_ref, m_ref):
    @pl.when(pl.program_id(1) == 0)
    def _():
        st_ref[...] = st0_ref[...]
        m_ref[...] = m0_ref[...]

    t = qf_ref.shape[1]
    m_all = m_ref[0]
    rows = [_mlstm_gate_rows(g_ref, m_all[d * N_HEADS:(d + 1) * N_HEADS], d) for d, g_ref in ((0, gf_ref), (1, gr_ref))]
    both = lambda name: jnp.concatenate([rows[0][name], rows[1][name]], axis=0)
    n_hd = 2 * N_HEADS
    pad = jnp.zeros((128 - 3 * n_hd, t), F32)
    cols = jnp.concatenate([both('a'), both('ie'), both('emt'), pad], axis=0).T
    m_ref[0] = both('m_new')
    r_idx = lax.broadcasted_iota(jnp.int32, (t, t), 0)
    c_idx = lax.broadcasted_iota(jnp.int32, (t, t), 1)
    ones_blk = jnp.ones((t, D_V), BF16)
    for d, (q_ref, kt_ref, v_ref, h_ref) in enumerate(((qf_ref, kf_ref, vf_ref, hf_ref), (qr_ref, kr_ref, vr_ref, hr_ref))):
        causal = (c_idx >= r_idx) if d == 1 else (c_idx <= r_idx)
        for h in range(N_HEADS):
            j = d * N_HEADS + h
            q = q_ref[0, :, h * D_QK:(h + 1) * D_QK]
            kt = kt_ref[0, h * D_QK:(h + 1) * D_QK, :]
            v1 = jnp.concatenate([v_ref[0, :, h * D_V:(h + 1) * D_V], ones_blk], axis=1)
            state = st_ref[0, j]
            a_col = cols[:, j:j + 1]
            ie_col = cols[:, n_hd + j:n_hd + j + 1]
            emt_col = cols[:, 2 * n_hd + j:2 * n_hd + j + 1]
            dmat = jnp.exp(jnp.where(causal, rows[d]['g'][h:h + 1, :] - a_col, -jnp.inf))
            s = (jnp.dot(q, kt, preferred_element_type=F32) * dmat).astype(BF16)
            z = jnp.dot(q, state.astype(BF16), preferred_element_type=F32)
            p = jnp.dot(s, v1, preferred_element_type=F32)
            num = ie_col * z[:, :D_V] + p[:, :D_V]
            den = ie_col * z[:, D_V:] + p[:, D_V:]
            h_ref[0, :, h * D_V:(h + 1) * D_V] = (num / jnp.maximum(jnp.abs(den), emt_col)).astype(h_ref.dtype)
            kw = (kt.astype(F32) * rows[d]['we'][h:h + 1, :]).astype(BF16)
            st_ref[0, j] = rows[d]['dec'][h:h + 1, :] * state + jnp.dot(kw, v1, preferred_element_type=F32)


def _mlstm(q, k_t, v, gif_t, st0, m0):
    bsz, length, hv = v.shape
    t = M_CHUNK
    nc = length // t
    hq = N_HEADS * D_QK
    ng = gif_t.shape[1]
    fwd = lambda b, i: (b, i, 0)
    rev = lambda b, i: (b, nc - 1 - i, 0)
    fwd_t = lambda b, i: (b, 0, i)
    rev_t = lambda b, i: (b, 0, nc - 1 - i)
    st_spec = pl.BlockSpec((1,) + st0.shape[1:], lambda b, i: (b, 0, 0, 0))
    m_spec = pl.BlockSpec((1,) + m0.shape[1:], lambda b, i: (b, 0, 0))
    vmem = 24 * 1024 * 1024
    return pl.pallas_call(
        _mlstm_body,
        grid=(bsz, nc),
        in_specs=[pl.BlockSpec((1, t, hq), fwd), pl.BlockSpec((1, hq, t), fwd_t),
                  pl.BlockSpec((1, t, hv), fwd), pl.BlockSpec((1, ng, t), fwd_t),
                  pl.BlockSpec((1, t, hq), rev), pl.BlockSpec((1, hq, t), rev_t),
                  pl.BlockSpec((1, t, hv), rev), pl.BlockSpec((1, ng, t), rev_t),
                  st_spec, m_spec],
        out_specs=[pl.BlockSpec((1, t, hv), fwd), pl.BlockSpec((1, t, hv), rev), st_spec, m_spec],
        out_shape=[jax.ShapeDtypeStruct((bsz, length, hv), BF16),
                   jax.ShapeDtypeStruct((bsz, length, hv), BF16),
                   jax.ShapeDtypeStruct(st0.shape, F32),
                   jax.ShapeDtypeStruct(m0.shape, F32)],
        compiler_params=_cparams(2, vmem),
        name="mlstm",
    )(q, k_t, v, gif_t, q, k_t, v, gif_t, st0, m0)


def _s5_tables(a_re, a_im, log_dt, b_re, b_im, c_re, c_im, d_skip):
    hp = lax.Precision.HIGHEST
    t = S5_CHUNK
    n_dir, groups, p = a_re.shape
    cg = b_re.shape[-1]
    dt = jnp.exp(log_dt)[..., None]

    def lam_pow(n):
        mag = jnp.exp(n * (dt * a_re)[..., None])
        ang = n * (dt * a_im)[..., None]
        return mag * jnp.cos(ang), mag * jnp.sin(ang)

    ab_re, ab_im = (z[..., 0] for z in lam_pow(jnp.ones((1,), F32)))
    den = a_re * a_re + a_im * a_im
    xr = ab_re - 1.0
    cf_re = (xr * a_re + ab_im * a_im) / den
    cf_im = (ab_im * a_re - xr * a_im) / den
    bb_re = cf_re[..., None] * b_re - cf_im[..., None] * b_im
    bb_im = cf_re[..., None] * b_im + cf_im[..., None] * b_re
    jj = jnp.arange(t + 1, dtype=F32)
    lp_re, lp_im = lam_pow(jj)

    def w_dir(d, exps):
        lr = lp_re[d][:, :, exps]
        li = lp_im[d][:, :, exps]
        wr = lr[..., None] * bb_re[d][:, :, None, :] - li[..., None] * bb_im[d][:, :, None, :]
        wi = lr[..., None] * bb_im[d][:, :, None, :] + li[..., None] * bb_re[d][:, :, None, :]
        to_rows = lambda w: jnp.transpose(w, (0, 2, 3, 1)).reshape(groups, t * cg, p)
        return to_rows(wr), to_rows(wi)

    s_idx = jnp.arange(t)
    wf_re, wf_im = w_dir(0, t - 1 - s_idx)
    wr_re, wr_im = w_dir(1, s_idx)
    zw = jnp.zeros_like(wf_re)
    w_in = jnp.concatenate([jnp.concatenate([wf_re, zw, wf_im, zw], axis=-1),
                            jnp.concatenate([zw, wr_re, zw, wr_im], axis=-1)], axis=1)

    def c_dir(d, exps):
        lr = lp_re[d][:, :, exps]
        li = lp_im[d][:, :, exps]
        cr = jnp.transpose(c_re[d], (0, 2, 1))
        ci = jnp.transpose(c_im[d], (0, 2, 1))
        o_re = cr[:, :, None, :] * lr[..., None] - ci[:, :, None, :] * li[..., None]
        o_im = cr[:, :, None, :] * li[..., None] + ci[:, :, None, :] * lr[..., None]
        return o_re.reshape(groups, p, t * cg), (-o_im).reshape(groups, p, t * cg)

    cf_r, cf_i = c_dir(0, s_idx + 1)
    cr_r, cr_i = c_dir(1, t - s_idx)
    c_out = jnp.concatenate([cf_r, cr_r, cf_i, cr_i], axis=1)

    def k_dir(d):
        lr = lp_re[d][:, :, :t]
        li = lp_im[d][:, :, :t]
        clr = c_re[d][:, :, :, None] * lr[:, None] - c_im[d][:, :, :, None] * li[:, None]
        cli = c_re[d][:, :, :, None] * li[:, None] + c_im[d][:, :, :, None] * lr[:, None]
        return (jnp.einsum('gqpj,gpc->gjqc', clr, bb_re[d], precision=hp)
                - jnp.einsum('gqpj,gpc->gjqc', cli, bb_im[d], precision=hp))

    kf = k_dir(0)
    kr = k_dir(1)
    lag = s_idx[None, :] - s_idx[:, None]
    resp_f = jnp.where((lag >= 0)[None, :, :, None, None], kf[:, jnp.clip(lag, 0, t - 1)], 0.0)
    resp_r = jnp.where((lag <= 0)[None, :, :, None, None], kr[:, jnp.clip(-lag, 0, t - 1)], 0.0)
    skip = (jnp.eye(t, dtype=F32)[None, :, :, None, None] * jnp.eye(cg, dtype=F32)[None, None, None]
            * d_skip.reshape(groups, 1, 1, cg, 1))
    m_tz = jnp.transpose(resp_f + resp_r + skip, (0, 1, 4, 2, 3)).reshape(groups, t * cg, t * cg)

    n_pow = 8
    kk = (t * (2 ** jnp.arange(n_pow))).astype(F32)
    mp_re, mp_im = lam_pow(kk)
    mp_re = jnp.transpose(mp_re, (0, 1, 3, 2))
    mp_im = jnp.transpose(mp_im, (0, 1, 3, 2))
    mu_re = jnp.concatenate([mp_re[0], mp_re[1]], axis=-1)
    mu_im = jnp.concatenate([mp_im[0], mp_im[1]], axis=-1)
    tr = lambda m: jnp.transpose(m, (0, 2, 1)).astype(BF16)
    return w_in.astype(BF16), tr(m_tz), tr(c_out), mu_re, mu_im


def _pad_rows(x, rows):
    return x if x.shape[0] == rows else jnp.concatenate([x, jnp.zeros((rows - x.shape[0],) + x.shape[1:], x.dtype)], axis=0)


def _s5_body(*refs, nch, n_steps, with_output):
    if with_output:
        (u_ref, h0_ref, win_ref, mtzt_ref, coutt_ref, mure_ref, muim_ref, wglut_ref, bglu_ref, wbt_ref,
         yb_ref, hout_ref, xt_scr, yt_scr) = refs
    else:
        u_ref, h0_ref, win_ref, mure_ref, muim_ref, hout_ref, xt_scr = refs
    t_c, cg, groups = S5_CHUNK, S5_GROUP, S5_GROUPS
    p2 = 2 * S5_STATE
    ncp = xt_scr.shape[-1]
    for t in range(t_c):
        ut = jnp.concatenate([u_ref[0, j, pl.ds(t, nch, stride=t_c), :] for j in range(u_ref.shape[1])], axis=1)
        ut = _pad_rows(ut, ncp)
        xt_scr[:, t] = ut.T.astype(BF16).reshape(groups, cg, ncp)

    r_idx = lax.broadcasted_iota(jnp.int32, (ncp, ncp), 0)
    c_idx = lax.broadcasted_iota(jnp.int32, (ncp, ncp), 1)
    flip = jnp.where(r_idx + c_idx == nch - 1, 1.0, 0.0).astype(BF16)
    row = lax.broadcasted_iota(jnp.int32, (ncp, p2), 0)
    is_fwd = (lax.broadcasted_iota(jnp.int32, (ncp, 2 * p2), 1) % p2) < S5_STATE

    def group(g, carry):
        x = xt_scr[g].reshape(t_c * cg, ncp)
        x_rev = jnp.dot(x, flip, preferred_element_type=F32).astype(BF16)
        local = _dot_tn(jnp.concatenate([x, x_rev], axis=0), win_ref[g])
        l_re, l_im = local[:, :p2], local[:, p2:]
        h0 = h0_ref[0, pl.ds(g, 1), :]
        z_re = jnp.where(row == 0, h0[:, :p2], pltpu.roll(l_re, 1, axis=0))
        z_im = jnp.where(row == 0, h0[:, p2:], pltpu.roll(l_im, 1, axis=0))
        for k in range(n_steps):
            sft = 1 << k
            a_re = mure_ref[g, k:k + 1, :]
            a_im = muim_ref[g, k:k + 1, :]
            p_re = jnp.where(row >= sft, pltpu.roll(z_re, sft, axis=0), 0.0)
            p_im = jnp.where(row >= sft, pltpu.roll(z_im, sft, axis=0), 0.0)
            z_re, z_im = z_re + a_re * p_re - a_im * p_im, z_im + a_re * p_im + a_im * p_re
        if with_output:
            z = jnp.concatenate([z_re, z_im], axis=1).astype(BF16)
            z_flip = jnp.dot(flip, z, preferred_element_type=F32).astype(BF16)
            z_nat = jnp.where(is_fwd, z, z_flip)
            y = jnp.dot(mtzt_ref[g], x, preferred_element_type=F32) + _dot_nt(coutt_ref[g], z_nat)
            yt_scr[g] = jax.nn.gelu(y).astype(BF16).reshape(t_c, cg, ncp)
        a_re = mure_ref[g, 0:1, :]
        a_im = muim_ref[g, 0:1, :]
        e_re, e_im = z_re[nch - 1:nch], z_im[nch - 1:nch]
        hout_ref[0, pl.ds(g, 1), :] = jnp.concatenate(
            [a_re * e_re - a_im * e_im + l_re[nch - 1:nch], a_re * e_im + a_im * e_re + l_im[nch - 1:nch]], axis=1)
        return carry

    lax.fori_loop(0, groups, group, 0, unroll=4)

    if with_output:
        def tail(t, carry):
            ys = yt_scr[:, t].reshape(groups * cg, ncp)
            glu = jax.nn.sigmoid(jnp.dot(wglut_ref[...], ys, preferred_element_type=F32) + bglu_ref[...])
            ybt = jnp.dot(wbt_ref[...], (ys.astype(F32) * glu).astype(BF16), preferred_element_type=F32)
            yb_ref[0, t] = ybt.T[:nch].astype(yb_ref.dtype)
            return carry

        lax.fori_loop(0, t_c, tail, 0)


def _s5(u, h0, w_in_t, m_tz_t, c_out_t, mu_a, mu_b, w_glu_t=None, b_glu=None, w_b_t=None):
    bsz, n_blk, length, _ = u.shape
    width = n_blk * LANES
    with_output = w_glu_t is not None
    nch = length // S5_CHUNK
    ncp = -(-nch // 128) * 128
    p4 = 4 * S5_STATE
    n_steps = max(1, (nch - 1).bit_length())
    tc = S5_CHUNK * S5_GROUP
    one = pl.Buffered(1)
    full = lambda a: pl.BlockSpec(a.shape, lambda b: (0,) * a.ndim, pipeline_mode=one)
    in_specs = [pl.BlockSpec((1, n_blk, length, LANES), lambda b: (b, 0, 0, 0), pipeline_mode=one),
                pl.BlockSpec((1, S5_GROUPS, p4), lambda b: (b, 0, 0)), full(w_in_t)]
    args = [u, h0, w_in_t]
    scratch = [pltpu.VMEM((S5_GROUPS, S5_CHUNK, S5_GROUP, ncp), BF16)]
    out_specs = [pl.BlockSpec((1, S5_GROUPS, p4), lambda b: (b, 0, 0))]
    out_shape = [jax.ShapeDtypeStruct((bsz, S5_GROUPS, p4), F32)]
    vmem = length * width * 4 + 4 * S5_GROUPS * tc * tc * 2 + 2 * S5_GROUPS * tc * ncp * 2 + 8 * 1024 * 1024
    if with_output:
        d_out = w_b_t.shape[0]
        in_specs += [full(m_tz_t), full(c_out_t), full(mu_a), full(mu_b), full(w_glu_t),
                     pl.BlockSpec((width, 1), lambda b: (0, 0)), full(w_b_t)]
        args += [m_tz_t, c_out_t, mu_a, mu_b, w_glu_t, b_glu.reshape(width, 1), w_b_t]
        scratch.append(pltpu.VMEM((S5_GROUPS, S5_CHUNK, S5_GROUP, ncp), BF16))
        out_specs.insert(0, pl.BlockSpec((1, S5_CHUNK, nch, d_out), lambda b: (b, 0, 0, 0)))
        out_shape.insert(0, jax.ShapeDtypeStruct((bsz, S5_CHUNK, nch, d_out), BF16))
        vmem += 2 * S5_CHUNK * nch * d_out * 2 + (width * width + width * d_out) * 2
    else:
        in_specs += [full(mu_a), full(mu_b)]
        args += [mu_a, mu_b]
    return pl.pallas_call(
        functools.partial(_s5_body, nch=nch, n_steps=n_steps, with_output=with_output),
        grid=(bsz,),
        in_specs=in_specs,
        out_specs=out_specs,
        out_shape=out_shape,
        scratch_shapes=scratch,
        compiler_params=_cparams(1, vmem),
        name="s5",
    )(*args)


def _mixout_body(x_ref, hf_ref, hr_ref, yb_ref, mod_ref, g1_ref, g2_ref, gmh_ref, wog_ref, wa_ref,
                 bgate_ref, wo_ref, wr_ref, br_ref,
                 x1_ref, h2_ref, ti_ref, tp_ref, rk_ref, cnt_ref, base_ref, yb_scr):
    x = x_ref[0]
    d = x.shape[-1]
    mod = mod_ref[0]
    h = _norm_mod(x, g1_ref[...], mod[1:2], mod[0:1]).astype(BF16)
    og = jnp.dot(h, wog_ref[...], preferred_element_type=F32)
    hm = hf_ref[0].astype(F32) + hr_ref[0].astype(F32)
    heads = []
    for hd in range(N_HEADS):
        blk = hm[:, hd * D_V:(hd + 1) * D_V]
        heads.append(blk * lax.rsqrt(jnp.mean(blk * blk, axis=-1, keepdims=True) + EPS))
    hn = jnp.concatenate(heads, axis=1) * gmh_ref[...]
    y_a = jnp.dot((hn * jax.nn.sigmoid(og[:, :d])).astype(BF16), wa_ref[...], preferred_element_type=F32)
    n_pos = yb_ref.shape[1]
    for t in range(n_pos):
        blk = yb_ref[0, t].astype(F32)
        for j in range(yb_scr.shape[0]):
            yb_scr[j, pl.ds(t, yb_ref.shape[2], stride=n_pos), :] = blk[:, j * LANES:(j + 1) * LANES]
    y_b = jnp.concatenate([yb_scr[j] for j in range(yb_scr.shape[0])], axis=1)
    gates = jax.nn.sigmoid(og[:, d:] + bgate_ref[...])
    merged = gates[:, :d] * y_a + gates[:, d:] * y_b
    x1 = x + mod[2:3] * jnp.dot(merged.astype(BF16), wo_ref[...], preferred_element_type=F32)
    x1_ref[0] = x1
    h2f = _norm_mod(x1, g2_ref[...], mod[4:5], mod[3:4])
    h2_ref[0] = h2f
    logits = jnp.dot(h2f.astype(BF16), wr_ref[...], preferred_element_type=F32) + br_ref[...]
    tm, n_e = logits.shape
    e_iota = lax.broadcasted_iota(jnp.int32, (tm, n_e), 1)
    lane = lax.broadcasted_iota(jnp.int32, (tm, ti_ref.shape[-1]), 1)
    ti = jnp.zeros(lane.shape, jnp.int32)
    tv = jnp.zeros(lane.shape, F32)
    top = None
    chosen = []
    for k in range(TOP_K):
        mx = jnp.max(logits, axis=-1, keepdims=True)
        idx = jnp.min(jnp.where(logits == mx, e_iota, n_e), axis=-1, keepdims=True)
        top = mx if top is None else top
        ti = jnp.where(lane == k, idx, ti)
        tv = jnp.where(lane == k, jnp.exp(mx - top), tv)
        chosen.append(e_iota == idx)
        logits = jnp.where(chosen[-1], -jnp.inf, logits)
    ti_ref[0] = ti
    tp_ref[0] = tv / jnp.sum(tv, axis=-1, keepdims=True)

    @pl.when((pl.program_id(0) == 0) & (pl.program_id(1) == 0))
    def _():
        base_ref[...] = jnp.zeros(base_ref.shape, F32)

    onehot = jnp.zeros((tm, n_e), F32)
    for sel in chosen:
        onehot = onehot + jnp.where(sel, 1.0, 0.0)
    below = (lax.broadcasted_iota(jnp.int32, (tm, tm), 1) < lax.broadcasted_iota(jnp.int32, (tm, tm), 0))
    before = jnp.dot(jnp.where(below, 1.0, 0.0).astype(BF16), onehot.astype(BF16),
                     preferred_element_type=F32) + base_ref[...]
    rk = jnp.zeros(lane.shape, jnp.int32)
    for k, sel in enumerate(chosen):
        rank = jnp.sum(jnp.where(sel, before, 0.0), axis=-1, keepdims=True)
        rk = jnp.where(lane == k, rank.astype(jnp.int32), rk)
    rk_ref[0] = rk
    base_ref[...] = base_ref[...] + jnp.sum(onehot, axis=0, keepdims=True)
    cnt_ref[...] = base_ref[...].astype(jnp.int32)


def _mixout(x, h_f, h_r, y_b, mod, g1, g2, g_mh, w_og, w_a, b_gate, w_o, w_r, b_r, tm):
    bsz, length, d = x.shape
    n_pos = y_b.shape[1]
    n_e = w_r.shape[-1]
    tok = lambda b, i: (b, i, 0)
    const2 = lambda b, i: (0, 0)

    def wspec(w):
        return pl.BlockSpec(w.shape, const2, pipeline_mode=pl.Buffered(1))

    weights = (w_og, w_a, w_o, w_r)
    w_bytes = sum(int(w.size) * w.dtype.itemsize for w in weights)
    vmem = w_bytes + 2 * tm * (d * 4 + 3 * d * 2 + d * 4 + d * 4) + 15 * tm * d * 4
    return pl.pallas_call(
        _mixout_body,
        grid=(bsz, length // tm),
        in_specs=[pl.BlockSpec((1, tm, d), tok), pl.BlockSpec((1, tm, d), tok), pl.BlockSpec((1, tm, d), tok),
                  pl.BlockSpec((1, n_pos, tm // n_pos, d), lambda b, i: (b, 0, i, 0)),
                  pl.BlockSpec((1, 6, d), lambda b, i: (b, 0, 0)),
                  pl.BlockSpec((1, d), const2), pl.BlockSpec((1, d), const2), pl.BlockSpec((1, d), const2),
                  wspec(w_og), wspec(w_a),
                  pl.BlockSpec((1, 2 * d), const2), wspec(w_o), wspec(w_r), pl.BlockSpec((1, n_e), const2)],
        out_specs=[pl.BlockSpec((1, tm, d), tok), pl.BlockSpec((1, tm, d), tok),
                   pl.BlockSpec((1, tm, 8), tok), pl.BlockSpec((1, tm, 8), tok), pl.BlockSpec((1, tm, 8), tok),
                   pl.BlockSpec((1, n_e), const2)],
        out_shape=[jax.ShapeDtypeStruct((bsz, length, d), F32),
                   jax.ShapeDtypeStruct((bsz, length, d), F32),
                   jax.ShapeDtypeStruct((bsz, length, 8), jnp.int32),
                   jax.ShapeDtypeStruct((bsz, length, 8), F32),
                   jax.ShapeDtypeStruct((bsz, length, 8), jnp.int32),
                   jax.ShapeDtypeStruct((1, n_e), jnp.int32)],
        scratch_shapes=[pltpu.VMEM((1, n_e), F32), pltpu.VMEM((d // LANES, tm, LANES), F32)],
        compiler_params=_cparams(2, vmem),
        name="mixout",
    )(x, h_f, h_r, y_b, mod, g1.reshape(1, d), g2.reshape(1, d), g_mh.reshape(1, d), w_og, w_a,
      b_gate.reshape(1, 2 * d), w_o, w_r, b_r.reshape(1, n_e))


def _rowpos_body(ti_ref, rk_ref, rs_ref, pos_ref):
    ti = ti_ref[...]
    rk = rk_ref[...]
    n, w = ti.shape
    n_e = rs_ref.shape[-1]
    e_iota = lax.broadcasted_iota(jnp.int32, (n, n_e), 1)
    lane = lax.broadcasted_iota(jnp.int32, (n, w), 1)
    pos = jnp.zeros((n, w), jnp.int32)
    for k in range(TOP_K):
        start = jnp.sum(jnp.where(e_iota == ti[:, k:k + 1], rs_ref[...], 0), axis=-1, keepdims=True)
        pos = jnp.where(lane == k, start + rk[:, k:k + 1], pos)
    pos_ref[...] = pos


def _rowpos(top_i, rank, row_start, tn):
    n, w = top_i.shape
    n_e = row_start.shape[-1]
    return pl.pallas_call(
        _rowpos_body,
        grid=(n // tn,),
        in_specs=[pl.BlockSpec((tn, w), lambda i: (i, 0)), pl.BlockSpec((tn, w), lambda i: (i, 0)),
                  pl.BlockSpec((1, n_e), lambda i: (0, 0))],
        out_specs=pl.BlockSpec((tn, w), lambda i: (i, 0)),
        out_shape=jax.ShapeDtypeStruct((n, w), jnp.int32),
        compiler_params=_cparams(1, 16 * tn * 128 * 4),
        name="rowpos",
    )(top_i, rank, row_start)


def _dispatch_body(pos_ref, h_ref, xs_hbm, sem):
    tm = h_ref.shape[0]

    def issue(i, carry):
        base = pl.multiple_of(i * SUBLANES, SUBLANES)
        for s in range(SUBLANES):
            for k in range(TOP_K):
                row = pos_ref[(base + s) * TOP_K + k]
                pltpu.make_async_copy(h_ref.at[pl.ds(base + s, 1), :], xs_hbm.at[row],
                                      sem).start(priority=k % 2)
        return carry

    lax.fori_loop(0, tm // SUBLANES, issue, 0)
    for k in range(TOP_K):
        pltpu.make_async_copy(h_ref, h_ref, sem).wait()


def _dispatch(pos_flat, h_rows, n_rows, tm):
    n, w = h_rows.shape
    return pl.pallas_call(
        _dispatch_body,
        grid=(n // tm,),
        in_specs=[pl.BlockSpec((tm * TOP_K,), lambda i: (i,), memory_space=pltpu.SMEM),
                  pl.BlockSpec((tm, w), lambda i: (i, 0))],
        out_specs=pl.BlockSpec(memory_space=pl.ANY),
        out_shape=jax.ShapeDtypeStruct((n_rows, 1, w), h_rows.dtype),
        scratch_shapes=[pltpu.SemaphoreType.DMA(())],
        compiler_params=_cparams(1, 8 * tm * w * 4),
        name="dispatch",
    )(pos_flat, h_rows)


def _experts_body(te_ref, tv_ref, nt_ref, xs_ref, win_ref, bin_ref, wout_ref, bout_ref, y_ref, win_bf, wout_bf,
                  *, f_chunk, cast_rows):
    i = pl.program_id(0)
    e = te_ref[i]
    e_prev = te_ref[jnp.maximum(i - 1, 0)]
    d, f2 = win_bf.shape
    f = f2 // 2

    @pl.when((i == 0) | (e != e_prev))
    def _():
        def cast_in(r, carry):
            rows = pl.ds(pl.multiple_of(r * cast_rows, cast_rows), cast_rows)
            win_bf[rows, :] = win_ref[0, rows, :].astype(BF16)
            return carry

        def cast_out(r, carry):
            rows = pl.ds(pl.multiple_of(r * cast_rows, cast_rows), cast_rows)
            wout_bf[rows, :] = wout_ref[0, rows, :].astype(BF16)
            return carry

        lax.fori_loop(0, d // cast_rows, cast_in, 0)
        lax.fori_loop(0, f // cast_rows, cast_out, 0)

    @pl.when(i < nt_ref[0])
    def _():
        xs = xs_ref[:, 0, :]
        live = lax.broadcasted_iota(jnp.int32, xs.shape, 0) < tv_ref[i]
        x = jnp.where(live, xs, 0.0).astype(BF16)
        acc = None
        for c in range(f // f_chunk):
            lo = c * f_chunk
            zg = jnp.dot(x, win_bf[:, lo:lo + f_chunk], preferred_element_type=F32) + bin_ref[0, :, lo:lo + f_chunk]
            zl = (jnp.dot(x, win_bf[:, f + lo:f + lo + f_chunk], preferred_element_type=F32)
                  + bin_ref[0, :, f + lo:f + lo + f_chunk])
            glu = jnp.minimum(zg, SWIGLU_LIMIT)
            lin = jnp.clip(zl, -SWIGLU_LIMIT, SWIGLU_LIMIT)
            act = glu * jax.nn.sigmoid(SWIGLU_ALPHA * glu) * (lin + 1.0)
            part = jnp.dot(act.astype(BF16), wout_bf[lo:lo + f_chunk, :], preferred_element_type=F32)
            acc = part if acc is None else acc + part
        y_ref[:, 0, :] = acc + bout_ref[0]

    @pl.when(i >= nt_ref[0])
    def _():
        y_ref[...] = jnp.zeros(y_ref.shape, y_ref.dtype)


def _experts(tile_expert, tile_valid, n_tiles, xs, w_e_in, b_e_in, w_e_out, b_e_out, tm):
    rows, _, d = xs.shape
    n_e, _, f2 = w_e_in.shape
    f = f2 // 2
    nt_max = rows // tm
    row_map = lambda i, te, tv, nt: (jnp.minimum(i, nt[0] - 1), 0, 0)
    exp_map = lambda i, te, tv, nt: (te[i], 0, 0)
    vmem = 2 * (d * f2 + f * d) * 4 + (d * f2 + f * d) * 2 + 4 * tm * d * 2 + 10 * tm * d * 4
    grid_spec = pltpu.PrefetchScalarGridSpec(
        num_scalar_prefetch=3,
        grid=(nt_max,),
        in_specs=[pl.BlockSpec((tm, 1, d), row_map),
                  pl.BlockSpec((1, d, f2), exp_map), pl.BlockSpec((1, 1, f2), exp_map),
                  pl.BlockSpec((1, f, d), exp_map), pl.BlockSpec((1, 1, d), exp_map)],
        out_specs=pl.BlockSpec((tm, 1, d), lambda i, te, tv, nt: (i, 0, 0)),
        scratch_shapes=[pltpu.VMEM((d, f2), BF16), pltpu.VMEM((f, d), BF16)],
    )
    return pl.pallas_call(
        functools.partial(_experts_body, f_chunk=512, cast_rows=128),
        grid_spec=grid_spec,
        out_shape=jax.ShapeDtypeStruct((rows, 1, d), F32),
        compiler_params=_cparams(1, vmem),
        name="experts",
    )(tile_expert, tile_valid, n_tiles, xs, w_e_in, b_e_in.reshape(n_e, 1, f2), w_e_out, b_e_out.reshape(n_e, 1, d))


def _final_body(pos_ref, posn_ref, x1_ref, tp_ref, gt_ref, g_ref, y_hbm, o_ref, buf, sem):
    i = pl.program_id(0)
    n = pl.num_programs(0)
    tm = x1_ref.shape[0]

    def gather(p_ref, slot):
        def issue(i, carry):
            base = pl.multiple_of(i * SUBLANES, SUBLANES)
            for s in range(SUBLANES):
                for k in range(TOP_K):
                    row = p_ref[(base + s) * TOP_K + k]
                    pltpu.make_async_copy(y_hbm.at[row], buf.at[slot, k, pl.ds(base + s, 1), :],
                                          sem.at[slot]).start(priority=k % 2)
            return carry

        lax.fori_loop(0, tm // SUBLANES, issue, 0)

    @pl.when(i == 0)
    def _():
        gather(pos_ref, 0)

    @pl.when(i + 1 < n)
    def _():
        gather(posn_ref, (i + 1) % 2)

    slot = i % 2
    for k in range(TOP_K):
        pltpu.make_async_copy(buf.at[slot, k], buf.at[slot, k], sem.at[slot]).wait()
    moe = None
    for k in range(TOP_K):
        term = tp_ref[:, k:k + 1] * buf[slot, k]
        moe = term if moe is None else moe + term
    x2 = x1_ref[...] + gt_ref[0] * moe
    o_ref[...] = x2 * lax.rsqrt(jnp.mean(x2 * x2, axis=-1, keepdims=True) + EPS) * g_ref[...]


def _final(pos_flat, x1, top_p, gt2, g_final, y_rows, tm):
    n_tok, d = x1.shape
    bsz = gt2.shape[0]
    per_b = n_tok // bsz // tm
    nxt = lambda i: (jnp.minimum(i + 1, n_tok // tm - 1),)
    return pl.pallas_call(
        _final_body,
        grid=(n_tok // tm,),
        in_specs=[pl.BlockSpec((tm * TOP_K,), lambda i: (i,), memory_space=pltpu.SMEM),
                  pl.BlockSpec((tm * TOP_K,), nxt, memory_space=pltpu.SMEM),
                  pl.BlockSpec((tm, d), lambda i: (i, 0)),
                  pl.BlockSpec((tm, top_p.shape[-1]), lambda i: (i, 0)),
                  pl.BlockSpec((1, 1, d), lambda i: (i // per_b, 0, 0)),
                  pl.BlockSpec((1, d), lambda i: (0, 0)),
                  pl.BlockSpec(memory_space=pl.ANY)],
        out_specs=pl.BlockSpec((tm, d), lambda i: (i, 0)),
        out_shape=jax.ShapeDtypeStruct((n_tok, d), F32),
        scratch_shapes=[pltpu.VMEM((2, TOP_K, tm, d), F32), pltpu.SemaphoreType.DMA((2,))],
        compiler_params=_cparams(1, 2 * TOP_K * tm * d * 4 + 4 * tm * d * 4 + 8 * tm * d * 4),
        name="final",
    )(pos_flat, pos_flat, x1, top_p, gt2, g_final.reshape(1, d), y_rows)


def _tile_table(counts, tm, nt_max):
    tiles = (counts + tm - 1) // tm
    tile_end = jnp.cumsum(tiles)
    tile_start = tile_end - tiles
    n_tiles = tile_end[-1]
    tile_ids = jnp.arange(nt_max, dtype=jnp.int32)
    last_e = jnp.sum((n_tiles - 1) >= tile_end).astype(jnp.int32)
    te = jnp.sum(tile_ids[:, None] >= tile_end[None, :], axis=1).astype(jnp.int32)
    te = jnp.where(tile_ids < n_tiles, te, last_e)
    sel = te[:, None] == jnp.arange(counts.shape[0], dtype=jnp.int32)[None, :]
    cnt_t = jnp.sum(jnp.where(sel, counts[None, :], 0), axis=1)
    start_t = jnp.sum(jnp.where(sel, tile_start[None, :], 0), axis=1)
    live = jnp.clip(cnt_t - (tile_ids - start_t) * tm, 0, tm).astype(jnp.int32)
    live = jnp.where(tile_ids < n_tiles, live, 0)
    return (tile_start * tm).astype(jnp.int32), te, live, n_tiles.reshape(1).astype(jnp.int32)


def kernel(x, c, ctx, c_ctx, w_ada, b_ada, g_norm1, g_norm2, w_in, w_conv_qk, b_ifgate, g_mh, w_branch_m,
           s5_a_re, s5_a_im, s5_log_dt, s5_b_re, s5_b_im, s5_c_re, s5_c_im, s5_d, w_glu, b_glu, w_branch_s,
           b_merge_gate, w_o, w_router, b_router, w_e_in, b_e_in, w_e_out, b_e_out, g_final):
    bsz, length, d = x.shape
    l_ctx = ctx.shape[1]
    n_qk = 2 * N_HEADS * D_QK
    n_v = N_HEADS * D_V
    n_if = 4 * N_HEADS
    n_u = S5_GROUPS * S5_GROUP
    off_if = n_qk + n_v
    off_u = off_if + n_if
    off_o = off_u + n_u
    layer = 0

    pad_rows = -(bsz + 1) % 8
    c_rows = jnp.concatenate([c, c_ctx[None, :], jnp.zeros((pad_rows, d), F32)], axis=0)
    mod_all = _ada(c_rows, w_ada[layer], b_ada[layer])
    mod = mod_all[:bsz].reshape(bsz, 6, d)
    mod_c = mod_all[bsz, :2 * d].reshape(2, 1, 1, d)

    w_l = w_in[layer]
    w_state = jnp.concatenate([w_l[:, :off_if], w_l[:, off_u:off_o]], axis=1).astype(BF16)
    w_if_t = w_l[:, off_if:off_u].T.astype(BF16)
    b_if = b_ifgate[layer].reshape(n_if)
    proj_c = _proj(ctx, g_norm1[layer], mod_c[1], mod_c[0], w_state, w_if_t, b_if, n_qk, n_v, n_u, l_ctx)
    proj_l = _proj(x, g_norm1[layer], mod[:, 1:2], mod[:, 0:1], w_state, w_if_t, b_if, n_qk, n_v, n_u, 512)
    qk_c, v_c, u_c, gif_c = proj_c
    qk_l, v_l, u_l, gif_l = proj_l

    w9 = w_conv_qk[layer].reshape(9, n_qk)
    q_c, kt_c = _conv(qk_c, w9, D_QK ** -0.5, l_ctx, l_ctx)
    q_l, kt_l = _conv(qk_l, w9, D_QK ** -0.5, GRID_W, 512)

    st0 = jnp.zeros((bsz, 2 * N_HEADS, D_QK, 2 * D_V), F32)
    m0 = jnp.zeros((bsz, 2 * N_HEADS, 1), F32)
    _, _, st_c, m_c = _mlstm(q_c, kt_c, v_c, gif_c, st0, m0)
    h_f, h_r, _, _ = _mlstm(q_l, kt_l, v_l, gif_l, st_c, m_c)

    w_in_t, m_tz_t, c_out_t, mu_a, mu_b = _s5_tables(
        s5_a_re[layer], s5_a_im[layer], s5_log_dt[layer], s5_b_re[layer], s5_b_im[layer],
        s5_c_re[layer], s5_c_im[layer], s5_d[layer])
    hs0 = jnp.zeros((bsz, S5_GROUPS, 4 * S5_STATE), F32)
    hs_c, = _s5(u_c, hs0, w_in_t, m_tz_t, c_out_t, mu_a, mu_b)
    y_b, _ = _s5(u_l, hs_c, w_in_t, m_tz_t, c_out_t, mu_a, mu_b,
                 w_glu[layer].T.astype(BF16), b_glu[layer], w_branch_s[layer].T.astype(BF16))

    w_og = w_l[:, off_o:].astype(BF16)
    x1, h2, top_i, top_p, rank, counts = _mixout(
        x, h_f, h_r, y_b, mod, g_norm1[layer], g_norm2[layer], g_mh[layer], w_og, w_branch_m[layer].astype(BF16),
        b_merge_gate[layer], w_o[layer].astype(BF16), w_router[layer].astype(BF16), b_router[layer], 512)

    tm_e = 512
    n_tok = bsz * length
    nt_max = n_tok * TOP_K // tm_e + N_EXPERTS
    row_start, tile_expert, tile_live, n_tiles = _tile_table(counts.reshape(N_EXPERTS), tm_e, nt_max)
    pos = _rowpos(top_i.reshape(n_tok, 8), rank.reshape(n_tok, 8), row_start.reshape(1, N_EXPERTS), min(4096, n_tok))
    pos_flat = pos[:, :TOP_K].reshape(n_tok * TOP_K)
    xs = _dispatch(pos_flat, h2.reshape(n_tok, d), nt_max * tm_e, 512)
    y_rows = _experts(tile_expert, tile_live, n_tiles, xs, w_e_in[layer], b_e_in[layer], w_e_out[layer],
                      b_e_out[layer], tm_e)
    out = _final(pos_flat, x1.reshape(n_tok, d), top_p.reshape(n_tok, 8), mod[:, 5:6], g_final, y_rows, 256)
    return out.reshape(bsz, length, d)
```

```python
import functools
import math

import jax
import jax.numpy as jnp
from jax import lax
from jax.experimental import pallas as pl
from jax.experimental.pallas import tpu as pltpu

F32 = jnp.float32
BF16 = jnp.bfloat16
EPS = 1e-6

N_HEADS = 8
D_QK = 64
D_V = 128
M_CHUNK = 256
GRID_W = 64
S5_GROUPS = 32
S5_GROUP = 16
S5_STATE = 64
S5_CHUNK = 16
N_EXPERTS = 32
TOP_K = 4
SWIGLU_LIMIT = 7.0
SWIGLU_ALPHA = 1.702

LANES = 128
SUBLANES = 8
V7X_VMEM_BYTES = 64 * 1024 * 1024
_VMEM_CAP = V7X_VMEM_BYTES - 8 * 1024 * 1024


def _cparams(n_axes, vmem_bytes):
    limit = int(min(_VMEM_CAP, max(32 * 1024 * 1024, vmem_bytes)))
    return pltpu.CompilerParams(dimension_semantics=("arbitrary",) * n_axes, vmem_limit_bytes=limit)


def _silu(x):
    return x * jax.nn.sigmoid(x)


def _norm_mod(x, g, scale, shift):
    ms = jnp.mean(x * x, axis=-1, keepdims=True)
    return (x * lax.rsqrt(ms + EPS) * g) * (1.0 + scale) + shift


def _split3(x):
    hi = x.astype(BF16)
    r1 = x - hi.astype(F32)
    mid = r1.astype(BF16)
    lo = (r1 - mid.astype(F32)).astype(BF16)
    return hi, mid, lo


def _dot_nt(a, b):
    return lax.dot_general(a, b, (((1,), (1,)), ((), ())), preferred_element_type=F32)


def _dot_tn(a, b):
    return lax.dot_general(a, b, (((0,), (0,)), ((), ())), preferred_element_type=F32)


def _ada_body(c_ref, w_ref, b_ref, o_ref):
    s = _silu(c_ref[...])
    o_ref[...] = jnp.dot(s, w_ref[...], preferred_element_type=F32,
                         precision=lax.Precision.HIGHEST) + b_ref[...]


def _ada(c_rows, w_ada, b_ada):
    rows, d = c_rows.shape
    n = w_ada.shape[1]
    tn = 1024
    return pl.pallas_call(
        _ada_body,
        grid=(n // tn,),
        in_specs=[pl.BlockSpec((rows, d), lambda j: (0, 0)),
                  pl.BlockSpec((d, tn), lambda j: (0, j)),
                  pl.BlockSpec((1, tn), lambda j: (0, j))],
        out_specs=pl.BlockSpec((rows, tn), lambda j: (0, j)),
        out_shape=jax.ShapeDtypeStruct((rows, n), F32),
        compiler_params=_cparams(1, 4 * d * tn * 4),
        name="ada",
    )(c_rows, w_ada, b_ada.reshape(1, n))


def _proj_body(x_ref, g_ref, sc_ref, sh_ref, w_ref, wif_ref, bif_ref, qk_ref, v_ref, u_ref, gif_ref):
    h = _norm_mod(x_ref[0], g_ref[...], sc_ref[0], sh_ref[0]).astype(BF16)
    n_qk = qk_ref.shape[-1]
    n_v = v_ref.shape[-1]
    qk_ref[0] = jnp.dot(h, w_ref[:, :n_qk], preferred_element_type=F32).astype(BF16)
    v_ref[0] = jnp.dot(h, w_ref[:, n_qk:n_qk + n_v], preferred_element_type=F32).astype(BF16)
    u = jnp.dot(h, w_ref[:, n_qk + n_v:], preferred_element_type=F32)
    for j in range(u_ref.shape[1]):
        u_ref[0, j] = u[:, j * LANES:(j + 1) * LANES]
    gif_ref[0] = _dot_nt(wif_ref[...], h) + bif_ref[...]


def _proj(x, g, scale, shift, w_state, w_if_t, b_if, n_qk, n_v, n_u, tm):
    bsz, length, d = x.shape
    n_if = b_if.shape[-1]
    per_batch = scale.shape[0] == bsz
    mod_map = (lambda b, i: (b, 0, 0)) if per_batch else (lambda b, i: (0, 0, 0))
    cols = w_state.shape[1]
    vmem = 2 * (tm * d * 4 + d * cols * 2 + tm * (n_qk + n_v + n_u) * 2 + tm * 128 * 4) + 6 * tm * d * 4
    return pl.pallas_call(
        _proj_body,
        grid=(bsz, length // tm),
        in_specs=[pl.BlockSpec((1, tm, d), lambda b, i: (b, i, 0)),
                  pl.BlockSpec((1, d), lambda b, i: (0, 0)),
                  pl.BlockSpec((1, 1, d), mod_map),
                  pl.BlockSpec((1, 1, d), mod_map),
                  pl.BlockSpec((d, cols), lambda b, i: (0, 0)),
                  pl.BlockSpec((n_if, d), lambda b, i: (0, 0)),
                  pl.BlockSpec((n_if, 1), lambda b, i: (0, 0))],
        out_specs=[pl.BlockSpec((1, tm, n_qk), lambda b, i: (b, i, 0)),
                   pl.BlockSpec((1, tm, n_v), lambda b, i: (b, i, 0)),
                   pl.BlockSpec((1, n_u // LANES, tm, LANES), lambda b, i: (b, 0, i, 0)),
                   pl.BlockSpec((1, n_if, tm), lambda b, i: (b, 0, i))],
        out_shape=[jax.ShapeDtypeStruct((bsz, length, n_qk), BF16),
                   jax.ShapeDtypeStruct((bsz, length, n_v), BF16),
                   jax.ShapeDtypeStruct((bsz, n_u // LANES, length, LANES), F32),
                   jax.ShapeDtypeStruct((bsz, n_if, length), F32)],
        compiler_params=_cparams(2, vmem),
        name="proj",
    )(x, g.reshape(1, d), scale, shift, w_state, w_if_t, b_if.reshape(n_if, 1))


def _conv_body(main_ref, prev_ref, next_ref, w_ref, q_ref, kt_ref, *, width, q_scale):
    i = pl.program_id(1)
    last = pl.num_programs(1) - 1
    t = main_ref.shape[1]
    n = t + 2 * width
    main = main_ref[0].astype(F32)
    prev = jnp.where(i > 0, prev_ref[0].astype(F32), 0.0)
    nxt = jnp.where(i < last, next_ref[0].astype(F32), 0.0)
    ext = jnp.concatenate([prev, main, nxt], axis=0)
    col = lax.broadcasted_iota(jnp.int32, (t, 1), 0) % width
    acc = None
    for dx in (-1, 0, 1):
        shifted = ext if dx == 0 else pltpu.roll(ext, (-dx) % n, axis=0)
        part = None
        for dy in (-1, 0, 1):
            tap = w_ref[(dy + 1) * 3 + (dx + 1):(dy + 1) * 3 + (dx + 1) + 1, :]
            term = tap * shifted[width + dy * width:width + dy * width + t]
            part = term if part is None else part + term
        if dx == -1:
            part = jnp.where(col == 0, 0.0, part)
        elif dx == 1:
            part = jnp.where(col == width - 1, 0.0, part)
        acc = part if acc is None else acc + part
    y = _silu(acc)

    @pl.when(pl.program_id(2) == 0)
    def _():
        q_ref[0] = (y * q_scale).astype(q_ref.dtype)

    @pl.when(pl.program_id(2) == 1)
    def _():
        kt_ref[0] = y.T.astype(kt_ref.dtype)


def _conv(qk_pre, w9, q_scale, width, t_block):
    bsz, length, ch2 = qk_pre.shape
    ch = ch2 // 2
    rpb = t_block // width
    n_rows = length // width
    vmem = 4 * (t_block + 2 * width) * ch * 2 + 14 * (t_block + 2 * width) * ch * 4
    return pl.pallas_call(
        functools.partial(_conv_body, width=width, q_scale=q_scale),
        grid=(bsz, length // t_block, 2),
        in_specs=[pl.BlockSpec((1, t_block, ch), lambda b, i, c: (b, i, c)),
                  pl.BlockSpec((1, width, ch), lambda b, i, c: (b, jnp.maximum(i * rpb - 1, 0), c)),
                  pl.BlockSpec((1, width, ch), lambda b, i, c: (b, jnp.minimum((i + 1) * rpb, n_rows - 1), c)),
                  pl.BlockSpec((9, ch), lambda b, i, c: (0, c))],
        out_specs=[pl.BlockSpec((1, t_block, ch), lambda b, i, c: (b, i, 0)),
                   pl.BlockSpec((1, ch, t_block), lambda b, i, c: (b, 0, i))],
        out_shape=[jax.ShapeDtypeStruct((bsz, length, ch), BF16),
                   jax.ShapeDtypeStruct((bsz, ch, length), BF16)],
        compiler_params=_cparams(3, vmem),
        name="conv",
    )(qk_pre, qk_pre, qk_pre, w9)


def _log_sigmoid(x):
    return jnp.minimum(x, 0.0) - jnp.log1p(jnp.exp(-jnp.abs(x)))


def _cumsum_lanes_exact(x, reverse):
    t = x.shape[-1]
    r_idx = lax.broadcasted_iota(jnp.int32, (t, t), 0)
    c_idx = lax.broadcasted_iota(jnp.int32, (t, t), 1)
    u01 = jnp.where((r_idx >= c_idx) if reverse else (r_idx <= c_idx), 1.0, 0.0).astype(BF16)
    out = None
    for piece in _split3(x):
        p = jnp.dot(piece, u01, preferred_element_type=F32)
        out = p if out is None else out + p
    return out


def _cummax_lanes(x, reverse):
    t = x.shape[-1]
    lane = lax.broadcasted_iota(jnp.int32, x.shape, 1)
    sh = 1
    while sh < t:
        if reverse:
            cand = jnp.where(lane < t - sh, pltpu.roll(x, t - sh, axis=1), -jnp.inf)
        else:
            cand = jnp.where(lane >= sh, pltpu.roll(x, sh, axis=1), -jnp.inf)
        x = jnp.maximum(x, cand)
        sh *= 2
    return x


def _mlstm_gate_rows(g_ref, m_col, d):
    t = g_ref.shape[-1]
    base = d * 2 * N_HEADS
    li = g_ref[0, base:base + N_HEADS, :]
    lf = _log_sigmoid(g_ref[0, base + N_HEADS:base + 2 * N_HEADS, :])
    b = _cumsum_lanes_exact(lf, reverse=(d == 1))
    g = li - b
    a = jnp.maximum(_cummax_lanes(g, reverse=(d == 1)), m_col)
    end = 0 if d == 1 else t - 1
    a_end = a[:, end:end + 1]
    return dict(g=g, a=a, ie=jnp.exp(m_col - a), emt=jnp.exp(-b - a), we=jnp.exp(g - a_end),
                dec=jnp.exp(m_col - a_end), m_new=b[:, end:end + 1] + a_end)


def _mlstm_body(qf_ref, kf_ref, vf_ref, gf_ref, qr_ref, kr_ref, vr_ref, gr_ref, st0_ref, m0_ref,
                hf_ref, hr_ref, st_ref, m_ref):
    @pl.when(pl.program_id(1) == 0)
    def _():
        st_ref[...] = st0_ref[...]
        m_ref[...] = m0_ref[...]

    t = qf_ref.shape[1]
    m_all = m_ref[0]
    rows = [_mlstm_gate_rows(g_ref, m_all[d * N_HEADS:(d + 1) * N_HEADS], d) for d, g_ref in ((0, gf_ref), (1, gr_ref))]
    both = lambda name: jnp.concatenate([rows[0][name], rows[1][name]], axis=0)
    n_hd = 2 * N_HEADS
    pad = jnp.zeros((128 - 3 * n_hd, t), F32)
    cols = jnp.concatenate([both('a'), both('ie'), both('emt'), pad], axis=0).T
    m_ref[0] = both('m_new')
    r_idx = lax.broadcasted_iota(jnp.int32, (t, t), 0)
    c_idx = lax.broadcasted_iota(jnp.int32, (t, t), 1)
    ones_blk = jnp.ones((t, D_V), BF16)
    for d, (q_ref, kt_ref, v_ref, h_ref) in enumerate(((qf_ref, kf_ref, vf_ref, hf_ref), (qr_ref, kr_ref, vr_ref, hr_ref))):
        causal = (c_idx >= r_idx) if d == 1 else (c_idx <= r_idx)
        for h in range(N_HEADS):
            j = d * N_HEADS + h
            q = q_ref[0, :, h * D_QK:(h + 1) * D_QK]
            kt = kt_ref[0, h * D_QK:(h + 1) * D_QK, :]
            v1 = jnp.concatenate([v_ref[0, :, h * D_V:(h + 1) * D_V], ones_blk], axis=1)
            state = st_ref[0, j]
            a_col = cols[:, j:j + 1]
            ie_col = cols[:, n_hd + j:n_hd + j + 1]
            emt_col = cols[:, 2 * n_hd + j:2 * n_hd + j + 1]
            dmat = jnp.exp(jnp.where(causal, rows[d]['g'][h:h + 1, :] - a_col, -jnp.inf))
            s = (jnp.dot(q, kt, preferred_element_type=F32) * dmat).astype(BF16)
            z = jnp.dot(q, state.astype(BF16), preferred_element_type=F32)
            p = jnp.dot(s, v1, preferred_element_type=F32)
            num = ie_col * z[:, :D_V] + p[:, :D_V]
            den = ie_col * z[:, D_V:] + p[:, D_V:]
            h_ref[0, :, h * D_V:(h + 1) * D_V] = (num / jnp.maximum(jnp.abs(den), emt_col)).astype(h_ref.dtype)
            kw = (kt.astype(F32) * rows[d]['we'][h:h + 1, :]).astype(BF16)
            st_ref[0, j] = rows[d]['dec'][h:h + 1, :] * state + jnp.dot(kw, v1, preferred_element_type=F32)


def _mlstm(q, k_t, v, gif_t, st0, m0):
    bsz, length, hv = v.shape
    t = M_CHUNK
    nc = length // t
    hq = N_HEADS * D_QK
    ng = gif_t.shape[1]
    fwd = lambda b, i: (b, i, 0)
    rev = lambda b, i: (b, nc - 1 - i, 0)
    fwd_t = lambda b, i: (b, 0, i)
    rev_t = lambda b, i: (b, 0, nc - 1 - i)
    st_spec = pl.BlockSpec((1,) + st0.shape[1:], lambda b, i: (b, 0, 0, 0))
    m_spec = pl.BlockSpec((1,) + m0.shape[1:], lambda b, i: (b, 0, 0))
    vmem = 24 * 1024 * 1024
    return pl.pallas_call(
        _mlstm_body,
        grid=(bsz, nc),
        in_specs=[pl.BlockSpec((1, t, hq), fwd), pl.BlockSpec((1, hq, t), fwd_t),
                  pl.BlockSpec((1, t, hv), fwd), pl.BlockSpec((1, ng, t), fwd_t),
                  pl.BlockSpec((1, t, hq), rev), pl.BlockSpec((1, hq, t), rev_t),
                  pl.BlockSpec((1, t, hv), rev), pl.BlockSpec((1, ng, t), rev_t),
                  st_spec, m_spec],
        out_specs=[pl.BlockSpec((1, t, hv), fwd), pl.BlockSpec((1, t, hv), rev), st_spec, m_spec],
        out_shape=[jax.ShapeDtypeStruct((bsz, length, hv), BF16),
                   jax.ShapeDtypeStruct((bsz, length, hv), BF16),
                   jax.ShapeDtypeStruct(st0.shape, F32),
                   jax.ShapeDtypeStruct(m0.shape, F32)],
        compiler_params=_cparams(2, vmem),
        name="mlstm",
    )(q, k_t, v, gif_t, q, k_t, v, gif_t, st0, m0)


def _s5_tables(a_re, a_im, log_dt, b_re, b_im, c_re, c_im, d_skip):
    hp = lax.Precision.HIGHEST
    t = S5_CHUNK
    n_dir, groups, p = a_re.shape
    cg = b_re.shape[-1]
    dt = jnp.exp(log_dt)[..., None]

    def lam_pow(n):
        mag = jnp.exp(n * (dt * a_re)[..., None])
        ang = n * (dt * a_im)[..., None]
        return mag * jnp.cos(ang), mag * jnp.sin(ang)

    ab_re, ab_im = (z[..., 0] for z in lam_pow(jnp.ones((1,), F32)))
    den = a_re * a_re + a_im * a_im
    xr = ab_re - 1.0
    cf_re = (xr * a_re + ab_im * a_im) / den
    cf_im = (ab_im * a_re - xr * a_im) / den
    bb_re = cf_re[..., None] * b_re - cf_im[..., None] * b_im
    bb_im = cf_re[..., None] * b_im + cf_im[..., None] * b_re
    jj = jnp.arange(t + 1, dtype=F32)
    lp_re, lp_im = lam_pow(jj)

    def w_dir(d, exps):
        lr = lp_re[d][:, :, exps]
        li = lp_im[d][:, :, exps]
        wr = lr[..., None] * bb_re[d][:, :, None, :] - li[..., None] * bb_im[d][:, :, None, :]
        wi = lr[..., None] * bb_im[d][:, :, None, :] + li[..., None] * bb_re[d][:, :, None, :]
        to_rows = lambda w: jnp.transpose(w, (0, 2, 3, 1)).reshape(groups, t * cg, p)
        return to_rows(wr), to_rows(wi)

    s_idx = jnp.arange(t)
    wf_re, wf_im = w_dir(0, t - 1 - s_idx)
    wr_re, wr_im = w_dir(1, s_idx)
    zw = jnp.zeros_like(wf_re)
    w_in = jnp.concatenate([jnp.concatenate([wf_re, zw, wf_im, zw], axis=-1),
                            jnp.concatenate([zw, wr_re, zw, wr_im], axis=-1)], axis=1)

    def c_dir(d, exps):
        lr = lp_re[d][:, :, exps]
        li = lp_im[d][:, :, exps]
        cr = jnp.transpose(c_re[d], (0, 2, 1))
        ci = jnp.transpose(c_im[d], (0, 2, 1))
        o_re = cr[:, :, None, :] * lr[..., None] - ci[:, :, None, :] * li[..., None]
        o_im = cr[:, :, None, :] * li[..., None] + ci[:, :, None, :] * lr[..., None]
        return o_re.reshape(groups, p, t * cg), (-o_im).reshape(groups, p, t * cg)

    cf_r, cf_i = c_dir(0, s_idx + 1)
    cr_r, cr_i = c_dir(1, t - s_idx)
    c_out = jnp.concatenate([cf_r, cr_r, cf_i, cr_i], axis=1)

    def k_dir(d):
        lr = lp_re[d][:, :, :t]
        li = lp_im[d][:, :, :t]
        clr = c_re[d][:, :, :, None] * lr[:, None] - c_im[d][:, :, :, None] * li[:, None]
        cli = c_re[d][:, :, :, None] * li[:, None] + c_im[d][:, :, :, None] * lr[:, None]
        return (jnp.einsum('gqpj,gpc->gjqc', clr, bb_re[d], precision=hp)
                - jnp.einsum('gqpj,gpc->gjqc', cli, bb_im[d], precision=hp))

    kf = k_dir(0)
    kr = k_dir(1)
    lag = s_idx[None, :] - s_idx[:, None]
    sel = jnp.concatenate([lag[None] == s_idx[:, None, None], -lag[None] == s_idx[:, None, None]], axis=0).astype(F32)
    resp_t = jnp.einsum('jst,gjqc->gtqsc', sel, jnp.concatenate([kf, kr], axis=1), precision=hp)
    skip = (jnp.eye(t, dtype=F32)[None, :, None, :, None] * jnp.eye(cg, dtype=F32)[None, None, :, None, :]
            * d_skip.reshape(groups, 1, cg, 1, 1))
    m_tz_t = (resp_t + skip).reshape(groups, t * cg, t * cg)

    n_pow = 8
    kk = (t * (2 ** jnp.arange(n_pow))).astype(F32)
    mp_re, mp_im = lam_pow(kk)
    mp_re = jnp.transpose(mp_re, (0, 1, 3, 2))
    mp_im = jnp.transpose(mp_im, (0, 1, 3, 2))
    mu_re = jnp.concatenate([mp_re[0], mp_re[1]], axis=-1)
    mu_im = jnp.concatenate([mp_im[0], mp_im[1]], axis=-1)
    tr = lambda m: jnp.transpose(m, (0, 2, 1)).astype(BF16)
    return w_in.astype(BF16), m_tz_t.astype(BF16), tr(c_out), mu_re, mu_im


def _pad_rows(x, rows):
    return x if x.shape[0] == rows else jnp.concatenate([x, jnp.zeros((rows - x.shape[0],) + x.shape[1:], x.dtype)], axis=0)


def _s5_body(*refs, nch, n_steps, with_output):
    if with_output:
        (u_ref, h0_ref, win_ref, mtzt_ref, coutt_ref, mure_ref, muim_ref, wglut_ref, bglu_ref, wbt_ref,
         yb_ref, hout_ref, xt_scr, yt_scr) = refs
    else:
        u_ref, h0_ref, win_ref, mure_ref, muim_ref, hout_ref, xt_scr = refs
    t_c, cg, groups = S5_CHUNK, S5_GROUP, S5_GROUPS
    p2 = 2 * S5_STATE
    ncp = xt_scr.shape[-1]
    for t in range(t_c):
        ut = jnp.concatenate([u_ref[0, j, pl.ds(t, nch, stride=t_c), :] for j in range(u_ref.shape[1])], axis=1)
        ut = _pad_rows(ut, ncp)
        xt_scr[:, t] = ut.T.astype(BF16).reshape(groups, cg, ncp)

    r_idx = lax.broadcasted_iota(jnp.int32, (ncp, ncp), 0)
    c_idx = lax.broadcasted_iota(jnp.int32, (ncp, ncp), 1)
    flip = jnp.where(r_idx + c_idx == nch - 1, 1.0, 0.0).astype(BF16)
    row = lax.broadcasted_iota(jnp.int32, (ncp, p2), 0)
    is_fwd = (lax.broadcasted_iota(jnp.int32, (ncp, 2 * p2), 1) % p2) < S5_STATE

    def group(g, carry):
        x = xt_scr[g].reshape(t_c * cg, ncp)
        x_rev = jnp.dot(x, flip, preferred_element_type=F32).astype(BF16)
        local = _dot_tn(jnp.concatenate([x, x_rev], axis=0), win_ref[g])
        l_re, l_im = local[:, :p2], local[:, p2:]
        h0 = h0_ref[0, pl.ds(g, 1), :]
        z_re = jnp.where(row == 0, h0[:, :p2], pltpu.roll(l_re, 1, axis=0))
        z_im = jnp.where(row == 0, h0[:, p2:], pltpu.roll(l_im, 1, axis=0))
        for k in range(n_steps):
            sft = 1 << k
            a_re = mure_ref[g, k:k + 1, :]
            a_im = muim_ref[g, k:k + 1, :]
            p_re = jnp.where(row >= sft, pltpu.roll(z_re, sft, axis=0), 0.0)
            p_im = jnp.where(row >= sft, pltpu.roll(z_im, sft, axis=0), 0.0)
            z_re, z_im = z_re + a_re * p_re - a_im * p_im, z_im + a_re * p_im + a_im * p_re
        if with_output:
            z = jnp.concatenate([z_re, z_im], axis=1).astype(BF16)
            z_flip = jnp.dot(flip, z, preferred_element_type=F32).astype(BF16)
            z_nat = jnp.where(is_fwd, z, z_flip)
            y = jnp.dot(mtzt_ref[g], x, preferred_element_type=F32) + _dot_nt(coutt_ref[g], z_nat)
            yt_scr[g] = jax.nn.gelu(y).astype(BF16).reshape(t_c, cg, ncp)
        a_re = mure_ref[g, 0:1, :]
        a_im = muim_ref[g, 0:1, :]
        e_re, e_im = z_re[nch - 1:nch], z_im[nch - 1:nch]
        hout_ref[0, pl.ds(g, 1), :] = jnp.concatenate(
            [a_re * e_re - a_im * e_im + l_re[nch - 1:nch], a_re * e_im + a_im * e_re + l_im[nch - 1:nch]], axis=1)
        return carry

    lax.fori_loop(0, groups, group, 0, unroll=4)

    if with_output:
        def tail(t, carry):
            ys = yt_scr[:, t].reshape(groups * cg, ncp)
            glu = jax.nn.sigmoid(jnp.dot(wglut_ref[...], ys, preferred_element_type=F32) + bglu_ref[...])
            ybt = jnp.dot(wbt_ref[...], (ys.astype(F32) * glu).astype(BF16), preferred_element_type=F32)
            yb_ref[0, t] = ybt.T[:nch].astype(yb_ref.dtype)
            return carry

        lax.fori_loop(0, t_c, tail, 0)


def _s5(u, h0, w_in_t, m_tz_t, c_out_t, mu_a, mu_b, w_glu_t=None, b_glu=None, w_b_t=None):
    bsz, n_blk, length, _ = u.shape
    width = n_blk * LANES
    with_output = w_glu_t is not None
    nch = length // S5_CHUNK
    ncp = -(-nch // 128) * 128
    p4 = 4 * S5_STATE
    n_steps = max(1, (nch - 1).bit_length())
    tc = S5_CHUNK * S5_GROUP
    one = pl.Buffered(1)
    full = lambda a: pl.BlockSpec(a.shape, lambda b: (0,) * a.ndim, pipeline_mode=one)
    in_specs = [pl.BlockSpec((1, n_blk, length, LANES), lambda b: (b, 0, 0, 0), pipeline_mode=one),
                pl.BlockSpec((1, S5_GROUPS, p4), lambda b: (b, 0, 0)), full(w_in_t)]
    args = [u, h0, w_in_t]
    scratch = [pltpu.VMEM((S5_GROUPS, S5_CHUNK, S5_GROUP, ncp), BF16)]
    out_specs = [pl.BlockSpec((1, S5_GROUPS, p4), lambda b: (b, 0, 0))]
    out_shape = [jax.ShapeDtypeStruct((bsz, S5_GROUPS, p4), F32)]
    vmem = length * width * 4 + 4 * S5_GROUPS * tc * tc * 2 + 2 * S5_GROUPS * tc * ncp * 2 + 8 * 1024 * 1024
    if with_output:
        d_out = w_b_t.shape[0]
        in_specs += [full(m_tz_t), full(c_out_t), full(mu_a), full(mu_b), full(w_glu_t),
                     pl.BlockSpec((width, 1), lambda b: (0, 0)), full(w_b_t)]
        args += [m_tz_t, c_out_t, mu_a, mu_b, w_glu_t, b_glu.reshape(width, 1), w_b_t]
        scratch.append(pltpu.VMEM((S5_GROUPS, S5_CHUNK, S5_GROUP, ncp), BF16))
        out_specs.insert(0, pl.BlockSpec((1, S5_CHUNK, nch, d_out), lambda b: (b, 0, 0, 0)))
        out_shape.insert(0, jax.ShapeDtypeStruct((bsz, S5_CHUNK, nch, d_out), BF16))
        vmem += 2 * S5_CHUNK * nch * d_out * 2 + (width * width + width * d_out) * 2
    else:
        in_specs += [full(mu_a), full(mu_b)]
        args += [mu_a, mu_b]
    return pl.pallas_call(
        functools.partial(_s5_body, nch=nch, n_steps=n_steps, with_output=with_output),
        grid=(bsz,),
        in_specs=in_specs,
        out_specs=out_specs,
        out_shape=out_shape,
        scratch_shapes=scratch,
        compiler_params=_cparams(1, vmem),
        name="s5",
    )(*args)


def _mixout_body(x_ref, hf_ref, hr_ref, yb_ref, mod_ref, g1_ref, g2_ref, gmh_ref, wog_ref, wa_ref,
                 bgate_ref, wo_ref, wr_ref, br_ref,
                 x1_ref, h2_ref, ti_ref, tp_ref, rk_ref, cnt_ref, base_ref, yb_scr):
    x = x_ref[0]
    d = x.shape[-1]
    mod = mod_ref[0]
    h = _norm_mod(x, g1_ref[...], mod[1:2], mod[0:1]).astype(BF16)
    og = jnp.dot(h, wog_ref[...], preferred_element_type=F32)
    hm = hf_ref[0].astype(F32) + hr_ref[0].astype(F32)
    heads = []
    for hd in range(N_HEADS):
        blk = hm[:, hd * D_V:(hd + 1) * D_V]
        heads.append(blk * lax.rsqrt(jnp.mean(blk * blk, axis=-1, keepdims=True) + EPS))
    hn = jnp.concatenate(heads, axis=1) * gmh_ref[...]
    y_a = jnp.dot((hn * jax.nn.sigmoid(og[:, :d])).astype(BF16), wa_ref[...], preferred_element_type=F32)
    n_pos = yb_ref.shape[1]
    for t in range(n_pos):
        blk = yb_ref[0, t].astype(F32)
        for j in range(yb_scr.shape[0]):
            yb_scr[j, pl.ds(t, yb_ref.shape[2], stride=n_pos), :] = blk[:, j * LANES:(j + 1) * LANES]
    y_b = jnp.concatenate([yb_scr[j] for j in range(yb_scr.shape[0])], axis=1)
    gates = jax.nn.sigmoid(og[:, d:] + bgate_ref[...])
    merged = gates[:, :d] * y_a + gates[:, d:] * y_b
    x1 = x + mod[2:3] * jnp.dot(merged.astype(BF16), wo_ref[...], preferred_element_type=F32)
    x1_ref[0] = x1
    h2f = _norm_mod(x1, g2_ref[...], mod[4:5], mod[3:4])
    h2_ref[0] = h2f
    logits = jnp.dot(h2f.astype(BF16), wr_ref[...], preferred_element_type=F32) + br_ref[...]
    tm, n_e = logits.shape
    e_iota = lax.broadcasted_iota(jnp.int32, (tm, n_e), 1)
    lane = lax.broadcasted_iota(jnp.int32, (tm, ti_ref.shape[-1]), 1)
    ti = jnp.zeros(lane.shape, jnp.int32)
    tv = jnp.zeros(lane.shape, F32)
    top = None
    chosen = []
    for k in range(TOP_K):
        mx = jnp.max(logits, axis=-1, keepdims=True)
        idx = jnp.min(jnp.where(logits == mx, e_iota, n_e), axis=-1, keepdims=True)
        top = mx if top is None else top
        ti = jnp.where(lane == k, idx, ti)
        tv = jnp.where(lane == k, jnp.exp(mx - top), tv)
        chosen.append(e_iota == idx)
        logits = jnp.where(chosen[-1], -jnp.inf, logits)
    ti_ref[0] = ti
    tp_ref[0] = tv / jnp.sum(tv, axis=-1, keepdims=True)

    @pl.when((pl.program_id(0) == 0) & (pl.program_id(1) == 0))
    def _():
        base_ref[...] = jnp.zeros(base_ref.shape, F32)

    onehot = jnp.zeros((tm, n_e), F32)
    for sel in chosen:
        onehot = onehot + jnp.where(sel, 1.0, 0.0)
    below = (lax.broadcasted_iota(jnp.int32, (tm, tm), 1) < lax.broadcasted_iota(jnp.int32, (tm, tm), 0))
    before = jnp.dot(jnp.where(below, 1.0, 0.0).astype(BF16), onehot.astype(BF16),
                     preferred_element_type=F32) + base_ref[...]
    rk = jnp.zeros(lane.shape, jnp.int32)
    for k, sel in enumerate(chosen):
        rank = jnp.sum(jnp.where(sel, before, 0.0), axis=-1, keepdims=True)
        rk = jnp.where(lane == k, rank.astype(jnp.int32), rk)
    rk_ref[0] = rk
    base_ref[...] = base_ref[...] + jnp.sum(onehot, axis=0, keepdims=True)
    cnt_ref[...] = base_ref[...].astype(jnp.int32)


def _mixout(x, h_f, h_r, y_b, mod, g1, g2, g_mh, w_og, w_a, b_gate, w_o, w_r, b_r, tm):
    bsz, length, d = x.shape
    n_pos = y_b.shape[1]
    n_e = w_r.shape[-1]
    tok = lambda b, i: (b, i, 0)
    const2 = lambda b, i: (0, 0)

    def wspec(w):
        return pl.BlockSpec(w.shape, const2, pipeline_mode=pl.Buffered(1))

    weights = (w_og, w_a, w_o, w_r)
    w_bytes = sum(int(w.size) * w.dtype.itemsize for w in weights)
    vmem = w_bytes + 2 * tm * (d * 4 + 3 * d * 2 + d * 4 + d * 4) + 15 * tm * d * 4
    return pl.pallas_call(
        _mixout_body,
        grid=(bsz, length // tm),
        in_specs=[pl.BlockSpec((1, tm, d), tok), pl.BlockSpec((1, tm, d), tok), pl.BlockSpec((1, tm, d), tok),
                  pl.BlockSpec((1, n_pos, tm // n_pos, d), lambda b, i: (b, 0, i, 0)),
                  pl.BlockSpec((1, 6, d), lambda b, i: (b, 0, 0)),
                  pl.BlockSpec((1, d), const2), pl.BlockSpec((1, d), const2), pl.BlockSpec((1, d), const2),
                  wspec(w_og), wspec(w_a),
                  pl.BlockSpec((1, 2 * d), const2), wspec(w_o), wspec(w_r), pl.BlockSpec((1, n_e), const2)],
        out_specs=[pl.BlockSpec((1, tm, d), tok), pl.BlockSpec((1, tm, d), tok),
                   pl.BlockSpec((1, tm, 8), tok), pl.BlockSpec((1, tm, 8), tok), pl.BlockSpec((1, tm, 8), tok),
                   pl.BlockSpec((1, n_e), const2)],
        out_shape=[jax.ShapeDtypeStruct((bsz, length, d), F32),
                   jax.ShapeDtypeStruct((bsz, length, d), F32),
                   jax.ShapeDtypeStruct((bsz, length, 8), jnp.int32),
                   jax.ShapeDtypeStruct((bsz, length, 8), F32),
                   jax.ShapeDtypeStruct((bsz, length, 8), jnp.int32),
                   jax.ShapeDtypeStruct((1, n_e), jnp.int32)],
        scratch_shapes=[pltpu.VMEM((1, n_e), F32), pltpu.VMEM((d // LANES, tm, LANES), F32)],
        compiler_params=_cparams(2, vmem),
        name="mixout",
    )(x, h_f, h_r, y_b, mod, g1.reshape(1, d), g2.reshape(1, d), g_mh.reshape(1, d), w_og, w_a,
      b_gate.reshape(1, 2 * d), w_o, w_r, b_r.reshape(1, n_e))


def _rowpos_body(ti_ref, rk_ref, rs_ref, pos_ref):
    ti = ti_ref[...]
    rk = rk_ref[...]
    n, w = ti.shape
    n_e = rs_ref.shape[-1]
    e_iota = lax.broadcasted_iota(jnp.int32, (n, n_e), 1)
    lane = lax.broadcasted_iota(jnp.int32, (n, w), 1)
    pos = jnp.zeros((n, w), jnp.int32)
    for k in range(TOP_K):
        start = jnp.sum(jnp.where(e_iota == ti[:, k:k + 1], rs_ref[...], 0), axis=-1, keepdims=True)
        pos = jnp.where(lane == k, start + rk[:, k:k + 1], pos)
    pos_ref[...] = pos


def _rowpos(top_i, rank, row_start, tn):
    n, w = top_i.shape
    n_e = row_start.shape[-1]
    return pl.pallas_call(
        _rowpos_body,
        grid=(n // tn,),
        in_specs=[pl.BlockSpec((tn, w), lambda i: (i, 0)), pl.BlockSpec((tn, w), lambda i: (i, 0)),
                  pl.BlockSpec((1, n_e), lambda i: (0, 0))],
        out_specs=pl.BlockSpec((tn, w), lambda i: (i, 0)),
        out_shape=jax.ShapeDtypeStruct((n, w), jnp.int32),
        compiler_params=_cparams(1, 16 * tn * 128 * 4),
        name="rowpos",
    )(top_i, rank, row_start)


def _dispatch_body(pos_ref, h_ref, xs_hbm, sem):
    tm = h_ref.shape[0]

    def issue(i, carry):
        base = pl.multiple_of(i * SUBLANES, SUBLANES)
        for s in range(SUBLANES):
            for k in range(TOP_K):
                row = pos_ref[(base + s) * TOP_K + k]
                pltpu.make_async_copy(h_ref.at[pl.ds(base + s, 1), :], xs_hbm.at[row],
                                      sem).start(priority=k % 2)
        return carry

    lax.fori_loop(0, tm // SUBLANES, issue, 0)
    for k in range(TOP_K):
        pltpu.make_async_copy(h_ref, h_ref, sem).wait()


def _dispatch(pos_flat, h_rows, n_rows, tm):
    n, w = h_rows.shape
    return pl.pallas_call(
        _dispatch_body,
        grid=(n // tm,),
        in_specs=[pl.BlockSpec((tm * TOP_K,), lambda i: (i,), memory_space=pltpu.SMEM),
                  pl.BlockSpec((tm, w), lambda i: (i, 0))],
        out_specs=pl.BlockSpec(memory_space=pl.ANY),
        out_shape=jax.ShapeDtypeStruct((n_rows, 1, w), h_rows.dtype),
        scratch_shapes=[pltpu.SemaphoreType.DMA(())],
        compiler_params=_cparams(1, 8 * tm * w * 4),
        name="dispatch",
    )(pos_flat, h_rows)


def _experts_body(te_ref, tv_ref, nt_ref, xs_ref, win_ref, bin_ref, wout_ref, bout_ref, y_ref, win_bf, wout_bf,
                  *, f_chunk, cast_rows):
    i = pl.program_id(0)
    e = te_ref[i]
    e_prev = te_ref[jnp.maximum(i - 1, 0)]
    d, f2 = win_bf.shape
    f = f2 // 2

    @pl.when((i == 0) | (e != e_prev))
    def _():
        def cast_in(r, carry):
            rows = pl.ds(pl.multiple_of(r * cast_rows, cast_rows), cast_rows)
            win_bf[rows, :] = win_ref[0, rows, :].astype(BF16)
            return carry

        def cast_out(r, carry):
            rows = pl.ds(pl.multiple_of(r * cast_rows, cast_rows), cast_rows)
            wout_bf[rows, :] = wout_ref[0, rows, :].astype(BF16)
            return carry

        lax.fori_loop(0, d // cast_rows, cast_in, 0)
        lax.fori_loop(0, f // cast_rows, cast_out, 0)

    @pl.when(i < nt_ref[0])
    def _():
        xs = xs_ref[:, 0, :]
        live = lax.broadcasted_iota(jnp.int32, xs.shape, 0) < tv_ref[i]
        x = jnp.where(live, xs, 0.0).astype(BF16)
        acc = None
        for c in range(f // f_chunk):
            lo = c * f_chunk
            zg = jnp.dot(x, win_bf[:, lo:lo + f_chunk], preferred_element_type=F32) + bin_ref[0, :, lo:lo + f_chunk]
            zl = (jnp.dot(x, win_bf[:, f + lo:f + lo + f_chunk], preferred_element_type=F32)
                  + bin_ref[0, :, f + lo:f + lo + f_chunk])
            glu = jnp.minimum(zg, SWIGLU_LIMIT)
            lin = jnp.clip(zl, -SWIGLU_LIMIT, SWIGLU_LIMIT)
            act = glu * jax.nn.sigmoid(SWIGLU_ALPHA * glu) * (lin + 1.0)
            part = jnp.dot(act.astype(BF16), wout_bf[lo:lo + f_chunk, :], preferred_element_type=F32)
            acc = part if acc is None else acc + part
        y_ref[:, 0, :] = acc + bout_ref[0]

    @pl.when(i >= nt_ref[0])
    def _():
        y_ref[...] = jnp.zeros(y_ref.shape, y_ref.dtype)


def _experts(tile_expert, tile_valid, n_tiles, xs, w_e_in, b_e_in, w_e_out, b_e_out, tm):
    rows, _, d = xs.shape
    n_e, _, f2 = w_e_in.shape
    f = f2 // 2
    nt_max = rows // tm
    row_map = lambda i, te, tv, nt: (jnp.minimum(i, nt[0] - 1), 0, 0)
    exp_map = lambda i, te, tv, nt: (te[i], 0, 0)
    vmem = 2 * (d * f2 + f * d) * 4 + (d * f2 + f * d) * 2 + 4 * tm * d * 2 + 10 * tm * d * 4
    grid_spec = pltpu.PrefetchScalarGridSpec(
        num_scalar_prefetch=3,
        grid=(nt_max,),
        in_specs=[pl.BlockSpec((tm, 1, d), row_map),
                  pl.BlockSpec((1, d, f2), exp_map), pl.BlockSpec((1, 1, f2), exp_map),
                  pl.BlockSpec((1, f, d), exp_map), pl.BlockSpec((1, 1, d), exp_map)],
        out_specs=pl.BlockSpec((tm, 1, d), lambda i, te, tv, nt: (i, 0, 0)),
        scratch_shapes=[pltpu.VMEM((d, f2), BF16), pltpu.VMEM((f, d), BF16)],
    )
    return pl.pallas_call(
        functools.partial(_experts_body, f_chunk=512, cast_rows=128),
        grid_spec=grid_spec,
        out_shape=jax.ShapeDtypeStruct((rows, 1, d), F32),
        compiler_params=_cparams(1, vmem),
        name="experts",
    )(tile_expert, tile_valid, n_tiles, xs, w_e_in, b_e_in.reshape(n_e, 1, f2), w_e_out, b_e_out.reshape(n_e, 1, d))


def _final_body(pos_ref, posn_ref, x1_ref, tp_ref, gt_ref, g_ref, y_hbm, o_ref, buf, sem):
    i = pl.program_id(0)
    n = pl.num_programs(0)
    tm = x1_ref.shape[0]

    def gather(p_ref, slot):
        def issue(i, carry):
            base = pl.multiple_of(i * SUBLANES, SUBLANES)
            for s in range(SUBLANES):
                for k in range(TOP_K):
                    row = p_ref[(base + s) * TOP_K + k]
                    pltpu.make_async_copy(y_hbm.at[row], buf.at[slot, k, pl.ds(base + s, 1), :],
                                          sem.at[slot]).start(priority=k % 2)
            return carry

        lax.fori_loop(0, tm // SUBLANES, issue, 0)

    @pl.when(i == 0)
    def _():
        gather(pos_ref, 0)

    @pl.when(i + 1 < n)
    def _():
        gather(posn_ref, (i + 1) % 2)

    slot = i % 2
    for k in range(TOP_K):
        pltpu.make_async_copy(buf.at[slot, k], buf.at[slot, k], sem.at[slot]).wait()
    moe = None
    for k in range(TOP_K):
        term = tp_ref[:, k:k + 1] * buf[slot, k]
        moe = term if moe is None else moe + term
    x2 = x1_ref[...] + gt_ref[0] * moe
    o_ref[...] = x2 * lax.rsqrt(jnp.mean(x2 * x2, axis=-1, keepdims=True) + EPS) * g_ref[...]


def _final(pos_flat, x1, top_p, gt2, g_final, y_rows, tm):
    n_tok, d = x1.shape
    bsz = gt2.shape[0]
    per_b = n_tok // bsz // tm
    nxt = lambda i: (jnp.minimum(i + 1, n_tok // tm - 1),)
    return pl.pallas_call(
        _final_body,
        grid=(n_tok // tm,),
        in_specs=[pl.BlockSpec((tm * TOP_K,), lambda i: (i,), memory_space=pltpu.SMEM),
                  pl.BlockSpec((tm * TOP_K,), nxt, memory_space=pltpu.SMEM),
                  pl.BlockSpec((tm, d), lambda i: (i, 0)),
                  pl.BlockSpec((tm, top_p.shape[-1]), lambda i: (i, 0)),
                  pl.BlockSpec((1, 1, d), lambda i: (i // per_b, 0, 0)),
                  pl.BlockSpec((1, d), lambda i: (0, 0)),
                  pl.BlockSpec(memory_space=pl.ANY)],
        out_specs=pl.BlockSpec((tm, d), lambda i: (i, 0)),
        out_shape=jax.ShapeDtypeStruct((n_tok, d), F32),
        scratch_shapes=[pltpu.VMEM((2, TOP_K, tm, d), F32), pltpu.SemaphoreType.DMA((2,))],
        compiler_params=_cparams(1, 2 * TOP_K * tm * d * 4 + 4 * tm * d * 4 + 8 * tm * d * 4),
        name="final",
    )(pos_flat, pos_flat, x1, top_p, gt2, g_final.reshape(1, d), y_rows)


def _tile_table(counts, tm, nt_max):
    tiles = (counts + tm - 1) // tm
    tile_end = jnp.cumsum(tiles)
    tile_start = tile_end - tiles
    n_tiles = tile_end[-1]
    tile_ids = jnp.arange(nt_max, dtype=jnp.int32)
    last_e = jnp.sum((n_tiles - 1) >= tile_end).astype(jnp.int32)
    te = jnp.sum(tile_ids[:, None] >= tile_end[None, :], axis=1).astype(jnp.int32)
    te = jnp.where(tile_ids < n_tiles, te, last_e)
    sel = te[:, None] == jnp.arange(counts.shape[0], dtype=jnp.int32)[None, :]
    cnt_t = jnp.sum(jnp.where(sel, counts[None, :], 0), axis=1)
    start_t = jnp.sum(jnp.where(sel, tile_start[None, :], 0), axis=1)
    live = jnp.clip(cnt_t - (tile_ids - start_t) * tm, 0, tm).astype(jnp.int32)
    live = jnp.where(tile_ids < n_tiles, live, 0)
    return (tile_start * tm).astype(jnp.int32), te, live, n_tiles.reshape(1).astype(jnp.int32)


def kernel(x, c, ctx, c_ctx, w_ada, b_ada, g_norm1, g_norm2, w_in, w_conv_qk, b_ifgate, g_mh, w_branch_m,
           s5_a_re, s5_a_im, s5_log_dt, s5_b_re, s5_b_im, s5_c_re, s5_c_im, s5_d, w_glu, b_glu, w_branch_s,
           b_merge_gate, w_o, w_router, b_router, w_e_in, b_e_in, w_e_out, b_e_out, g_final):
    bsz, length, d = x.shape
    l_ctx = ctx.shape[1]
    n_qk = 2 * N_HEADS * D_QK
    n_v = N_HEADS * D_V
    n_if = 4 * N_HEADS
    n_u = S5_GROUPS * S5_GROUP
    off_if = n_qk + n_v
    off_u = off_if + n_if
    off_o = off_u + n_u
    layer = 0

    pad_rows = -(bsz + 1) % 8
    c_rows = jnp.concatenate([c, c_ctx[None, :], jnp.zeros((pad_rows, d), F32)], axis=0)
    mod_all = _ada(c_rows, w_ada[layer], b_ada[layer])
    mod = mod_all[:bsz].reshape(bsz, 6, d)
    mod_c = mod_all[bsz, :2 * d].reshape(2, 1, 1, d)

    w_l = w_in[layer]
    w_state = jnp.concatenate([w_l[:, :off_if], w_l[:, off_u:off_o]], axis=1).astype(BF16)
    w_if_t = w_l[:, off_if:off_u].T.astype(BF16)
    b_if = b_ifgate[layer].reshape(n_if)
    proj_c = _proj(ctx, g_norm1[layer], mod_c[1], mod_c[0], w_state, w_if_t, b_if, n_qk, n_v, n_u, l_ctx)
    proj_l = _proj(x, g_norm1[layer], mod[:, 1:2], mod[:, 0:1], w_state, w_if_t, b_if, n_qk, n_v, n_u, 512)
    qk_c, v_c, u_c, gif_c = proj_c
    qk_l, v_l, u_l, gif_l = proj_l

    w9 = w_conv_qk[layer].reshape(9, n_qk)
    q_c, kt_c = _conv(qk_c, w9, D_QK ** -0.5, l_ctx, l_ctx)
    q_l, kt_l = _conv(qk_l, w9, D_QK ** -0.5, GRID_W, 512)

    st0 = jnp.zeros((bsz, 2 * N_HEADS, D_QK, 2 * D_V), F32)
    m0 = jnp.zeros((bsz, 2 * N_HEADS, 1), F32)
    _, _, st_c, m_c = _mlstm(q_c, kt_c, v_c, gif_c, st0, m0)
    h_f, h_r, _, _ = _mlstm(q_l, kt_l, v_l, gif_l, st_c, m_c)

    w_in_t, m_tz_t, c_out_t, mu_a, mu_b = _s5_tables(
        s5_a_re[layer], s5_a_im[layer], s5_log_dt[layer], s5_b_re[layer], s5_b_im[layer],
        s5_c_re[layer], s5_c_im[layer], s5_d[layer])
    hs0 = jnp.zeros((bsz, S5_GROUPS, 4 * S5_STATE), F32)
    hs_c, = _s5(u_c, hs0, w_in_t, m_tz_t, c_out_t, mu_a, mu_b)
    y_b, _ = _s5(u_l, hs_c, w_in_t, m_tz_t, c_out_t, mu_a, mu_b,
                 w_glu[layer].T.astype(BF16), b_glu[layer], w_branch_s[layer].T.astype(BF16))

    w_og = w_l[:, off_o:].astype(BF16)
    x1, h2, top_i, top_p, rank, counts = _mixout(
        x, h_f, h_r, y_b, mod, g_norm1[layer], g_norm2[layer], g_mh[layer], w_og, w_branch_m[layer].astype(BF16),
        b_merge_gate[layer], w_o[layer].astype(BF16), w_router[layer].astype(BF16), b_router[layer], 512)

    tm_e = 512
    n_tok = bsz * length
    nt_max = n_tok * TOP_K // tm_e + N_EXPERTS
    row_start, tile_expert, tile_live, n_tiles = _tile_table(counts.reshape(N_EXPERTS), tm_e, nt_max)
    pos = _rowpos(top_i.reshape(n_tok, 8), rank.reshape(n_tok, 8), row_start.reshape(1, N_EXPERTS), min(4096, n_tok))
    pos_flat = pos[:, :TOP_K].reshape(n_tok * TOP_K)
    xs = _dispatch(pos_flat, h2.reshape(n_tok, d), nt_max * tm_e, 512)
    y_rows = _experts(tile_expert, tile_live, n_tiles, xs, w_e_in[layer], b_e_in[layer], w_e_out[layer],
                      b_e_out[layer], tm_e)
    out = _final(pos_flat, x1.reshape(n_tok, d), top_p.reshape(n_tok, 8), mod[:, 5:6], g_final, y_rows, 256)
    return out.reshape(bsz, length, d)
```

```python
import functools
import math

import jax
import jax.numpy as jnp
from jax import lax
from jax.experimental import pallas as pl
from jax.experimental.pallas import tpu as pltpu

F32 = jnp.float32
BF16 = jnp.bfloat16
EPS = 1e-6

N_HEADS = 8
D_QK = 64
D_V = 128
M_CHUNK = 256
GRID_W = 64
S5_GROUPS = 32
S5_GROUP = 16
S5_STATE = 64
S5_CHUNK = 16
N_EXPERTS = 32
TOP_K = 4
SWIGLU_LIMIT = 7.0
SWIGLU_ALPHA = 1.702

LANES = 128
SUBLANES = 8
V7X_VMEM_BYTES = 64 * 1024 * 1024
_VMEM_CAP = V7X_VMEM_BYTES - 8 * 1024 * 1024


def _cparams(n_axes, vmem_bytes):
    limit = int(min(_VMEM_CAP, max(32 * 1024 * 1024, vmem_bytes)))
    return pltpu.CompilerParams(dimension_semantics=("arbitrary",) * n_axes, vmem_limit_bytes=limit)


def _silu(x):
    return x * jax.nn.sigmoid(x)


def _norm_mod(x, g, scale, shift):
    ms = jnp.mean(x * x, axis=-1, keepdims=True)
    return (x * lax.rsqrt(ms + EPS) * g) * (1.0 + scale) + shift


def _split3(x):
    hi = x.astype(BF16)
    r1 = x - hi.astype(F32)
    mid = r1.astype(BF16)
    lo = (r1 - mid.astype(F32)).astype(BF16)
    return hi, mid, lo


def _dot_nt(a, b):
    return lax.dot_general(a, b, (((1,), (1,)), ((), ())), preferred_element_type=F32)


def _dot_tn(a, b):
    return lax.dot_general(a, b, (((0,), (0,)), ((), ())), preferred_element_type=F32)


def _ada_body(c_ref, w_ref, b_ref, o_ref):
    s = _silu(c_ref[...])
    o_ref[...] = jnp.dot(s, w_ref[...], preferred_element_type=F32,
                         precision=lax.Precision.HIGHEST) + b_ref[...]


def _ada(c_rows, w_ada, b_ada):
    rows, d = c_rows.shape
    n = w_ada.shape[1]
    tn = 1024
    return pl.pallas_call(
        _ada_body,
        grid=(n // tn,),
        in_specs=[pl.BlockSpec((rows, d), lambda j: (0, 0)),
                  pl.BlockSpec((d, tn), lambda j: (0, j)),
                  pl.BlockSpec((1, tn), lambda j: (0, j))],
        out_specs=pl.BlockSpec((rows, tn), lambda j: (0, j)),
        out_shape=jax.ShapeDtypeStruct((rows, n), F32),
        compiler_params=_cparams(1, 4 * d * tn * 4),
        name="ada",
    )(c_rows, w_ada, b_ada.reshape(1, n))


def _proj_body(x_ref, g_ref, sc_ref, sh_ref, w_ref, wif_ref, bif_ref, qk_ref, v_ref, u_ref, gif_ref):
    h = _norm_mod(x_ref[0], g_ref[...], sc_ref[0], sh_ref[0]).astype(BF16)
    n_qk = qk_ref.shape[-1]
    n_v = v_ref.shape[-1]
    qk_ref[0] = jnp.dot(h, w_ref[:, :n_qk], preferred_element_type=F32).astype(BF16)
    v_ref[0] = jnp.dot(h, w_ref[:, n_qk:n_qk + n_v], preferred_element_type=F32).astype(BF16)
    u = jnp.dot(h, w_ref[:, n_qk + n_v:], preferred_element_type=F32)
    for j in range(u_ref.shape[1]):
        u_ref[0, j] = u[:, j * LANES:(j + 1) * LANES]
    gif_ref[0] = _dot_nt(wif_ref[...], h) + bif_ref[...]


def _proj(x, g, scale, shift, w_state, w_if_t, b_if, n_qk, n_v, n_u, tm):
    bsz, length, d = x.shape
    n_if = b_if.shape[-1]
    per_batch = scale.shape[0] == bsz
    mod_map = (lambda b, i: (b, 0, 0)) if per_batch else (lambda b, i: (0, 0, 0))
    cols = w_state.shape[1]
    vmem = 2 * (tm * d * 4 + d * cols * 2 + tm * (n_qk + n_v + n_u) * 2 + tm * 128 * 4) + 6 * tm * d * 4
    return pl.pallas_call(
        _proj_body,
        grid=(bsz, length // tm),
        in_specs=[pl.BlockSpec((1, tm, d), lambda b, i: (b, i, 0)),
                  pl.BlockSpec((1, d), lambda b, i: (0, 0)),
                  pl.BlockSpec((1, 1, d), mod_map),
                  pl.BlockSpec((1, 1, d), mod_map),
                  pl.BlockSpec((d, cols), lambda b, i: (0, 0)),
                  pl.BlockSpec((n_if, d), lambda b, i: (0, 0)),
                  pl.BlockSpec((n_if, 1), lambda b, i: (0, 0))],
        out_specs=[pl.BlockSpec((1, tm, n_qk), lambda b, i: (b, i, 0)),
                   pl.BlockSpec((1, tm, n_v), lambda b, i: (b, i, 0)),
                   pl.BlockSpec((1, n_u // LANES, tm, LANES), lambda b, i: (b, 0, i, 0)),
                   pl.BlockSpec((1, n_if, tm), lambda b, i: (b, 0, i))],
        out_shape=[jax.ShapeDtypeStruct((bsz, length, n_qk), BF16),
                   jax.ShapeDtypeStruct((bsz, length, n_v), BF16),
                   jax.ShapeDtypeStruct((bsz, n_u // LANES, length, LANES), F32),
                   jax.ShapeDtypeStruct((bsz, n_if, length), F32)],
        compiler_params=_cparams(2, vmem),
        name="proj",
    )(x, g.reshape(1, d), scale, shift, w_state, w_if_t, b_if.reshape(n_if, 1))


def _conv_body(main_ref, prev_ref, next_ref, w_ref, q_ref, kt_ref, *, width, q_scale):
    i = pl.program_id(1)
    last = pl.num_programs(1) - 1
    t = main_ref.shape[1]
    n = t + 2 * width
    main = main_ref[0].astype(F32)
    prev = jnp.where(i > 0, prev_ref[0].astype(F32), 0.0)
    nxt = jnp.where(i < last, next_ref[0].astype(F32), 0.0)
    ext = jnp.concatenate([prev, main, nxt], axis=0)
    col = lax.broadcasted_iota(jnp.int32, (t, 1), 0) % width
    acc = None
    for dx in (-1, 0, 1):
        shifted = ext if dx == 0 else pltpu.roll(ext, (-dx) % n, axis=0)
        part = None
        for dy in (-1, 0, 1):
            tap = w_ref[(dy + 1) * 3 + (dx + 1):(dy + 1) * 3 + (dx + 1) + 1, :]
            term = tap * shifted[width + dy * width:width + dy * width + t]
            part = term if part is None else part + term
        if dx == -1:
            part = jnp.where(col == 0, 0.0, part)
        elif dx == 1:
            part = jnp.where(col == width - 1, 0.0, part)
        acc = part if acc is None else acc + part
    y = _silu(acc)

    @pl.when(pl.program_id(2) == 0)
    def _():
        q_ref[0] = (y * q_scale).astype(q_ref.dtype)

    @pl.when(pl.program_id(2) == 1)
    def _():
        kt_ref[0] = y.T.astype(kt_ref.dtype)


def _conv(qk_pre, w9, q_scale, width, t_block):
    bsz, length, ch2 = qk_pre.shape
    ch = ch2 // 2
    rpb = t_block // width
    n_rows = length // width
    vmem = 4 * (t_block + 2 * width) * ch * 2 + 14 * (t_block + 2 * width) * ch * 4
    return pl.pallas_call(
        functools.partial(_conv_body, width=width, q_scale=q_scale),
        grid=(bsz, length // t_block, 2),
        in_specs=[pl.BlockSpec((1, t_block, ch), lambda b, i, c: (b, i, c)),
                  pl.BlockSpec((1, width, ch), lambda b, i, c: (b, jnp.maximum(i * rpb - 1, 0), c)),
                  pl.BlockSpec((1, width, ch), lambda b, i, c: (b, jnp.minimum((i + 1) * rpb, n_rows - 1), c)),
                  pl.BlockSpec((9, ch), lambda b, i, c: (0, c))],
        out_specs=[pl.BlockSpec((1, t_block, ch), lambda b, i, c: (b, i, 0)),
                   pl.BlockSpec((1, ch, t_block), lambda b, i, c: (b, 0, i))],
        out_shape=[jax.ShapeDtypeStruct((bsz, length, ch), BF16),
                   jax.ShapeDtypeStruct((bsz, ch, length), BF16)],
        compiler_params=_cparams(3, vmem),
        name="conv",
    )(qk_pre, qk_pre, qk_pre, w9)


def _log_sigmoid(x):
    return jnp.minimum(x, 0.0) - jnp.log1p(jnp.exp(-jnp.abs(x)))


def _cumsum_lanes_exact(x, reverse):
    t = x.shape[-1]
    r_idx = lax.broadcasted_iota(jnp.int32, (t, t), 0)
    c_idx = lax.broadcasted_iota(jnp.int32, (t, t), 1)
    u01 = jnp.where((r_idx >= c_idx) if reverse else (r_idx <= c_idx), 1.0, 0.0).astype(BF16)
    out = None
    for piece in _split3(x):
        p = jnp.dot(piece, u01, preferred_element_type=F32)
        out = p if out is None else out + p
    return out


def _cummax_lanes(x, reverse):
    t = x.shape[-1]
    lane = lax.broadcasted_iota(jnp.int32, x.shape, 1)
    sh = 1
    while sh < t:
        if reverse:
            cand = jnp.where(lane < t - sh, pltpu.roll(x, t - sh, axis=1), -jnp.inf)
        else:
            cand = jnp.where(lane >= sh, pltpu.roll(x, sh, axis=1), -jnp.inf)
        x = jnp.maximum(x, cand)
        sh *= 2
    return x


def _mlstm_gate_rows(g_ref, m_col, d):
    t = g_ref.shape[-1]
    base = d * 2 * N_HEADS
    li = g_ref[0, base:base + N_HEADS, :]
    lf = _log_sigmoid(g_ref[0, base + N_HEADS:base + 2 * N_HEADS, :])
    b = _cumsum_lanes_exact(lf, reverse=(d == 1))
    g = li - b
    a = jnp.maximum(_cummax_lanes(g, reverse=(d == 1)), m_col)
    end = 0 if d == 1 else t - 1
    a_end = a[:, end:end + 1]
    return dict(g=g, a=a, ie=jnp.exp(m_col - a), emt=jnp.exp(-b - a), we=jnp.exp(g - a_end),
                dec=jnp.exp(m_col - a_end), m_new=b[:, end:end + 1] + a_end)


def _mlstm_body(qf_ref, kf_ref, vf_ref, gf_ref, qr_ref, kr_ref, vr_ref, gr_ref, st0_ref, m0_ref,
                hf_ref, hr_ref, st_ref, m_ref):
    @pl.when(pl.program_id(1) == 0)
    def _():
        st_ref[...] = st0_ref[...]
        m_ref[...] = m0_ref[...]

    t = qf_ref.shape[1]
    m_all = m_ref[0]
    rows = [_mlstm_gate_rows(g_ref, m_all[d * N_HEADS:(d + 1) * N_HEADS], d) for d, g_ref in ((0, gf_ref), (1, gr_ref))]
    both = lambda name: jnp.concatenate([rows[0][name], rows[1][name]], axis=0)
    n_hd = 2 * N_HEADS
    pad = jnp.zeros((128 - 3 * n_hd, t), F32)
    cols = jnp.concatenate([both('a'), both('ie'), both('emt'), pad], axis=0).T
    m_ref[0] = both('m_new')
    r_idx = lax.broadcasted_iota(jnp.int32, (t, t), 0)
    c_idx = lax.broadcasted_iota(jnp.int32, (t, t), 1)
    ones_blk = jnp.ones((t, D_V), BF16)
    for d, (q_ref, kt_ref, v_ref, h_ref) in enumerate(((qf_ref, kf_ref, vf_ref, hf_ref), (qr_ref, kr_ref, vr_ref, hr_ref))):
        causal = (c_idx >= r_idx) if d == 1 else (c_idx <= r_idx)
        for h in range(N_HEADS):
            j = d * N_HEADS + h
            q = q_ref[0, :, h * D_QK:(h + 1) * D_QK]
            kt = kt_ref[0, h * D_QK:(h + 1) * D_QK, :]
            v1 = jnp.concatenate([v_ref[0, :, h * D_V:(h + 1) * D_V], ones_blk], axis=1)
            state = st_ref[0, j]
            a_col = cols[:, j:j + 1]
            ie_col = cols[:, n_hd + j:n_hd + j + 1]
            emt_col = cols[:, 2 * n_hd + j:2 * n_hd + j + 1]
            dmat = jnp.exp(jnp.where(causal, rows[d]['g'][h:h + 1, :] - a_col, -jnp.inf))
            s = (jnp.dot(q, kt, preferred_element_type=F32) * dmat).astype(BF16)
            z = jnp.dot(q, state.astype(BF16), preferred_element_type=F32)
            p = jnp.dot(s, v1, preferred_element_type=F32)
            num = ie_col * z[:, :D_V] + p[:, :D_V]
            den = ie_col * z[:, D_V:] + p[:, D_V:]
            h_ref[0, :, h * D_V:(h + 1) * D_V] = (num / jnp.maximum(jnp.abs(den), emt_col)).astype(h_ref.dtype)
            kw = (kt.astype(F32) * rows[d]['we'][h:h + 1, :]).astype(BF16)
            st_ref[0, j] = rows[d]['dec'][h:h + 1, :] * state + jnp.dot(kw, v1, preferred_element_type=F32)


def _mlstm(q, k_t, v, gif_t, st0, m0):
    bsz, length, hv = v.shape
    t = M_CHUNK
    nc = length // t
    hq = N_HEADS * D_QK
    ng = gif_t.shape[1]
    fwd = lambda b, i: (b, i, 0)
    rev = lambda b, i: (b, nc - 1 - i, 0)
    fwd_t = lambda b, i: (b, 0, i)
    rev_t = lambda b, i: (b, 0, nc - 1 - i)
    st_spec = pl.BlockSpec((1,) + st0.shape[1:], lambda b, i: (b, 0, 0, 0))
    m_spec = pl.BlockSpec((1,) + m0.shape[1:], lambda b, i: (b, 0, 0))
    vmem = 24 * 1024 * 1024
    return pl.pallas_call(
        _mlstm_body,
        grid=(bsz, nc),
        in_specs=[pl.BlockSpec((1, t, hq), fwd), pl.BlockSpec((1, hq, t), fwd_t),
                  pl.BlockSpec((1, t, hv), fwd), pl.BlockSpec((1, ng, t), fwd_t),
                  pl.BlockSpec((1, t, hq), rev), pl.BlockSpec((1, hq, t), rev_t),
                  pl.BlockSpec((1, t, hv), rev), pl.BlockSpec((1, ng, t), rev_t),
                  st_spec, m_spec],
        out_specs=[pl.BlockSpec((1, t, hv), fwd), pl.BlockSpec((1, t, hv), rev), st_spec, m_spec],
        out_shape=[jax.ShapeDtypeStruct((bsz, length, hv), BF16),
                   jax.ShapeDtypeStruct((bsz, length, hv), BF16),
                   jax.ShapeDtypeStruct(st0.shape, F32),
                   jax.ShapeDtypeStruct(m0.shape, F32)],
        compiler_params=_cparams(2, vmem),
        name="mlstm",
    )(q, k_t, v, gif_t, q, k_t, v, gif_t, st0, m0)


def _s5_tables(a_re, a_im, log_dt, b_re, b_im, c_re, c_im, d_skip):
    hp = lax.Precision.HIGHEST
    t = S5_CHUNK
    n_dir, groups, p = a_re.shape
    cg = b_re.shape[-1]
    dt = jnp.exp(log_dt)[..., None]

    def lam_pow(n):
        mag = jnp.exp(n * (dt * a_re)[..., None])
        ang = n * (dt * a_im)[..., None]
        return mag * jnp.cos(ang), mag * jnp.sin(ang)

    ab_re, ab_im = (z[..., 0] for z in lam_pow(jnp.ones((1,), F32)))
    den = a_re * a_re + a_im * a_im
    xr = ab_re - 1.0
    cf_re = (xr * a_re + ab_im * a_im) / den
    cf_im = (ab_im * a_re - xr * a_im) / den
    bb_re = cf_re[..., None] * b_re - cf_im[..., None] * b_im
    bb_im = cf_re[..., None] * b_im + cf_im[..., None] * b_re
    jj = jnp.arange(t + 1, dtype=F32)
    lp_re, lp_im = lam_pow(jj)

    def w_dir(d, exps):
        lr = lp_re[d][:, :, exps]
        li = lp_im[d][:, :, exps]
        wr = lr[..., None] * bb_re[d][:, :, None, :] - li[..., None] * bb_im[d][:, :, None, :]
        wi = lr[..., None] * bb_im[d][:, :, None, :] + li[..., None] * bb_re[d][:, :, None, :]
        to_rows = lambda w: jnp.transpose(w, (0, 2, 3, 1)).reshape(groups, t * cg, p)
        return to_rows(wr), to_rows(wi)

    s_idx = jnp.arange(t)
    wf_re, wf_im = w_dir(0, t - 1 - s_idx)
    wr_re, wr_im = w_dir(1, s_idx)
    zw = jnp.zeros_like(wf_re)
    w_in = jnp.concatenate([jnp.concatenate([wf_re, zw, wf_im, zw], axis=-1),
                            jnp.concatenate([zw, wr_re, zw, wr_im], axis=-1)], axis=1)

    def c_dir(d, exps):
        lr = lp_re[d][:, :, exps]
        li = lp_im[d][:, :, exps]
        cr = jnp.transpose(c_re[d], (0, 2, 1))
        ci = jnp.transpose(c_im[d], (0, 2, 1))
        o_re = cr[:, :, None, :] * lr[..., None] - ci[:, :, None, :] * li[..., None]
        o_im = cr[:, :, None, :] * li[..., None] + ci[:, :, None, :] * lr[..., None]
        return o_re.reshape(groups, p, t * cg), (-o_im).reshape(groups, p, t * cg)

    cf_r, cf_i = c_dir(0, s_idx + 1)
    cr_r, cr_i = c_dir(1, t - s_idx)
    c_out = jnp.concatenate([cf_r, cr_r, cf_i, cr_i], axis=1)

    def k_dir(d):
        lr = lp_re[d][:, :, :t]
        li = lp_im[d][:, :, :t]
        clr = c_re[d][:, :, :, None] * lr[:, None] - c_im[d][:, :, :, None] * li[:, None]
        cli = c_re[d][:, :, :, None] * li[:, None] + c_im[d][:, :, :, None] * lr[:, None]
        return (jnp.einsum('gqpj,gpc->gjqc', clr, bb_re[d], precision=hp)
                - jnp.einsum('gqpj,gpc->gjqc', cli, bb_im[d], precision=hp))

    kf = k_dir(0)
    kr = k_dir(1)
    lag = s_idx[None, :] - s_idx[:, None]
    sel = jnp.concatenate([lag[None] == s_idx[:, None, None], -lag[None] == s_idx[:, None, None]], axis=0).astype(F32)
    resp_t = jnp.einsum('jst,gjqc->gtqsc', sel, jnp.concatenate([kf, kr], axis=1), precision=hp)
    skip = (jnp.eye(t, dtype=F32)[None, :, None, :, None] * jnp.eye(cg, dtype=F32)[None, None, :, None, :]
            * d_skip.reshape(groups, 1, cg, 1, 1))
    m_tz_t = (resp_t + skip).reshape(groups, t * cg, t * cg)

    n_pow = 8
    kk = (t * (2 ** jnp.arange(n_pow))).astype(F32)
    mp_re, mp_im = lam_pow(kk)
    mp_re = jnp.transpose(mp_re, (0, 1, 3, 2))
    mp_im = jnp.transpose(mp_im, (0, 1, 3, 2))
    mu_re = jnp.concatenate([mp_re[0], mp_re[1]], axis=-1)
    mu_im = jnp.concatenate([mp_im[0], mp_im[1]], axis=-1)
    tr = lambda m: jnp.transpose(m, (0, 2, 1)).astype(BF16)
    return w_in.astype(BF16), m_tz_t.astype(BF16), tr(c_out), mu_re, mu_im


def _pad_rows(x, rows):
    return x if x.shape[0] == rows else jnp.concatenate([x, jnp.zeros((rows - x.shape[0],) + x.shape[1:], x.dtype)], axis=0)


def _s5_body(*refs, nch, n_steps, with_output):
    if with_output:
        (u_ref, h0_ref, win_ref, mtzt_ref, coutt_ref, mure_ref, muim_ref, wglut_ref, bglu_ref, wbt_ref,
         yb_ref, hout_ref, xt_scr, yt_scr) = refs
    else:
        u_ref, h0_ref, win_ref, mure_ref, muim_ref, hout_ref, xt_scr = refs
    t_c, cg, groups = S5_CHUNK, S5_GROUP, S5_GROUPS
    p2 = 2 * S5_STATE
    ncp = xt_scr.shape[-1]
    for t in range(t_c):
        ut = jnp.concatenate([u_ref[0, j, pl.ds(t, nch, stride=t_c), :] for j in range(u_ref.shape[1])], axis=1)
        ut = _pad_rows(ut, ncp)
        xt_scr[:, t] = ut.T.astype(BF16).reshape(groups, cg, ncp)

    r_idx = lax.broadcasted_iota(jnp.int32, (ncp, ncp), 0)
    c_idx = lax.broadcasted_iota(jnp.int32, (ncp, ncp), 1)
    flip = jnp.where(r_idx + c_idx == nch - 1, 1.0, 0.0).astype(BF16)
    row = lax.broadcasted_iota(jnp.int32, (ncp, p2), 0)
    is_fwd = (lax.broadcasted_iota(jnp.int32, (ncp, 2 * p2), 1) % p2) < S5_STATE

    def group(g, carry):
        x = xt_scr[g].reshape(t_c * cg, ncp)
        x_rev = jnp.dot(x, flip, preferred_element_type=F32).astype(BF16)
        local = _dot_tn(jnp.concatenate([x, x_rev], axis=0), win_ref[g])
        l_re, l_im = local[:, :p2], local[:, p2:]
        h0 = h0_ref[0, pl.ds(g, 1), :]
        z_re = jnp.where(row == 0, h0[:, :p2], pltpu.roll(l_re, 1, axis=0))
        z_im = jnp.where(row == 0, h0[:, p2:], pltpu.roll(l_im, 1, axis=0))
        for k in range(n_steps):
            sft = 1 << k
            a_re = mure_ref[g, k:k + 1, :]
            a_im = muim_ref[g, k:k + 1, :]
            p_re = jnp.where(row >= sft, pltpu.roll(z_re, sft, axis=0), 0.0)
            p_im = jnp.where(row >= sft, pltpu.roll(z_im, sft, axis=0), 0.0)
            z_re, z_im = z_re + a_re * p_re - a_im * p_im, z_im + a_re * p_im + a_im * p_re
        if with_output:
            z = jnp.concatenate([z_re, z_im], axis=1).astype(BF16)
            z_flip = jnp.dot(flip, z, preferred_element_type=F32).astype(BF16)
            z_nat = jnp.where(is_fwd, z, z_flip)
            y = jnp.dot(mtzt_ref[g], x, preferred_element_type=F32) + _dot_nt(coutt_ref[g], z_nat)
            yt_scr[g] = jax.nn.gelu(y).astype(BF16).reshape(t_c, cg, ncp)
        a_re = mure_ref[g, 0:1, :]
        a_im = muim_ref[g, 0:1, :]
        e_re, e_im = z_re[nch - 1:nch], z_im[nch - 1:nch]
        hout_ref[0, pl.ds(g, 1), :] = jnp.concatenate(
            [a_re * e_re - a_im * e_im + l_re[nch - 1:nch], a_re * e_im + a_im * e_re + l_im[nch - 1:nch]], axis=1)
        return carry

    lax.fori_loop(0, groups, group, 0, unroll=8)

    if with_output:
        def tail(t, carry):
            ys = yt_scr[:, t].reshape(groups * cg, ncp)
            glu = jax.nn.sigmoid(jnp.dot(wglut_ref[...], ys, preferred_element_type=F32) + bglu_ref[...])
            ybt = jnp.dot(wbt_ref[...], (ys.astype(F32) * glu).astype(BF16), preferred_element_type=F32)
            yb_ref[0, t] = ybt.T[:nch].astype(yb_ref.dtype)
            return carry

        lax.fori_loop(0, t_c, tail, 0, unroll=4)


def _s5(u, h0, w_in_t, m_tz_t, c_out_t, mu_a, mu_b, w_glu_t=None, b_glu=None, w_b_t=None):
    bsz, n_blk, length, _ = u.shape
    width = n_blk * LANES
    with_output = w_glu_t is not None
    nch = length // S5_CHUNK
    ncp = -(-nch // 128) * 128
    p4 = 4 * S5_STATE
    n_steps = max(1, (nch - 1).bit_length())
    tc = S5_CHUNK * S5_GROUP
    one = pl.Buffered(1)
    full = lambda a: pl.BlockSpec(a.shape, lambda b: (0,) * a.ndim, pipeline_mode=one)
    in_specs = [pl.BlockSpec((1, n_blk, length, LANES), lambda b: (b, 0, 0, 0), pipeline_mode=one),
                pl.BlockSpec((1, S5_GROUPS, p4), lambda b: (b, 0, 0)), full(w_in_t)]
    args = [u, h0, w_in_t]
    scratch = [pltpu.VMEM((S5_GROUPS, S5_CHUNK, S5_GROUP, ncp), BF16)]
    out_specs = [pl.BlockSpec((1, S5_GROUPS, p4), lambda b: (b, 0, 0))]
    out_shape = [jax.ShapeDtypeStruct((bsz, S5_GROUPS, p4), F32)]
    vmem = length * width * 4 + 4 * S5_GROUPS * tc * tc * 2 + 2 * S5_GROUPS * tc * ncp * 2 + 8 * 1024 * 1024
    if with_output:
        d_out = w_b_t.shape[0]
        in_specs += [full(m_tz_t), full(c_out_t), full(mu_a), full(mu_b), full(w_glu_t),
                     pl.BlockSpec((width, 1), lambda b: (0, 0)), full(w_b_t)]
        args += [m_tz_t, c_out_t, mu_a, mu_b, w_glu_t, b_glu.reshape(width, 1), w_b_t]
        scratch.append(pltpu.VMEM((S5_GROUPS, S5_CHUNK, S5_GROUP, ncp), BF16))
        out_specs.insert(0, pl.BlockSpec((1, S5_CHUNK, nch, d_out), lambda b: (b, 0, 0, 0)))
        out_shape.insert(0, jax.ShapeDtypeStruct((bsz, S5_CHUNK, nch, d_out), BF16))
        vmem += 2 * S5_CHUNK * nch * d_out * 2 + (width * width + width * d_out) * 2
    else:
        in_specs += [full(mu_a), full(mu_b)]
        args += [mu_a, mu_b]
    return pl.pallas_call(
        functools.partial(_s5_body, nch=nch, n_steps=n_steps, with_output=with_output),
        grid=(bsz,),
        in_specs=in_specs,
        out_specs=out_specs,
        out_shape=out_shape,
        scratch_shapes=scratch,
        compiler_params=_cparams(1, vmem),
        name="s5",
    )(*args)


def _mixout_body(x_ref, hf_ref, hr_ref, yb_ref, mod_ref, g1_ref, g2_ref, gmh_ref, wog_ref, wa_ref,
                 bgate_ref, wo_ref, wr_ref, br_ref,
                 x1_ref, h2_ref, ti_ref, tp_ref, rk_ref, cnt_ref, base_ref, yb_scr):
    x = x_ref[0]
    d = x.shape[-1]
    mod = mod_ref[0]
    h = _norm_mod(x, g1_ref[...], mod[1:2], mod[0:1]).astype(BF16)
    og = jnp.dot(h, wog_ref[...], preferred_element_type=F32)
    hm = hf_ref[0].astype(F32) + hr_ref[0].astype(F32)
    heads = []
    for hd in range(N_HEADS):
        blk = hm[:, hd * D_V:(hd + 1) * D_V]
        heads.append(blk * lax.rsqrt(jnp.mean(blk * blk, axis=-1, keepdims=True) + EPS))
    hn = jnp.concatenate(heads, axis=1) * gmh_ref[...]
    y_a = jnp.dot((hn * jax.nn.sigmoid(og[:, :d])).astype(BF16), wa_ref[...], preferred_element_type=F32)
    n_pos = yb_ref.shape[1]
    for t in range(n_pos):
        blk = yb_ref[0, t].astype(F32)
        for j in range(yb_scr.shape[0]):
            yb_scr[j, pl.ds(t, yb_ref.shape[2], stride=n_pos), :] = blk[:, j * LANES:(j + 1) * LANES]
    y_b = jnp.concatenate([yb_scr[j] for j in range(yb_scr.shape[0])], axis=1)
    gates = jax.nn.sigmoid(og[:, d:] + bgate_ref[...])
    merged = gates[:, :d] * y_a + gates[:, d:] * y_b
    x1 = x + mod[2:3] * jnp.dot(merged.astype(BF16), wo_ref[...], preferred_element_type=F32)
    x1_ref[0] = x1
    h2f = _norm_mod(x1, g2_ref[...], mod[4:5], mod[3:4])
    h2_ref[0] = h2f
    logits = jnp.dot(h2f.astype(BF16), wr_ref[...], preferred_element_type=F32) + br_ref[...]
    tm, n_e = logits.shape
    e_iota = lax.broadcasted_iota(jnp.int32, (tm, n_e), 1)
    lane = lax.broadcasted_iota(jnp.int32, (tm, ti_ref.shape[-1]), 1)
    ti = jnp.zeros(lane.shape, jnp.int32)
    tv = jnp.zeros(lane.shape, F32)
    top = None
    chosen = []
    for k in range(TOP_K):
        mx = jnp.max(logits, axis=-1, keepdims=True)
        idx = jnp.min(jnp.where(logits == mx, e_iota, n_e), axis=-1, keepdims=True)
        top = mx if top is None else top
        ti = jnp.where(lane == k, idx, ti)
        tv = jnp.where(lane == k, jnp.exp(mx - top), tv)
        chosen.append(e_iota == idx)
        logits = jnp.where(chosen[-1], -jnp.inf, logits)
    ti_ref[0] = ti
    tp_ref[0] = tv / jnp.sum(tv, axis=-1, keepdims=True)

    @pl.when((pl.program_id(0) == 0) & (pl.program_id(1) == 0))
    def _():
        base_ref[...] = jnp.zeros(base_ref.shape, F32)

    onehot = jnp.zeros((tm, n_e), F32)
    for sel in chosen:
        onehot = onehot + jnp.where(sel, 1.0, 0.0)
    below = (lax.broadcasted_iota(jnp.int32, (tm, tm), 1) < lax.broadcasted_iota(jnp.int32, (tm, tm), 0))
    before = jnp.dot(jnp.where(below, 1.0, 0.0).astype(BF16), onehot.astype(BF16),
                     preferred_element_type=F32) + base_ref[...]
    rk = jnp.zeros(lane.shape, jnp.int32)
    for k, sel in enumerate(chosen):
        rank = jnp.sum(jnp.where(sel, before, 0.0), axis=-1, keepdims=True)
        rk = jnp.where(lane == k, rank.astype(jnp.int32), rk)
    rk_ref[0] = rk
    base_ref[...] = base_ref[...] + jnp.sum(onehot, axis=0, keepdims=True)
    cnt_ref[...] = base_ref[...].astype(jnp.int32)


def _mixout(x, h_f, h_r, y_b, mod, g1, g2, g_mh, w_og, w_a, b_gate, w_o, w_r, b_r, tm):
    bsz, length, d = x.shape
    n_pos = y_b.shape[1]
    n_e = w_r.shape[-1]
    tok = lambda b, i: (b, i, 0)
    const2 = lambda b, i: (0, 0)

    def wspec(w):
        return pl.BlockSpec(w.shape, const2, pipeline_mode=pl.Buffered(1))

    weights = (w_og, w_a, w_o, w_r)
    w_bytes = sum(int(w.size) * w.dtype.itemsize for w in weights)
    vmem = w_bytes + 2 * tm * (d * 4 + 3 * d * 2 + d * 4 + d * 4) + 15 * tm * d * 4
    return pl.pallas_call(
        _mixout_body,
        grid=(bsz, length // tm),
        in_specs=[pl.BlockSpec((1, tm, d), tok), pl.BlockSpec((1, tm, d), tok), pl.BlockSpec((1, tm, d), tok),
                  pl.BlockSpec((1, n_pos, tm // n_pos, d), lambda b, i: (b, 0, i, 0)),
                  pl.BlockSpec((1, 6, d), lambda b, i: (b, 0, 0)),
                  pl.BlockSpec((1, d), const2), pl.BlockSpec((1, d), const2), pl.BlockSpec((1, d), const2),
                  wspec(w_og), wspec(w_a),
                  pl.BlockSpec((1, 2 * d), const2), wspec(w_o), wspec(w_r), pl.BlockSpec((1, n_e), const2)],
        out_specs=[pl.BlockSpec((1, tm, d), tok), pl.BlockSpec((1, tm, d), tok),
                   pl.BlockSpec((1, tm, 8), tok), pl.BlockSpec((1, tm, 8), tok), pl.BlockSpec((1, tm, 8), tok),
                   pl.BlockSpec((1, n_e), const2)],
        out_shape=[jax.ShapeDtypeStruct((bsz, length, d), F32),
                   jax.ShapeDtypeStruct((bsz, length, d), F32),
                   jax.ShapeDtypeStruct((bsz, length, 8), jnp.int32),
                   jax.ShapeDtypeStruct((bsz, length, 8), F32),
                   jax.ShapeDtypeStruct((bsz, length, 8), jnp.int32),
                   jax.ShapeDtypeStruct((1, n_e), jnp.int32)],
        scratch_shapes=[pltpu.VMEM((1, n_e), F32), pltpu.VMEM((d // LANES, tm, LANES), F32)],
        compiler_params=_cparams(2, vmem),
        name="mixout",
    )(x, h_f, h_r, y_b, mod, g1.reshape(1, d), g2.reshape(1, d), g_mh.reshape(1, d), w_og, w_a,
      b_gate.reshape(1, 2 * d), w_o, w_r, b_r.reshape(1, n_e))


def _rowpos_body(ti_ref, rk_ref, rs_ref, pos_ref):
    ti = ti_ref[...]
    rk = rk_ref[...]
    n, w = ti.shape
    n_e = rs_ref.shape[-1]
    e_iota = lax.broadcasted_iota(jnp.int32, (n, n_e), 1)
    lane = lax.broadcasted_iota(jnp.int32, (n, w), 1)
    pos = jnp.zeros((n, w), jnp.int32)
    for k in range(TOP_K):
        start = jnp.sum(jnp.where(e_iota == ti[:, k:k + 1], rs_ref[...], 0), axis=-1, keepdims=True)
        pos = jnp.where(lane == k, start + rk[:, k:k + 1], pos)
    pos_ref[...] = pos


def _rowpos(top_i, rank, row_start, tn):
    n, w = top_i.shape
    n_e = row_start.shape[-1]
    return pl.pallas_call(
        _rowpos_body,
        grid=(n // tn,),
        in_specs=[pl.BlockSpec((tn, w), lambda i: (i, 0)), pl.BlockSpec((tn, w), lambda i: (i, 0)),
                  pl.BlockSpec((1, n_e), lambda i: (0, 0))],
        out_specs=pl.BlockSpec((tn, w), lambda i: (i, 0)),
        out_shape=jax.ShapeDtypeStruct((n, w), jnp.int32),
        compiler_params=_cparams(1, 16 * tn * 128 * 4),
        name="rowpos",
    )(top_i, rank, row_start)


def _dispatch_body(pos_ref, h_ref, xs_hbm, sem):
    tm = h_ref.shape[0]

    def issue(i, carry):
        base = pl.multiple_of(i * SUBLANES, SUBLANES)
        for s in range(SUBLANES):
            for k in range(TOP_K):
                row = pos_ref[(base + s) * TOP_K + k]
                pltpu.make_async_copy(h_ref.at[pl.ds(base + s, 1), :], xs_hbm.at[row],
                                      sem).start(priority=k % 2)
        return carry

    lax.fori_loop(0, tm // SUBLANES, issue, 0)
    for k in range(TOP_K):
        pltpu.make_async_copy(h_ref, h_ref, sem).wait()


def _dispatch(pos_flat, h_rows, n_rows, tm):
    n, w = h_rows.shape
    return pl.pallas_call(
        _dispatch_body,
        grid=(n // tm,),
        in_specs=[pl.BlockSpec((tm * TOP_K,), lambda i: (i,), memory_space=pltpu.SMEM),
                  pl.BlockSpec((tm, w), lambda i: (i, 0))],
        out_specs=pl.BlockSpec(memory_space=pl.ANY),
        out_shape=jax.ShapeDtypeStruct((n_rows, 1, w), h_rows.dtype),
        scratch_shapes=[pltpu.SemaphoreType.DMA(())],
        compiler_params=_cparams(1, 8 * tm * w * 4),
        name="dispatch",
    )(pos_flat, h_rows)


def _experts_body(te_ref, tv_ref, nt_ref, xs_ref, win_ref, bin_ref, wout_ref, bout_ref, y_ref, win_bf, wout_bf,
                  *, f_chunk, cast_rows):
    i = pl.program_id(0)
    e = te_ref[i]
    e_prev = te_ref[jnp.maximum(i - 1, 0)]
    d, f2 = win_bf.shape
    f = f2 // 2

    @pl.when((i == 0) | (e != e_prev))
    def _():
        def cast_in(r, carry):
            rows = pl.ds(pl.multiple_of(r * cast_rows, cast_rows), cast_rows)
            win_bf[rows, :] = win_ref[0, rows, :].astype(BF16)
            return carry

        def cast_out(r, carry):
            rows = pl.ds(pl.multiple_of(r * cast_rows, cast_rows), cast_rows)
            wout_bf[rows, :] = wout_ref[0, rows, :].astype(BF16)
            return carry

        lax.fori_loop(0, d // cast_rows, cast_in, 0)
        lax.fori_loop(0, f // cast_rows, cast_out, 0)

    @pl.when(i < nt_ref[0])
    def _():
        xs = xs_ref[:, 0, :]
        live = lax.broadcasted_iota(jnp.int32, xs.shape, 0) < tv_ref[i]
        x = jnp.where(live, xs, 0.0).astype(BF16)
        acc = None
        for c in range(f // f_chunk):
            lo = c * f_chunk
            zg = jnp.dot(x, win_bf[:, lo:lo + f_chunk], preferred_element_type=F32) + bin_ref[0, :, lo:lo + f_chunk]
            zl = (jnp.dot(x, win_bf[:, f + lo:f + lo + f_chunk], preferred_element_type=F32)
                  + bin_ref[0, :, f + lo:f + lo + f_chunk])
            glu = jnp.minimum(zg, SWIGLU_LIMIT)
            lin = jnp.clip(zl, -SWIGLU_LIMIT, SWIGLU_LIMIT)
            act = glu * jax.nn.sigmoid(SWIGLU_ALPHA * glu) * (lin + 1.0)
            part = jnp.dot(act.astype(BF16), wout_bf[lo:lo + f_chunk, :], preferred_element_type=F32)
            acc = part if acc is None else acc + part
        y_ref[:, 0, :] = acc + bout_ref[0]

    @pl.when(i >= nt_ref[0])
    def _():
        y_ref[...] = jnp.zeros(y_ref.shape, y_ref.dtype)


def _experts(tile_expert, tile_valid, n_tiles, xs, w_e_in, b_e_in, w_e_out, b_e_out, tm):
    rows, _, d = xs.shape
    n_e, _, f2 = w_e_in.shape
    f = f2 // 2
    nt_max = rows // tm
    row_map = lambda i, te, tv, nt: (jnp.minimum(i, nt[0] - 1), 0, 0)
    exp_map = lambda i, te, tv, nt: (te[i], 0, 0)
    vmem = 2 * (d * f2 + f * d) * 4 + (d * f2 + f * d) * 2 + 4 * tm * d * 2 + 10 * tm * d * 4
    grid_spec = pltpu.PrefetchScalarGridSpec(
        num_scalar_prefetch=3,
        grid=(nt_max,),
        in_specs=[pl.BlockSpec((tm, 1, d), row_map),
                  pl.BlockSpec((1, d, f2), exp_map), pl.BlockSpec((1, 1, f2), exp_map),
                  pl.BlockSpec((1, f, d), exp_map), pl.BlockSpec((1, 1, d), exp_map)],
        out_specs=pl.BlockSpec((tm, 1, d), lambda i, te, tv, nt: (i, 0, 0)),
        scratch_shapes=[pltpu.VMEM((d, f2), BF16), pltpu.VMEM((f, d), BF16)],
    )
    return pl.pallas_call(
        functools.partial(_experts_body, f_chunk=512, cast_rows=128),
        grid_spec=grid_spec,
        out_shape=jax.ShapeDtypeStruct((rows, 1, d), F32),
        compiler_params=_cparams(1, vmem),
        name="experts",
    )(tile_expert, tile_valid, n_tiles, xs, w_e_in, b_e_in.reshape(n_e, 1, f2), w_e_out, b_e_out.reshape(n_e, 1, d))


def _final_body(pos_ref, posn_ref, x1_ref, tp_ref, gt_ref, g_ref, y_hbm, o_ref, buf, sem):
    i = pl.program_id(0)
    n = pl.num_programs(0)
    tm = x1_ref.shape[0]

    def gather(p_ref, slot):
        def issue(i, carry):
            base = pl.multiple_of(i * SUBLANES, SUBLANES)
            for s in range(SUBLANES):
                for k in range(TOP_K):
                    row = p_ref[(base + s) * TOP_K + k]
                    pltpu.make_async_copy(y_hbm.at[row], buf.at[slot, k, pl.ds(base + s, 1), :],
                                          sem.at[slot]).start(priority=k % 2)
            return carry

        lax.fori_loop(0, tm // SUBLANES, issue, 0)

    @pl.when(i == 0)
    def _():
        gather(pos_ref, 0)

    @pl.when(i + 1 < n)
    def _():
        gather(posn_ref, (i + 1) % 2)

    slot = i % 2
    for k in range(TOP_K):
        pltpu.make_async_copy(buf.at[slot, k], buf.at[slot, k], sem.at[slot]).wait()
    moe = None
    for k in range(TOP_K):
        term = tp_ref[:, k:k + 1] * buf[slot, k]
        moe = term if moe is None else moe + term
    x2 = x1_ref[...] + gt_ref[0] * moe
    o_ref[...] = x2 * lax.rsqrt(jnp.mean(x2 * x2, axis=-1, keepdims=True) + EPS) * g_ref[...]


def _final(pos_flat, x1, top_p, gt2, g_final, y_rows, tm):
    n_tok, d = x1.shape
    bsz = gt2.shape[0]
    per_b = n_tok // bsz // tm
    nxt = lambda i: (jnp.minimum(i + 1, n_tok // tm - 1),)
    return pl.pallas_call(
        _final_body,
        grid=(n_tok // tm,),
        in_specs=[pl.BlockSpec((tm * TOP_K,), lambda i: (i,), memory_space=pltpu.SMEM),
                  pl.BlockSpec((tm * TOP_K,), nxt, memory_space=pltpu.SMEM),
                  pl.BlockSpec((tm, d), lambda i: (i, 0)),
                  pl.BlockSpec((tm, top_p.shape[-1]), lambda i: (i, 0)),
                  pl.BlockSpec((1, 1, d), lambda i: (i // per_b, 0, 0)),
                  pl.BlockSpec((1, d), lambda i: (0, 0)),
                  pl.BlockSpec(memory_space=pl.ANY)],
        out_specs=pl.BlockSpec((tm, d), lambda i: (i, 0)),
        out_shape=jax.ShapeDtypeStruct((n_tok, d), F32),
        scratch_shapes=[pltpu.VMEM((2, TOP_K, tm, d), F32), pltpu.SemaphoreType.DMA((2,))],
        compiler_params=_cparams(1, 2 * TOP_K * tm * d * 4 + 4 * tm * d * 4 + 8 * tm * d * 4),
        name="final",
    )(pos_flat, pos_flat, x1, top_p, gt2, g_final.reshape(1, d), y_rows)


def _tile_table(counts, tm, nt_max):
    tiles = (counts + tm - 1) // tm
    tile_end = jnp.cumsum(tiles)
    tile_start = tile_end - tiles
    n_tiles = tile_end[-1]
    tile_ids = jnp.arange(nt_max, dtype=jnp.int32)
    last_e = jnp.sum((n_tiles - 1) >= tile_end).astype(jnp.int32)
    te = jnp.sum(tile_ids[:, None] >= tile_end[None, :], axis=1).astype(jnp.int32)
    te = jnp.where(tile_ids < n_tiles, te, last_e)
    sel = te[:, None] == jnp.arange(counts.shape[0], dtype=jnp.int32)[None, :]
    cnt_t = jnp.sum(jnp.where(sel, counts[None, :], 0), axis=1)
    start_t = jnp.sum(jnp.where(sel, tile_start[None, :], 0), axis=1)
    live = jnp.clip(cnt_t - (tile_ids - start_t) * tm, 0, tm).astype(jnp.int32)
    live = jnp.where(tile_ids < n_tiles, live, 0)
    return (tile_start * tm).astype(jnp.int32), te, live, n_tiles.reshape(1).astype(jnp.int32)


def kernel(x, c, ctx, c_ctx, w_ada, b_ada, g_norm1, g_norm2, w_in, w_conv_qk, b_ifgate, g_mh, w_branch_m,
           s5_a_re, s5_a_im, s5_log_dt, s5_b_re, s5_b_im, s5_c_re, s5_c_im, s5_d, w_glu, b_glu, w_branch_s,
           b_merge_gate, w_o, w_router, b_router, w_e_in, b_e_in, w_e_out, b_e_out, g_final):
    bsz, length, d = x.shape
    l_ctx = ctx.shape[1]
    n_qk = 2 * N_HEADS * D_QK
    n_v = N_HEADS * D_V
    n_if = 4 * N_HEADS
    n_u = S5_GROUPS * S5_GROUP
    off_if = n_qk + n_v
    off_u = off_if + n_if
    off_o = off_u + n_u
    layer = 0

    pad_rows = -(bsz + 1) % 8
    c_rows = jnp.concatenate([c, c_ctx[None, :], jnp.zeros((pad_rows, d), F32)], axis=0)
    mod_all = _ada(c_rows, w_ada[layer], b_ada[layer])
    mod = mod_all[:bsz].reshape(bsz, 6, d)
    mod_c = mod_all[bsz, :2 * d].reshape(2, 1, 1, d)

    w_l = w_in[layer]
    w_state = jnp.concatenate([w_l[:, :off_if], w_l[:, off_u:off_o]], axis=1).astype(BF16)
    w_if_t = w_l[:, off_if:off_u].T.astype(BF16)
    b_if = b_ifgate[layer].reshape(n_if)
    proj_c = _proj(ctx, g_norm1[layer], mod_c[1], mod_c[0], w_state, w_if_t, b_if, n_qk, n_v, n_u, l_ctx)
    proj_l = _proj(x, g_norm1[layer], mod[:, 1:2], mod[:, 0:1], w_state, w_if_t, b_if, n_qk, n_v, n_u, 512)
    qk_c, v_c, u_c, gif_c = proj_c
    qk_l, v_l, u_l, gif_l = proj_l

    w9 = w_conv_qk[layer].reshape(9, n_qk)
    q_c, kt_c = _conv(qk_c, w9, D_QK ** -0.5, l_ctx, l_ctx)
    q_l, kt_l = _conv(qk_l, w9, D_QK ** -0.5, GRID_W, min(1024, length))

    st0 = jnp.zeros((bsz, 2 * N_HEADS, D_QK, 2 * D_V), F32)
    m0 = jnp.zeros((bsz, 2 * N_HEADS, 1), F32)
    _, _, st_c, m_c = _mlstm(q_c, kt_c, v_c, gif_c, st0, m0)
    h_f, h_r, _, _ = _mlstm(q_l, kt_l, v_l, gif_l, st_c, m_c)

    w_in_t, m_tz_t, c_out_t, mu_a, mu_b = _s5_tables(
        s5_a_re[layer], s5_a_im[layer], s5_log_dt[layer], s5_b_re[layer], s5_b_im[layer],
        s5_c_re[layer], s5_c_im[layer], s5_d[layer])
    hs0 = jnp.zeros((bsz, S5_GROUPS, 4 * S5_STATE), F32)
    hs_c, = _s5(u_c, hs0, w_in_t, m_tz_t, c_out_t, mu_a, mu_b)
    y_b, _ = _s5(u_l, hs_c, w_in_t, m_tz_t, c_out_t, mu_a, mu_b,
                 w_glu[layer].T.astype(BF16), b_glu[layer], w_branch_s[layer].T.astype(BF16))

    w_og = w_l[:, off_o:].astype(BF16)
    x1, h2, top_i, top_p, rank, counts = _mixout(
        x, h_f, h_r, y_b, mod, g_norm1[layer], g_norm2[layer], g_mh[layer], w_og, w_branch_m[layer].astype(BF16),
        b_merge_gate[layer], w_o[layer].astype(BF16), w_router[layer].astype(BF16), b_router[layer], 512)

    tm_e = 512
    n_tok = bsz * length
    nt_max = n_tok * TOP_K // tm_e + N_EXPERTS
    row_start, tile_expert, tile_live, n_tiles = _tile_table(counts.reshape(N_EXPERTS), tm_e, nt_max)
    pos = _rowpos(top_i.reshape(n_tok, 8), rank.reshape(n_tok, 8), row_start.reshape(1, N_EXPERTS), min(4096, n_tok))
    pos_flat = pos[:, :TOP_K].reshape(n_tok * TOP_K)
    xs = _dispatch(pos_flat, h2.reshape(n_tok, d), nt_max * tm_e, min(1024, n_tok))
    y_rows = _experts(tile_expert, tile_live, n_tiles, xs, w_e_in[layer], b_e_in[layer], w_e_out[layer],
                      b_e_out[layer], tm_e)
    out = _final(pos_flat, x1.reshape(n_tok, d), top_p.reshape(n_tok, 8), mod[:, 5:6], g_final, y_rows, 512)
    return out.reshape(bsz, length, d)
```

```python
import functools
import math

import jax
import jax.numpy as jnp
from jax import lax
from jax.experimental import pallas as pl
from jax.experimental.pallas import tpu as pltpu

F32 = jnp.float32
BF16 = jnp.bfloat16
EPS = 1e-6

N_HEADS = 8
D_QK = 64
D_V = 128
M_CHUNK = 256
GRID_W = 64
S5_GROUPS = 32
S5_GROUP = 16
S5_STATE = 64
S5_CHUNK = 16
N_EXPERTS = 32
TOP_K = 4
SWIGLU_LIMIT = 7.0
SWIGLU_ALPHA = 1.702

LANES = 128
SUBLANES = 8
V7X_VMEM_BYTES = 64 * 1024 * 1024
_VMEM_CAP = V7X_VMEM_BYTES - 8 * 1024 * 1024


def _cparams(n_axes, vmem_bytes):
    limit = int(min(_VMEM_CAP, max(32 * 1024 * 1024, vmem_bytes)))
    return pltpu.CompilerParams(dimension_semantics=("arbitrary",) * n_axes, vmem_limit_bytes=limit)


def _silu(x):
    return x * jax.nn.sigmoid(x)


def _norm_mod(x, g, scale, shift):
    ms = jnp.mean(x * x, axis=-1, keepdims=True)
    return (x * lax.rsqrt(ms + EPS) * g) * (1.0 + scale) + shift


def _split3(x):
    hi = x.astype(BF16)
    r1 = x - hi.astype(F32)
    mid = r1.astype(BF16)
    lo = (r1 - mid.astype(F32)).astype(BF16)
    return hi, mid, lo


def _dot_nt(a, b):
    return lax.dot_general(a, b, (((1,), (1,)), ((), ())), preferred_element_type=F32)


def _dot_tn(a, b):
    return lax.dot_general(a, b, (((0,), (0,)), ((), ())), preferred_element_type=F32)


def _ada_body(c_ref, w_ref, b_ref, o_ref):
    s = _silu(c_ref[...])
    o_ref[...] = jnp.dot(s, w_ref[...], preferred_element_type=F32,
                         precision=lax.Precision.HIGHEST) + b_ref[...]


def _ada(c_rows, w_ada, b_ada):
    rows, d = c_rows.shape
    n = w_ada.shape[1]
    tn = 1024
    return pl.pallas_call(
        _ada_body,
        grid=(n // tn,),
        in_specs=[pl.BlockSpec((rows, d), lambda j: (0, 0)),
                  pl.BlockSpec((d, tn), lambda j: (0, j)),
                  pl.BlockSpec((1, tn), lambda j: (0, j))],
        out_specs=pl.BlockSpec((rows, tn), lambda j: (0, j)),
        out_shape=jax.ShapeDtypeStruct((rows, n), F32),
        compiler_params=_cparams(1, 4 * d * tn * 4),
        name="ada",
    )(c_rows, w_ada, b_ada.reshape(1, n))


def _proj_body(x_ref, g_ref, sc_ref, sh_ref, w_ref, wif_ref, bif_ref, qk_ref, v_ref, u_ref, gif_ref):
    h = _norm_mod(x_ref[0], g_ref[...], sc_ref[0], sh_ref[0]).astype(BF16)
    n_qk = qk_ref.shape[-1]
    n_v = v_ref.shape[-1]
    qk_ref[0] = jnp.dot(h, w_ref[:, :n_qk], preferred_element_type=F32).astype(BF16)
    v_ref[0] = jnp.dot(h, w_ref[:, n_qk:n_qk + n_v], preferred_element_type=F32).astype(BF16)
    u = jnp.dot(h, w_ref[:, n_qk + n_v:], preferred_element_type=F32)
    for j in range(u_ref.shape[1]):
        u_ref[0, j] = u[:, j * LANES:(j + 1) * LANES]
    gif_ref[0] = _dot_nt(wif_ref[...], h) + bif_ref[...]


def _proj(x, g, scale, shift, w_state, w_if_t, b_if, n_qk, n_v, n_u, tm):
    bsz, length, d = x.shape
    n_if = b_if.shape[-1]
    per_batch = scale.shape[0] == bsz
    mod_map = (lambda b, i: (b, 0, 0)) if per_batch else (lambda b, i: (0, 0, 0))
    cols = w_state.shape[1]
    vmem = 2 * (tm * d * 4 + d * cols * 2 + tm * (n_qk + n_v + n_u) * 2 + tm * 128 * 4) + 6 * tm * d * 4
    return pl.pallas_call(
        _proj_body,
        grid=(bsz, length // tm),
        in_specs=[pl.BlockSpec((1, tm, d), lambda b, i: (b, i, 0)),
                  pl.BlockSpec((1, d), lambda b, i: (0, 0)),
                  pl.BlockSpec((1, 1, d), mod_map),
                  pl.BlockSpec((1, 1, d), mod_map),
                  pl.BlockSpec((d, cols), lambda b, i: (0, 0)),
                  pl.BlockSpec((n_if, d), lambda b, i: (0, 0)),
                  pl.BlockSpec((n_if, 1), lambda b, i: (0, 0))],
        out_specs=[pl.BlockSpec((1, tm, n_qk), lambda b, i: (b, i, 0)),
                   pl.BlockSpec((1, tm, n_v), lambda b, i: (b, i, 0)),
                   pl.BlockSpec((1, n_u // LANES, tm, LANES), lambda b, i: (b, 0, i, 0)),
                   pl.BlockSpec((1, n_if, tm), lambda b, i: (b, 0, i))],
        out_shape=[jax.ShapeDtypeStruct((bsz, length, n_qk), BF16),
                   jax.ShapeDtypeStruct((bsz, length, n_v), BF16),
                   jax.ShapeDtypeStruct((bsz, n_u // LANES, length, LANES), F32),
                   jax.ShapeDtypeStruct((bsz, n_if, length), F32)],
        compiler_params=_cparams(2, vmem),
        name="proj",
    )(x, g.reshape(1, d), scale, shift, w_state, w_if_t, b_if.reshape(n_if, 1))


def _conv_body(main_ref, prev_ref, next_ref, w_ref, q_ref, kt_ref, *, width, q_scale):
    i = pl.program_id(1)
    last = pl.num_programs(1) - 1
    t = main_ref.shape[1]
    n = t + 2 * width
    main = main_ref[0].astype(F32)
    prev = jnp.where(i > 0, prev_ref[0].astype(F32), 0.0)
    nxt = jnp.where(i < last, next_ref[0].astype(F32), 0.0)
    ext = jnp.concatenate([prev, main, nxt], axis=0)
    col = lax.broadcasted_iota(jnp.int32, (t, 1), 0) % width
    acc = None
    for dx in (-1, 0, 1):
        shifted = ext if dx == 0 else pltpu.roll(ext, (-dx) % n, axis=0)
        part = None
        for dy in (-1, 0, 1):
            tap = w_ref[(dy + 1) * 3 + (dx + 1):(dy + 1) * 3 + (dx + 1) + 1, :]
            term = tap * shifted[width + dy * width:width + dy * width + t]
            part = term if part is None else part + term
        if dx == -1:
            part = jnp.where(col == 0, 0.0, part)
        elif dx == 1:
            part = jnp.where(col == width - 1, 0.0, part)
        acc = part if acc is None else acc + part
    y = _silu(acc)

    @pl.when(pl.program_id(2) == 0)
    def _():
        q_ref[0] = (y * q_scale).astype(q_ref.dtype)

    @pl.when(pl.program_id(2) == 1)
    def _():
        kt_ref[0] = y.T.astype(kt_ref.dtype)


def _conv(qk_pre, w9, q_scale, width, t_block):
    bsz, length, ch2 = qk_pre.shape
    ch = ch2 // 2
    rpb = t_block // width
    n_rows = length // width
    vmem = 4 * (t_block + 2 * width) * ch * 2 + 14 * (t_block + 2 * width) * ch * 4
    return pl.pallas_call(
        functools.partial(_conv_body, width=width, q_scale=q_scale),
        grid=(bsz, length // t_block, 2),
        in_specs=[pl.BlockSpec((1, t_block, ch), lambda b, i, c: (b, i, c)),
                  pl.BlockSpec((1, width, ch), lambda b, i, c: (b, jnp.maximum(i * rpb - 1, 0), c)),
                  pl.BlockSpec((1, width, ch), lambda b, i, c: (b, jnp.minimum((i + 1) * rpb, n_rows - 1), c)),
                  pl.BlockSpec((9, ch), lambda b, i, c: (0, c))],
        out_specs=[pl.BlockSpec((1, t_block, ch), lambda b, i, c: (b, i, 0)),
                   pl.BlockSpec((1, ch, t_block), lambda b, i, c: (b, 0, i))],
        out_shape=[jax.ShapeDtypeStruct((bsz, length, ch), BF16),
                   jax.ShapeDtypeStruct((bsz, ch, length), BF16)],
        compiler_params=_cparams(3, vmem),
        name="conv",
    )(qk_pre, qk_pre, qk_pre, w9)


def _log_sigmoid(x):
    return jnp.minimum(x, 0.0) - jnp.log1p(jnp.exp(-jnp.abs(x)))


def _cumsum_lanes_exact(x, reverse):
    t = x.shape[-1]
    r_idx = lax.broadcasted_iota(jnp.int32, (t, t), 0)
    c_idx = lax.broadcasted_iota(jnp.int32, (t, t), 1)
    u01 = jnp.where((r_idx >= c_idx) if reverse else (r_idx <= c_idx), 1.0, 0.0).astype(BF16)
    out = None
    for piece in _split3(x):
        p = jnp.dot(piece, u01, preferred_element_type=F32)
        out = p if out is None else out + p
    return out


def _cummax_lanes(x, reverse):
    t = x.shape[-1]
    lane = lax.broadcasted_iota(jnp.int32, x.shape, 1)
    sh = 1
    while sh < t:
        if reverse:
            cand = jnp.where(lane < t - sh, pltpu.roll(x, t - sh, axis=1), -jnp.inf)
        else:
            cand = jnp.where(lane >= sh, pltpu.roll(x, sh, axis=1), -jnp.inf)
        x = jnp.maximum(x, cand)
        sh *= 2
    return x


def _mlstm_gate_rows(g_ref, m_col, d):
    t = g_ref.shape[-1]
    base = d * 2 * N_HEADS
    li = g_ref[0, base:base + N_HEADS, :]
    lf = _log_sigmoid(g_ref[0, base + N_HEADS:base + 2 * N_HEADS, :])
    b = _cumsum_lanes_exact(lf, reverse=(d == 1))
    g = li - b
    a = jnp.maximum(_cummax_lanes(g, reverse=(d == 1)), m_col)
    end = 0 if d == 1 else t - 1
    a_end = a[:, end:end + 1]
    return dict(g=g, a=a, ie=jnp.exp(m_col - a), emt=jnp.exp(-b - a), we=jnp.exp(g - a_end),
                dec=jnp.exp(m_col - a_end), m_new=b[:, end:end + 1] + a_end)


def _mlstm_body(qf_ref, kf_ref, vf_ref, gf_ref, qr_ref, kr_ref, vr_ref, gr_ref, st0_ref, m0_ref,
                hf_ref, hr_ref, st_ref, m_ref):
    @pl.when(pl.program_id(1) == 0)
    def _():
        st_ref[...] = st0_ref[...]
        m_ref[...] = m0_ref[...]

    t = qf_ref.shape[1]
    m_all = m_ref[0]
    rows = [_mlstm_gate_rows(g_ref, m_all[d * N_HEADS:(d + 1) * N_HEADS], d) for d, g_ref in ((0, gf_ref), (1, gr_ref))]
    both = lambda name: jnp.concatenate([rows[0][name], rows[1][name]], axis=0)
    n_hd = 2 * N_HEADS
    pad = jnp.zeros((128 - 3 * n_hd, t), F32)
    cols = jnp.concatenate([both('a'), both('ie'), both('emt'), pad], axis=0).T
    m_ref[0] = both('m_new')
    r_idx = lax.broadcasted_iota(jnp.int32, (t, t), 0)
    c_idx = lax.broadcasted_iota(jnp.int32, (t, t), 1)
    ones_blk = jnp.ones((t, D_V), BF16)
    for d, (q_ref, kt_ref, v_ref, h_ref) in enumerate(((qf_ref, kf_ref, vf_ref, hf_ref), (qr_ref, kr_ref, vr_ref, hr_ref))):
        causal = (c_idx >= r_idx) if d == 1 else (c_idx <= r_idx)
        for h in range(N_HEADS):
            j = d * N_HEADS + h
            q = q_ref[0, :, h * D_QK:(h + 1) * D_QK]
            kt = kt_ref[0, h * D_QK:(h + 1) * D_QK, :]
            v1 = jnp.concatenate([v_ref[0, :, h * D_V:(h + 1) * D_V], ones_blk], axis=1)
            state = st_ref[0, j]
            a_col = cols[:, j:j + 1]
            ie_col = cols[:, n_hd + j:n_hd + j + 1]
            emt_col = cols[:, 2 * n_hd + j:2 * n_hd + j + 1]
            dmat = jnp.exp(jnp.where(causal, rows[d]['g'][h:h + 1, :] - a_col, -jnp.inf))
            s = (jnp.dot(q, kt, preferred_element_type=F32) * dmat).astype(BF16)
            z = jnp.dot(q, state.astype(BF16), preferred_element_type=F32)
            p = jnp.dot(s, v1, preferred_element_type=F32)
            num = ie_col * z[:, :D_V] + p[:, :D_V]
            den = ie_col * z[:, D_V:] + p[:, D_V:]
            h_ref[0, :, h * D_V:(h + 1) * D_V] = (num / jnp.maximum(jnp.abs(den), emt_col)).astype(h_ref.dtype)
            kw = (kt.astype(F32) * rows[d]['we'][h:h + 1, :]).astype(BF16)
            st_ref[0, j] = rows[d]['dec'][h:h + 1, :] * state + jnp.dot(kw, v1, preferred_element_type=F32)


def _mlstm(q, k_t, v, gif_t, st0, m0):
    bsz, length, hv = v.shape
    t = M_CHUNK
    nc = length // t
    hq = N_HEADS * D_QK
    ng = gif_t.shape[1]
    fwd = lambda b, i: (b, i, 0)
    rev = lambda b, i: (b, nc - 1 - i, 0)
    fwd_t = lambda b, i: (b, 0, i)
    rev_t = lambda b, i: (b, 0, nc - 1 - i)
    st_spec = pl.BlockSpec((1,) + st0.shape[1:], lambda b, i: (b, 0, 0, 0))
    m_spec = pl.BlockSpec((1,) + m0.shape[1:], lambda b, i: (b, 0, 0))
    vmem = 24 * 1024 * 1024
    return pl.pallas_call(
        _mlstm_body,
        grid=(bsz, nc),
        in_specs=[pl.BlockSpec((1, t, hq), fwd), pl.BlockSpec((1, hq, t), fwd_t),
                  pl.BlockSpec((1, t, hv), fwd), pl.BlockSpec((1, ng, t), fwd_t),
                  pl.BlockSpec((1, t, hq), rev), pl.BlockSpec((1, hq, t), rev_t),
                  pl.BlockSpec((1, t, hv), rev), pl.BlockSpec((1, ng, t), rev_t),
                  st_spec, m_spec],
        out_specs=[pl.BlockSpec((1, t, hv), fwd), pl.BlockSpec((1, t, hv), rev), st_spec, m_spec],
        out_shape=[jax.ShapeDtypeStruct((bsz, length, hv), BF16),
                   jax.ShapeDtypeStruct((bsz, length, hv), BF16),
                   jax.ShapeDtypeStruct(st0.shape, F32),
                   jax.ShapeDtypeStruct(m0.shape, F32)],
        compiler_params=_cparams(2, vmem),
        name="mlstm",
    )(q, k_t, v, gif_t, q, k_t, v, gif_t, st0, m0)


def _s5_tables(a_re, a_im, log_dt, b_re, b_im, c_re, c_im, d_skip):
    hp = lax.Precision.HIGHEST
    t = S5_CHUNK
    n_dir, groups, p = a_re.shape
    cg = b_re.shape[-1]
    dt = jnp.exp(log_dt)[..., None]

    def lam_pow(n):
        mag = jnp.exp(n * (dt * a_re)[..., None])
        ang = n * (dt * a_im)[..., None]
        return mag * jnp.cos(ang), mag * jnp.sin(ang)

    ab_re, ab_im = (z[..., 0] for z in lam_pow(jnp.ones((1,), F32)))
    den = a_re * a_re + a_im * a_im
    xr = ab_re - 1.0
    cf_re = (xr * a_re + ab_im * a_im) / den
    cf_im = (ab_im * a_re - xr * a_im) / den
    bb_re = cf_re[..., None] * b_re - cf_im[..., None] * b_im
    bb_im = cf_re[..., None] * b_im + cf_im[..., None] * b_re
    jj = jnp.arange(t + 1, dtype=F32)
    lp_re, lp_im = lam_pow(jj)

    def w_dir(d, exps):
        lr = lp_re[d][:, :, exps]
        li = lp_im[d][:, :, exps]
        wr = lr[..., None] * bb_re[d][:, :, None, :] - li[..., None] * bb_im[d][:, :, None, :]
        wi = lr[..., None] * bb_im[d][:, :, None, :] + li[..., None] * bb_re[d][:, :, None, :]
        to_rows = lambda w: jnp.transpose(w, (0, 2, 3, 1)).reshape(groups, t * cg, p)
        return to_rows(wr), to_rows(wi)

    s_idx = jnp.arange(t)
    wf_re, wf_im = w_dir(0, t - 1 - s_idx)
    wr_re, wr_im = w_dir(1, s_idx)
    zw = jnp.zeros_like(wf_re)
    w_in = jnp.concatenate([jnp.concatenate([wf_re, zw, wf_im, zw], axis=-1),
                            jnp.concatenate([zw, wr_re, zw, wr_im], axis=-1)], axis=1)

    def c_dir(d, exps):
        lr = lp_re[d][:, :, exps]
        li = lp_im[d][:, :, exps]
        cr = jnp.transpose(c_re[d], (0, 2, 1))
        ci = jnp.transpose(c_im[d], (0, 2, 1))
        o_re = cr[:, :, None, :] * lr[..., None] - ci[:, :, None, :] * li[..., None]
        o_im = cr[:, :, None, :] * li[..., None] + ci[:, :, None, :] * lr[..., None]
        return o_re.reshape(groups, p, t * cg), (-o_im).reshape(groups, p, t * cg)

    cf_r, cf_i = c_dir(0, s_idx + 1)
    cr_r, cr_i = c_dir(1, t - s_idx)
    c_out = jnp.concatenate([cf_r, cr_r, cf_i, cr_i], axis=1)

    def k_dir(d):
        lr = lp_re[d][:, :, :t]
        li = lp_im[d][:, :, :t]
        clr = c_re[d][:, :, :, None] * lr[:, None] - c_im[d][:, :, :, None] * li[:, None]
        cli = c_re[d][:, :, :, None] * li[:, None] + c_im[d][:, :, :, None] * lr[:, None]
        return (jnp.einsum('gqpj,gpc->gjqc', clr, bb_re[d], precision=hp)
                - jnp.einsum('gqpj,gpc->gjqc', cli, bb_im[d], precision=hp))

    kf = k_dir(0)
    kr = k_dir(1)
    lag = s_idx[None, :] - s_idx[:, None]
    sel = jnp.concatenate([lag[None] == s_idx[:, None, None], -lag[None] == s_idx[:, None, None]], axis=0).astype(F32)
    resp_t = jnp.einsum('jst,gjqc->gtqsc', sel, jnp.concatenate([kf, kr], axis=1), precision=hp)
    skip = (jnp.eye(t, dtype=F32)[None, :, None, :, None] * jnp.eye(cg, dtype=F32)[None, None, :, None, :]
            * d_skip.reshape(groups, 1, cg, 1, 1))
    m_tz_t = (resp_t + skip).reshape(groups, t * cg, t * cg)

    n_pow = 8
    kk = (t * (2 ** jnp.arange(n_pow))).astype(F32)
    mp_re, mp_im = lam_pow(kk)
    mp_re = jnp.transpose(mp_re, (0, 1, 3, 2))
    mp_im = jnp.transpose(mp_im, (0, 1, 3, 2))
    mu_re = jnp.concatenate([mp_re[0], mp_re[1]], axis=-1)
    mu_im = jnp.concatenate([mp_im[0], mp_im[1]], axis=-1)
    tr = lambda m: jnp.transpose(m, (0, 2, 1)).astype(BF16)
    return w_in.astype(BF16), m_tz_t.astype(BF16), tr(c_out), mu_re, mu_im


def _pad_rows(x, rows):
    return x if x.shape[0] == rows else jnp.concatenate([x, jnp.zeros((rows - x.shape[0],) + x.shape[1:], x.dtype)], axis=0)


def _s5_body(*refs, nch, n_steps, with_output):
    if with_output:
        (u_ref, h0_ref, win_ref, mtzt_ref, coutt_ref, mure_ref, muim_ref, wglut_ref, bglu_ref, wbt_ref,
         yb_ref, hout_ref, xt_scr, yt_scr) = refs
    else:
        u_ref, h0_ref, win_ref, mure_ref, muim_ref, hout_ref, xt_scr = refs
    t_c, cg, groups = S5_CHUNK, S5_GROUP, S5_GROUPS
    p2 = 2 * S5_STATE
    ncp = xt_scr.shape[-1]
    for t in range(t_c):
        ut = jnp.concatenate([u_ref[0, j, pl.ds(t, nch, stride=t_c), :] for j in range(u_ref.shape[1])], axis=1)
        ut = _pad_rows(ut, ncp)
        xt_scr[:, t] = ut.T.astype(BF16).reshape(groups, cg, ncp)

    r_idx = lax.broadcasted_iota(jnp.int32, (ncp, ncp), 0)
    c_idx = lax.broadcasted_iota(jnp.int32, (ncp, ncp), 1)
    flip = jnp.where(r_idx + c_idx == nch - 1, 1.0, 0.0).astype(BF16)
    row = lax.broadcasted_iota(jnp.int32, (ncp, p2), 0)
    is_fwd = (lax.broadcasted_iota(jnp.int32, (ncp, 2 * p2), 1) % p2) < S5_STATE

    def group(g, carry):
        x = xt_scr[g].reshape(t_c * cg, ncp)
        x_rev = jnp.dot(x, flip, preferred_element_type=F32).astype(BF16)
        local = _dot_tn(jnp.concatenate([x, x_rev], axis=0), win_ref[g])
        l_re, l_im = local[:, :p2], local[:, p2:]
        h0 = h0_ref[0, pl.ds(g, 1), :]
        z_re = jnp.where(row == 0, h0[:, :p2], pltpu.roll(l_re, 1, axis=0))
        z_im = jnp.where(row == 0, h0[:, p2:], pltpu.roll(l_im, 1, axis=0))
        for k in range(n_steps):
            sft = 1 << k
            a_re = mure_ref[g, k:k + 1, :]
            a_im = muim_ref[g, k:k + 1, :]
            p_re = jnp.where(row >= sft, pltpu.roll(z_re, sft, axis=0), 0.0)
            p_im = jnp.where(row >= sft, pltpu.roll(z_im, sft, axis=0), 0.0)
            z_re, z_im = z_re + a_re * p_re - a_im * p_im, z_im + a_re * p_im + a_im * p_re
        if with_output:
            z = jnp.concatenate([z_re, z_im], axis=1).astype(BF16)
            z_flip = jnp.dot(flip, z, preferred_element_type=F32).astype(BF16)
            z_nat = jnp.where(is_fwd, z, z_flip)
            y = jnp.dot(mtzt_ref[g], x, preferred_element_type=F32) + _dot_nt(coutt_ref[g], z_nat)
            yt_scr[g] = jax.nn.gelu(y).astype(BF16).reshape(t_c, cg, ncp)
        a_re = mure_ref[g, 0:1, :]
        a_im = muim_ref[g, 0:1, :]
        e_re, e_im = z_re[nch - 1:nch], z_im[nch - 1:nch]
        hout_ref[0, pl.ds(g, 1), :] = jnp.concatenate(
            [a_re * e_re - a_im * e_im + l_re[nch - 1:nch], a_re * e_im + a_im * e_re + l_im[nch - 1:nch]], axis=1)
        return carry

    lax.fori_loop(0, groups, group, 0, unroll=8)

    if with_output:
        def tail(t, carry):
            ys = yt_scr[:, t].reshape(groups * cg, ncp)
            glu = jax.nn.sigmoid(jnp.dot(wglut_ref[...], ys, preferred_element_type=F32) + bglu_ref[...])
            ybt = jnp.dot(wbt_ref[...], (ys.astype(F32) * glu).astype(BF16), preferred_element_type=F32)
            yb_ref[0, t] = ybt.T[:nch].astype(yb_ref.dtype)
            return carry

        lax.fori_loop(0, t_c, tail, 0, unroll=4)


def _s5(u, h0, w_in_t, m_tz_t, c_out_t, mu_a, mu_b, w_glu_t=None, b_glu=None, w_b_t=None):
    bsz, n_blk, length, _ = u.shape
    width = n_blk * LANES
    with_output = w_glu_t is not None
    nch = length // S5_CHUNK
    ncp = -(-nch // 128) * 128
    p4 = 4 * S5_STATE
    n_steps = max(1, (nch - 1).bit_length())
    tc = S5_CHUNK * S5_GROUP
    one = pl.Buffered(1)
    full = lambda a: pl.BlockSpec(a.shape, lambda b: (0,) * a.ndim, pipeline_mode=one)
    in_specs = [pl.BlockSpec((1, n_blk, length, LANES), lambda b: (b, 0, 0, 0), pipeline_mode=one),
                pl.BlockSpec((1, S5_GROUPS, p4), lambda b: (b, 0, 0)), full(w_in_t)]
    args = [u, h0, w_in_t]
    scratch = [pltpu.VMEM((S5_GROUPS, S5_CHUNK, S5_GROUP, ncp), BF16)]
    out_specs = [pl.BlockSpec((1, S5_GROUPS, p4), lambda b: (b, 0, 0))]
    out_shape = [jax.ShapeDtypeStruct((bsz, S5_GROUPS, p4), F32)]
    vmem = length * width * 4 + 4 * S5_GROUPS * tc * tc * 2 + 2 * S5_GROUPS * tc * ncp * 2 + 8 * 1024 * 1024
    if with_output:
        d_out = w_b_t.shape[0]
        in_specs += [full(m_tz_t), full(c_out_t), full(mu_a), full(mu_b), full(w_glu_t),
                     pl.BlockSpec((width, 1), lambda b: (0, 0)), full(w_b_t)]
        args += [m_tz_t, c_out_t, mu_a, mu_b, w_glu_t, b_glu.reshape(width, 1), w_b_t]
        scratch.append(pltpu.VMEM((S5_GROUPS, S5_CHUNK, S5_GROUP, ncp), BF16))
        out_specs.insert(0, pl.BlockSpec((1, S5_CHUNK, nch, d_out), lambda b: (b, 0, 0, 0)))
        out_shape.insert(0, jax.ShapeDtypeStruct((bsz, S5_CHUNK, nch, d_out), BF16))
        vmem += 2 * S5_CHUNK * nch * d_out * 2 + (width * width + width * d_out) * 2
    else:
        in_specs += [full(mu_a), full(mu_b)]
        args += [mu_a, mu_b]
    return pl.pallas_call(
        functools.partial(_s5_body, nch=nch, n_steps=n_steps, with_output=with_output),
        grid=(bsz,),
        in_specs=in_specs,
        out_specs=out_specs,
        out_shape=out_shape,
        scratch_shapes=scratch,
        compiler_params=_cparams(1, vmem),
        name="s5",
    )(*args)


def _mixout_body(x_ref, hf_ref, hr_ref, yb_ref, mod_ref, g1_ref, g2_ref, gmh_ref, wog_ref, wa_ref,
                 bgate_ref, wo_ref, wr_ref, br_ref,
                 x1_ref, h2_ref, ti_ref, tp_ref, rk_ref, cnt_ref, base_ref, yb_scr):
    x = x_ref[0]
    d = x.shape[-1]
    mod = mod_ref[0]
    h = _norm_mod(x, g1_ref[...], mod[1:2], mod[0:1]).astype(BF16)
    og = jnp.dot(h, wog_ref[...], preferred_element_type=F32)
    hm = hf_ref[0].astype(F32) + hr_ref[0].astype(F32)
    heads = []
    for hd in range(N_HEADS):
        blk = hm[:, hd * D_V:(hd + 1) * D_V]
        heads.append(blk * lax.rsqrt(jnp.mean(blk * blk, axis=-1, keepdims=True) + EPS))
    hn = jnp.concatenate(heads, axis=1) * gmh_ref[...]
    y_a = jnp.dot((hn * jax.nn.sigmoid(og[:, :d])).astype(BF16), wa_ref[...], preferred_element_type=F32)
    n_pos = yb_ref.shape[1]
    for t in range(n_pos):
        blk = yb_ref[0, t].astype(F32)
        for j in range(yb_scr.shape[0]):
            yb_scr[j, pl.ds(t, yb_ref.shape[2], stride=n_pos), :] = blk[:, j * LANES:(j + 1) * LANES]
    y_b = jnp.concatenate([yb_scr[j] for j in range(yb_scr.shape[0])], axis=1)
    gates = jax.nn.sigmoid(og[:, d:] + bgate_ref[...])
    merged = gates[:, :d] * y_a + gates[:, d:] * y_b
    x1 = x + mod[2:3] * jnp.dot(merged.astype(BF16), wo_ref[...], preferred_element_type=F32)
    x1_ref[0] = x1
    h2f = _norm_mod(x1, g2_ref[...], mod[4:5], mod[3:4])
    h2_ref[0] = h2f
    logits = jnp.dot(h2f.astype(BF16), wr_ref[...], preferred_element_type=F32) + br_ref[...]
    tm, n_e = logits.shape
    e_iota = lax.broadcasted_iota(jnp.int32, (tm, n_e), 1)
    lane = lax.broadcasted_iota(jnp.int32, (tm, LANES), 1)
    n_slot = ti_ref.shape[0]
    ti = jnp.zeros(lane.shape, jnp.int32)
    tv = jnp.zeros(lane.shape, F32)
    top = None
    chosen = []
    for k in range(TOP_K):
        mx = jnp.max(logits, axis=-1, keepdims=True)
        idx = jnp.min(jnp.where(logits == mx, e_iota, n_e), axis=-1, keepdims=True)
        top = mx if top is None else top
        ti = jnp.where(lane == k, idx, ti)
        tv = jnp.where(lane == k, jnp.exp(mx - top), tv)
        chosen.append(e_iota == idx)
        logits = jnp.where(chosen[-1], -jnp.inf, logits)
    ti_ref[...] = ti.T[:n_slot]
    tp_ref[0] = (tv / jnp.sum(tv, axis=-1, keepdims=True))[:, :tp_ref.shape[-1]]

    @pl.when((pl.program_id(0) == 0) & (pl.program_id(1) == 0))
    def _():
        base_ref[...] = jnp.zeros(base_ref.shape, F32)

    onehot = jnp.zeros((tm, n_e), F32)
    for sel in chosen:
        onehot = onehot + jnp.where(sel, 1.0, 0.0)
    below = (lax.broadcasted_iota(jnp.int32, (tm, tm), 1) < lax.broadcasted_iota(jnp.int32, (tm, tm), 0))
    before = jnp.dot(jnp.where(below, 1.0, 0.0).astype(BF16), onehot.astype(BF16),
                     preferred_element_type=F32) + base_ref[...]
    rk = jnp.zeros(lane.shape, jnp.int32)
    for k, sel in enumerate(chosen):
        rank = jnp.sum(jnp.where(sel, before, 0.0), axis=-1, keepdims=True)
        rk = jnp.where(lane == k, rank.astype(jnp.int32), rk)
    rk_ref[...] = rk.T[:n_slot]
    base_ref[...] = base_ref[...] + jnp.sum(onehot, axis=0, keepdims=True)
    cnt_ref[...] = base_ref[...].astype(jnp.int32)


def _mixout(x, h_f, h_r, y_b, mod, g1, g2, g_mh, w_og, w_a, b_gate, w_o, w_r, b_r, tm):
    bsz, length, d = x.shape
    n_pos = y_b.shape[1]
    n_e = w_r.shape[-1]
    nt = length // tm
    tok = lambda b, i: (b, i, 0)
    slot = lambda b, i: (0, b * nt + i)
    const2 = lambda b, i: (0, 0)

    def wspec(w):
        return pl.BlockSpec(w.shape, const2, pipeline_mode=pl.Buffered(1))

    weights = (w_og, w_a, w_o, w_r)
    w_bytes = sum(int(w.size) * w.dtype.itemsize for w in weights)
    vmem = w_bytes + 2 * tm * (d * 4 + 3 * d * 2 + d * 4 + d * 4) + 15 * tm * d * 4
    return pl.pallas_call(
        _mixout_body,
        grid=(bsz, length // tm),
        in_specs=[pl.BlockSpec((1, tm, d), tok), pl.BlockSpec((1, tm, d), tok), pl.BlockSpec((1, tm, d), tok),
                  pl.BlockSpec((1, n_pos, tm // n_pos, d), lambda b, i: (b, 0, i, 0)),
                  pl.BlockSpec((1, 6, d), lambda b, i: (b, 0, 0)),
                  pl.BlockSpec((1, d), const2), pl.BlockSpec((1, d), const2), pl.BlockSpec((1, d), const2),
                  wspec(w_og), wspec(w_a),
                  pl.BlockSpec((1, 2 * d), const2), wspec(w_o), wspec(w_r), pl.BlockSpec((1, n_e), const2)],
        out_specs=[pl.BlockSpec((1, tm, d), tok), pl.BlockSpec((1, tm, d), tok),
                   pl.BlockSpec((SUBLANES, tm), slot), pl.BlockSpec((1, tm, 8), tok), pl.BlockSpec((SUBLANES, tm), slot),
                   pl.BlockSpec((1, n_e), const2)],
        out_shape=[jax.ShapeDtypeStruct((bsz, length, d), F32),
                   jax.ShapeDtypeStruct((bsz, length, d), F32),
                   jax.ShapeDtypeStruct((SUBLANES, bsz * length), jnp.int32),
                   jax.ShapeDtypeStruct((bsz, length, 8), F32),
                   jax.ShapeDtypeStruct((SUBLANES, bsz * length), jnp.int32),
                   jax.ShapeDtypeStruct((1, n_e), jnp.int32)],
        scratch_shapes=[pltpu.VMEM((1, n_e), F32), pltpu.VMEM((d // LANES, tm, LANES), F32)],
        compiler_params=_cparams(2, vmem),
        name="mixout",
    )(x, h_f, h_r, y_b, mod, g1.reshape(1, d), g2.reshape(1, d), g_mh.reshape(1, d), w_og, w_a,
      b_gate.reshape(1, 2 * d), w_o, w_r, b_r.reshape(1, n_e))


def _rowpos_body(rs_ref, ti_ref, rk_ref, pos_ref):
    ti = ti_ref[...]
    start = jnp.zeros(ti.shape, jnp.int32)
    for e in range(N_EXPERTS):
        start = jnp.where(ti == e, rs_ref[e], start)
    pos_ref[...] = start + rk_ref[...]


def _rowpos(row_start, top_i, rank, tn):
    w, n = top_i.shape
    grid_spec = pltpu.PrefetchScalarGridSpec(
        num_scalar_prefetch=1,
        grid=(n // tn,),
        in_specs=[pl.BlockSpec((w, tn), lambda i, rs: (0, i)), pl.BlockSpec((w, tn), lambda i, rs: (0, i))],
        out_specs=pl.BlockSpec((w, tn), lambda i, rs: (0, i)),
    )
    return pl.pallas_call(
        _rowpos_body,
        grid_spec=grid_spec,
        out_shape=jax.ShapeDtypeStruct((w, n), jnp.int32),
        compiler_params=_cparams(1, 16 * w * tn * 4),
        name="rowpos",
    )(row_start, top_i, rank)


def _slot_specs(tm, n_blocks, block_of):
    return [pl.BlockSpec((tm,), functools.partial(lambda i, k: (k * n_blocks + block_of(i),), k=k),
                         memory_space=pltpu.SMEM) for k in range(TOP_K)]


def _dispatch_body(*refs):
    pos_refs, (h_ref, xs_hbm, sem) = refs[:TOP_K], refs[TOP_K:]
    tm = h_ref.shape[0]

    def issue(i, carry):
        base = pl.multiple_of(i * SUBLANES, SUBLANES)
        for s in range(SUBLANES):
            for k in range(TOP_K):
                row = pos_refs[k][base + s]
                pltpu.make_async_copy(h_ref.at[pl.ds(base + s, 1), :], xs_hbm.at[row],
                                      sem).start(priority=k % 2)
        return carry

    lax.fori_loop(0, tm // SUBLANES, issue, 0)
    for k in range(TOP_K):
        pltpu.make_async_copy(h_ref, h_ref, sem).wait()


def _dispatch(pos_flat, h_rows, n_rows, tm):
    n, w = h_rows.shape
    return pl.pallas_call(
        _dispatch_body,
        grid=(n // tm,),
        in_specs=_slot_specs(tm, n // tm, lambda i: i) + [pl.BlockSpec((tm, w), lambda i: (i, 0))],
        out_specs=pl.BlockSpec(memory_space=pl.ANY),
        out_shape=jax.ShapeDtypeStruct((n_rows, 1, w), h_rows.dtype),
        scratch_shapes=[pltpu.SemaphoreType.DMA(())],
        compiler_params=_cparams(1, 8 * tm * w * 4),
        name="dispatch",
    )(*([pos_flat] * TOP_K), h_rows)


def _experts_body(te_ref, tv_ref, nt_ref, xs_ref, win_ref, bin_ref, wout_ref, bout_ref, y_ref, win_bf, wout_bf,
                  *, f_chunk, cast_rows):
    i = pl.program_id(0)
    e = te_ref[i]
    e_prev = te_ref[jnp.maximum(i - 1, 0)]
    d, f2 = win_bf.shape
    f = f2 // 2

    @pl.when((i == 0) | (e != e_prev))
    def _():
        def cast_in(r, carry):
            rows = pl.ds(pl.multiple_of(r * cast_rows, cast_rows), cast_rows)
            win_bf[rows, :] = win_ref[0, rows, :].astype(BF16)
            return carry

        def cast_out(r, carry):
            rows = pl.ds(pl.multiple_of(r * cast_rows, cast_rows), cast_rows)
            wout_bf[rows, :] = wout_ref[0, rows, :].astype(BF16)
            return carry

        lax.fori_loop(0, d // cast_rows, cast_in, 0)
        lax.fori_loop(0, f // cast_rows, cast_out, 0)

    @pl.when(i < nt_ref[0])
    def _():
        xs = xs_ref[:, 0, :]
        live = lax.broadcasted_iota(jnp.int32, xs.shape, 0) < tv_ref[i]
        x = jnp.where(live, xs, 0.0).astype(BF16)
        acc = None
        for c in range(f // f_chunk):
            lo = c * f_chunk
            zg = jnp.dot(x, win_bf[:, lo:lo + f_chunk], preferred_element_type=F32) + bin_ref[0, :, lo:lo + f_chunk]
            zl = (jnp.dot(x, win_bf[:, f + lo:f + lo + f_chunk], preferred_element_type=F32)
                  + bin_ref[0, :, f + lo:f + lo + f_chunk])
            glu = jnp.minimum(zg, SWIGLU_LIMIT)
            lin = jnp.clip(zl, -SWIGLU_LIMIT, SWIGLU_LIMIT)
            act = glu * jax.nn.sigmoid(SWIGLU_ALPHA * glu) * (lin + 1.0)
            part = jnp.dot(act.astype(BF16), wout_bf[lo:lo + f_chunk, :], preferred_element_type=F32)
            acc = part if acc is None else acc + part
        y_ref[:, 0, :] = acc + bout_ref[0]

    @pl.when(i >= nt_ref[0])
    def _():
        y_ref[...] = jnp.zeros(y_ref.shape, y_ref.dtype)


def _experts(tile_expert, tile_valid, n_tiles, xs, w_e_in, b_e_in, w_e_out, b_e_out, tm):
    rows, _, d = xs.shape
    n_e, _, f2 = w_e_in.shape
    f = f2 // 2
    nt_max = rows // tm
    row_map = lambda i, te, tv, nt: (jnp.minimum(i, nt[0] - 1), 0, 0)
    exp_map = lambda i, te, tv, nt: (te[i], 0, 0)
    vmem = 2 * (d * f2 + f * d) * 4 + (d * f2 + f * d) * 2 + 4 * tm * d * 2 + 10 * tm * d * 4
    grid_spec = pltpu.PrefetchScalarGridSpec(
        num_scalar_prefetch=3,
        grid=(nt_max,),
        in_specs=[pl.BlockSpec((tm, 1, d), row_map),
                  pl.BlockSpec((1, d, f2), exp_map), pl.BlockSpec((1, 1, f2), exp_map),
                  pl.BlockSpec((1, f, d), exp_map), pl.BlockSpec((1, 1, d), exp_map)],
        out_specs=pl.BlockSpec((tm, 1, d), lambda i, te, tv, nt: (i, 0, 0)),
        scratch_shapes=[pltpu.VMEM((d, f2), BF16), pltpu.VMEM((f, d), BF16)],
    )
    return pl.pallas_call(
        functools.partial(_experts_body, f_chunk=512, cast_rows=128),
        grid_spec=grid_spec,
        out_shape=jax.ShapeDtypeStruct((rows, 1, d), F32),
        compiler_params=_cparams(1, vmem),
        name="experts",
    )(tile_expert, tile_valid, n_tiles, xs, w_e_in, b_e_in.reshape(n_e, 1, f2), w_e_out, b_e_out.reshape(n_e, 1, d))


def _final_body(*refs):
    pos_refs, posn_refs = refs[:TOP_K], refs[TOP_K:2 * TOP_K]
    x1_ref, tp_ref, gt_ref, g_ref, y_hbm, o_ref, buf, sem = refs[2 * TOP_K:]
    i = pl.program_id(0)
    n = pl.num_programs(0)
    tm = x1_ref.shape[0]

    def gather(p_refs, slot):
        def issue(i, carry):
            base = pl.multiple_of(i * SUBLANES, SUBLANES)
            for s in range(SUBLANES):
                for k in range(TOP_K):
                    row = p_refs[k][base + s]
                    pltpu.make_async_copy(y_hbm.at[row], buf.at[slot, k, pl.ds(base + s, 1), :],
                                          sem.at[slot]).start(priority=k % 2)
            return carry

        lax.fori_loop(0, tm // SUBLANES, issue, 0)

    @pl.when(i == 0)
    def _():
        gather(pos_refs, 0)

    @pl.when(i + 1 < n)
    def _():
        gather(posn_refs, (i + 1) % 2)

    slot = i % 2
    for k in range(TOP_K):
        pltpu.make_async_copy(buf.at[slot, k], buf.at[slot, k], sem.at[slot]).wait()
    moe = None
    for k in range(TOP_K):
        term = tp_ref[:, k:k + 1] * buf[slot, k]
        moe = term if moe is None else moe + term
    x2 = x1_ref[...] + gt_ref[0] * moe
    o_ref[...] = x2 * lax.rsqrt(jnp.mean(x2 * x2, axis=-1, keepdims=True) + EPS) * g_ref[...]


def _final(pos_flat, x1, top_p, gt2, g_final, y_rows, tm):
    n_tok, d = x1.shape
    bsz = gt2.shape[0]
    n_blk = n_tok // tm
    per_b = n_blk // bsz
    return pl.pallas_call(
        _final_body,
        grid=(n_blk,),
        in_specs=_slot_specs(tm, n_blk, lambda i: i) + _slot_specs(tm, n_blk, lambda i: jnp.minimum(i + 1, n_blk - 1)) + [
                  pl.BlockSpec((tm, d), lambda i: (i, 0)),
                  pl.BlockSpec((tm, top_p.shape[-1]), lambda i: (i, 0)),
                  pl.BlockSpec((1, 1, d), lambda i: (i // per_b, 0, 0)),
                  pl.BlockSpec((1, d), lambda i: (0, 0)),
                  pl.BlockSpec(memory_space=pl.ANY)],
        out_specs=pl.BlockSpec((tm, d), lambda i: (i, 0)),
        out_shape=jax.ShapeDtypeStruct((n_tok, d), F32),
        scratch_shapes=[pltpu.VMEM((2, TOP_K, tm, d), F32), pltpu.SemaphoreType.DMA((2,))],
        compiler_params=_cparams(1, 2 * TOP_K * tm * d * 4 + 4 * tm * d * 4 + 8 * tm * d * 4),
        name="final",
    )(*([pos_flat] * (2 * TOP_K)), x1, top_p, gt2, g_final.reshape(1, d), y_rows)


def _tile_table(counts, tm, nt_max):
    tiles = (counts + tm - 1) // tm
    tile_end = jnp.cumsum(tiles)
    tile_start = tile_end - tiles
    n_tiles = tile_end[-1]
    tile_ids = jnp.arange(nt_max, dtype=jnp.int32)
    last_e = jnp.sum((n_tiles - 1) >= tile_end).astype(jnp.int32)
    te = jnp.sum(tile_ids[:, None] >= tile_end[None, :], axis=1).astype(jnp.int32)
    te = jnp.where(tile_ids < n_tiles, te, last_e)
    sel = te[:, None] == jnp.arange(counts.shape[0], dtype=jnp.int32)[None, :]
    cnt_t = jnp.sum(jnp.where(sel, counts[None, :], 0), axis=1)
    start_t = jnp.sum(jnp.where(sel, tile_start[None, :], 0), axis=1)
    live = jnp.clip(cnt_t - (tile_ids - start_t) * tm, 0, tm).astype(jnp.int32)
    live = jnp.where(tile_ids < n_tiles, live, 0)
    return (tile_start * tm).astype(jnp.int32), te, live, n_tiles.reshape(1).astype(jnp.int32)


def kernel(x, c, ctx, c_ctx, w_ada, b_ada, g_norm1, g_norm2, w_in, w_conv_qk, b_ifgate, g_mh, w_branch_m,
           s5_a_re, s5_a_im, s5_log_dt, s5_b_re, s5_b_im, s5_c_re, s5_c_im, s5_d, w_glu, b_glu, w_branch_s,
           b_merge_gate, w_o, w_router, b_router, w_e_in, b_e_in, w_e_out, b_e_out, g_final):
    bsz, length, d = x.shape
    l_ctx = ctx.shape[1]
    n_qk = 2 * N_HEADS * D_QK
    n_v = N_HEADS * D_V
    n_if = 4 * N_HEADS
    n_u = S5_GROUPS * S5_GROUP
    off_if = n_qk + n_v
    off_u = off_if + n_if
    off_o = off_u + n_u
    layer = 0

    pad_rows = -(bsz + 1) % 8
    c_rows = jnp.concatenate([c, c_ctx[None, :], jnp.zeros((pad_rows, d), F32)], axis=0)
    mod_all = _ada(c_rows, w_ada[layer], b_ada[layer])
    mod = mod_all[:bsz].reshape(bsz, 6, d)
    mod_c = mod_all[bsz, :2 * d].reshape(2, 1, 1, d)

    w_l = w_in[layer]
    w_state = jnp.concatenate([w_l[:, :off_if], w_l[:, off_u:off_o]], axis=1).astype(BF16)
    w_if_t = w_l[:, off_if:off_u].T.astype(BF16)
    b_if = b_ifgate[layer].reshape(n_if)
    proj_c = _proj(ctx, g_norm1[layer], mod_c[1], mod_c[0], w_state, w_if_t, b_if, n_qk, n_v, n_u, l_ctx)
    proj_l = _proj(x, g_norm1[layer], mod[:, 1:2], mod[:, 0:1], w_state, w_if_t, b_if, n_qk, n_v, n_u, 512)
    qk_c, v_c, u_c, gif_c = proj_c
    qk_l, v_l, u_l, gif_l = proj_l

    w9 = w_conv_qk[layer].reshape(9, n_qk)
    q_c, kt_c = _conv(qk_c, w9, D_QK ** -0.5, l_ctx, l_ctx)
    q_l, kt_l = _conv(qk_l, w9, D_QK ** -0.5, GRID_W, min(1024, length))

    st0 = jnp.zeros((bsz, 2 * N_HEADS, D_QK, 2 * D_V), F32)
    m0 = jnp.zeros((bsz, 2 * N_HEADS, 1), F32)
    _, _, st_c, m_c = _mlstm(q_c, kt_c, v_c, gif_c, st0, m0)
    h_f, h_r, _, _ = _mlstm(q_l, kt_l, v_l, gif_l, st_c, m_c)

    w_in_t, m_tz_t, c_out_t, mu_a, mu_b = _s5_tables(
        s5_a_re[layer], s5_a_im[layer], s5_log_dt[layer], s5_b_re[layer], s5_b_im[layer],
        s5_c_re[layer], s5_c_im[layer], s5_d[layer])
    hs0 = jnp.zeros((bsz, S5_GROUPS, 4 * S5_STATE), F32)
    hs_c, = _s5(u_c, hs0, w_in_t, m_tz_t, c_out_t, mu_a, mu_b)
    y_b, _ = _s5(u_l, hs_c, w_in_t, m_tz_t, c_out_t, mu_a, mu_b,
                 w_glu[layer].T.astype(BF16), b_glu[layer], w_branch_s[layer].T.astype(BF16))

    w_og = w_l[:, off_o:].astype(BF16)
    x1, h2, top_i, top_p, rank, counts = _mixout(
        x, h_f, h_r, y_b, mod, g_norm1[layer], g_norm2[layer], g_mh[layer], w_og, w_branch_m[layer].astype(BF16),
        b_merge_gate[layer], w_o[layer].astype(BF16), w_router[layer].astype(BF16), b_router[layer], 512)

    tm_e = 512
    n_tok = bsz * length
    nt_max = n_tok * TOP_K // tm_e + N_EXPERTS
    row_start, tile_expert, tile_live, n_tiles = _tile_table(counts.reshape(N_EXPERTS), tm_e, nt_max)
    pos = _rowpos(row_start, top_i, rank, min(8192, n_tok))
    pos_flat = pos[:TOP_K].reshape(TOP_K * n_tok)
    xs = _dispatch(pos_flat, h2.reshape(n_tok, d), nt_max * tm_e, min(1024, n_tok))
    y_rows = _experts(tile_expert, tile_live, n_tiles, xs, w_e_in[layer], b_e_in[layer], w_e_out[layer],
                      b_e_out[layer], tm_e)
    out = _final(pos_flat, x1.reshape(n_tok, d), top_p.reshape(n_tok, 8), mod[:, 5:6], g_final, y_rows, 512)
    return out.reshape(bsz, length, d)
```

```python
import functools

import jax
import jax.numpy as jnp
from jax import lax
from jax.experimental import pallas as pl
from jax.experimental.pallas import tpu as pltpu

F32 = jnp.float32
BF16 = jnp.bfloat16
EPS = 1e-6

N_HEADS = 8
D_QK = 64
D_V = 128
M_CHUNK = 256
GRID_W = 64
S5_GROUPS = 32
S5_GROUP = 16
S5_STATE = 64
S5_CHUNK = 16
N_EXPERTS = 32
TOP_K = 4
SWIGLU_LIMIT = 7.0
SWIGLU_ALPHA = 1.702

LANES = 128
SUBLANES = 8
TM_PROJ = 512
TM_CONV = 1024
TM_MIX = 512
TM_EXPERT = 512
TM_DISPATCH = 1024
TM_FINAL = 512
TN_ROWPOS = 8192
F_CHUNK = 512
CAST_ROWS = 128

V7X_VMEM_BYTES = 64 * 1024 * 1024
_VMEM_CAP = V7X_VMEM_BYTES - 8 * 1024 * 1024


def _cparams(n_axes, vmem_bytes):
    limit = int(min(_VMEM_CAP, max(32 * 1024 * 1024, vmem_bytes)))
    return pltpu.CompilerParams(dimension_semantics=("arbitrary",) * n_axes, vmem_limit_bytes=limit)


def _silu(x):
    return x * jax.nn.sigmoid(x)


def _norm_mod(x, g, scale, shift):
    ms = jnp.mean(x * x, axis=-1, keepdims=True)
    return (x * lax.rsqrt(ms + EPS) * g) * (1.0 + scale) + shift


def _split3(x):
    hi = x.astype(BF16)
    r1 = x - hi.astype(F32)
    mid = r1.astype(BF16)
    lo = (r1 - mid.astype(F32)).astype(BF16)
    return hi, mid, lo


def _dot_nt(a, b):
    return lax.dot_general(a, b, (((1,), (1,)), ((), ())), preferred_element_type=F32)


def _dot_tn(a, b):
    return lax.dot_general(a, b, (((0,), (0,)), ((), ())), preferred_element_type=F32)


def _ada_body(c_ref, w_ref, b_ref, o_ref):
    s = _silu(c_ref[...])
    o_ref[...] = jnp.dot(s, w_ref[...], preferred_element_type=F32,
                         precision=lax.Precision.HIGHEST) + b_ref[...]


def _ada(c_rows, w_ada, b_ada):
    rows, d = c_rows.shape
    n = w_ada.shape[1]
    tn = 1024
    return pl.pallas_call(
        _ada_body,
        grid=(n // tn,),
        in_specs=[pl.BlockSpec((rows, d), lambda j: (0, 0)),
                  pl.BlockSpec((d, tn), lambda j: (0, j)),
                  pl.BlockSpec((1, tn), lambda j: (0, j))],
        out_specs=pl.BlockSpec((rows, tn), lambda j: (0, j)),
        out_shape=jax.ShapeDtypeStruct((rows, n), F32),
        compiler_params=_cparams(1, 4 * d * tn * 4),
        name="ada",
    )(c_rows, w_ada, b_ada.reshape(1, n))


def _proj_body(x_ref, g_ref, sc_ref, sh_ref, w_ref, wif_ref, bif_ref, qk_ref, v_ref, u_ref, gif_ref):
    h = _norm_mod(x_ref[0], g_ref[...], sc_ref[0], sh_ref[0]).astype(BF16)
    n_qk = qk_ref.shape[-1]
    n_v = v_ref.shape[-1]
    qk_ref[0] = jnp.dot(h, w_ref[:, :n_qk], preferred_element_type=F32).astype(BF16)
    v_ref[0] = jnp.dot(h, w_ref[:, n_qk:n_qk + n_v], preferred_element_type=F32).astype(BF16)
    u = jnp.dot(h, w_ref[:, n_qk + n_v:], preferred_element_type=F32)
    for j in range(u_ref.shape[1]):
        u_ref[0, j] = u[:, j * LANES:(j + 1) * LANES]
    gif_ref[0] = _dot_nt(wif_ref[...], h) + bif_ref[...]


def _proj(x, g, scale, shift, w_state, w_if_t, b_if, n_qk, n_v, n_u, tm):
    bsz, length, d = x.shape
    n_if = b_if.shape[-1]
    per_batch = scale.shape[0] == bsz
    mod_map = (lambda b, i: (b, 0, 0)) if per_batch else (lambda b, i: (0, 0, 0))
    cols = w_state.shape[1]
    vmem = 2 * (tm * d * 4 + d * cols * 2 + tm * (n_qk + n_v + n_u) * 2 + tm * 128 * 4) + 6 * tm * d * 4
    return pl.pallas_call(
        _proj_body,
        grid=(bsz, length // tm),
        in_specs=[pl.BlockSpec((1, tm, d), lambda b, i: (b, i, 0)),
                  pl.BlockSpec((1, d), lambda b, i: (0, 0)),
                  pl.BlockSpec((1, 1, d), mod_map),
                  pl.BlockSpec((1, 1, d), mod_map),
                  pl.BlockSpec((d, cols), lambda b, i: (0, 0)),
                  pl.BlockSpec((n_if, d), lambda b, i: (0, 0)),
                  pl.BlockSpec((n_if, 1), lambda b, i: (0, 0))],
        out_specs=[pl.BlockSpec((1, tm, n_qk), lambda b, i: (b, i, 0)),
                   pl.BlockSpec((1, tm, n_v), lambda b, i: (b, i, 0)),
                   pl.BlockSpec((1, n_u // LANES, tm, LANES), lambda b, i: (b, 0, i, 0)),
                   pl.BlockSpec((1, n_if, tm), lambda b, i: (b, 0, i))],
        out_shape=[jax.ShapeDtypeStruct((bsz, length, n_qk), BF16),
                   jax.ShapeDtypeStruct((bsz, length, n_v), BF16),
                   jax.ShapeDtypeStruct((bsz, n_u // LANES, length, LANES), F32),
                   jax.ShapeDtypeStruct((bsz, n_if, length), F32)],
        compiler_params=_cparams(2, vmem),
        name="proj",
    )(x, g.reshape(1, d), scale, shift, w_state, w_if_t, b_if.reshape(n_if, 1))


def _conv_body(main_ref, prev_ref, next_ref, w_ref, q_ref, kt_ref, *, width, q_scale):
    i = pl.program_id(1)
    last = pl.num_programs(1) - 1
    t = main_ref.shape[1]
    n = t + 2 * width
    main = main_ref[0].astype(F32)
    prev = jnp.where(i > 0, prev_ref[0].astype(F32), 0.0)
    nxt = jnp.where(i < last, next_ref[0].astype(F32), 0.0)
    ext = jnp.concatenate([prev, main, nxt], axis=0)
    col = lax.broadcasted_iota(jnp.int32, (t, 1), 0) % width
    acc = None
    for dx in (-1, 0, 1):
        shifted = ext if dx == 0 else pltpu.roll(ext, (-dx) % n, axis=0)
        part = None
        for dy in (-1, 0, 1):
            tap = w_ref[(dy + 1) * 3 + (dx + 1):(dy + 1) * 3 + (dx + 1) + 1, :]
            term = tap * shifted[width + dy * width:width + dy * width + t]
            part = term if part is None else part + term
        if dx == -1:
            part = jnp.where(col == 0, 0.0, part)
        elif dx == 1:
            part = jnp.where(col == width - 1, 0.0, part)
        acc = part if acc is None else acc + part
    y = _silu(acc)

    @pl.when(pl.program_id(2) == 0)
    def _():
        q_ref[0] = (y * q_scale).astype(q_ref.dtype)

    @pl.when(pl.program_id(2) == 1)
    def _():
        kt_ref[0] = y.T.astype(kt_ref.dtype)


def _conv(qk_pre, w9, q_scale, width, t_block):
    bsz, length, ch2 = qk_pre.shape
    ch = ch2 // 2
    rpb = t_block // width
    n_rows = length // width
    vmem = 4 * (t_block + 2 * width) * ch * 2 + 14 * (t_block + 2 * width) * ch * 4
    return pl.pallas_call(
        functools.partial(_conv_body, width=width, q_scale=q_scale),
        grid=(bsz, length // t_block, 2),
        in_specs=[pl.BlockSpec((1, t_block, ch), lambda b, i, c: (b, i, c)),
                  pl.BlockSpec((1, width, ch), lambda b, i, c: (b, jnp.maximum(i * rpb - 1, 0), c)),
                  pl.BlockSpec((1, width, ch), lambda b, i, c: (b, jnp.minimum((i + 1) * rpb, n_rows - 1), c)),
                  pl.BlockSpec((9, ch), lambda b, i, c: (0, c))],
        out_specs=[pl.BlockSpec((1, t_block, ch), lambda b, i, c: (b, i, 0)),
                   pl.BlockSpec((1, ch, t_block), lambda b, i, c: (b, 0, i))],
        out_shape=[jax.ShapeDtypeStruct((bsz, length, ch), BF16),
                   jax.ShapeDtypeStruct((bsz, ch, length), BF16)],
        compiler_params=_cparams(3, vmem),
        name="conv",
    )(qk_pre, qk_pre, qk_pre, w9)


def _log_sigmoid(x):
    return jnp.minimum(x, 0.0) - jnp.log1p(jnp.exp(-jnp.abs(x)))


def _cumsum_lanes_exact(x, reverse):
    t = x.shape[-1]
    r_idx = lax.broadcasted_iota(jnp.int32, (t, t), 0)
    c_idx = lax.broadcasted_iota(jnp.int32, (t, t), 1)
    u01 = jnp.where((r_idx >= c_idx) if reverse else (r_idx <= c_idx), 1.0, 0.0).astype(BF16)
    out = None
    for piece in _split3(x):
        p = jnp.dot(piece, u01, preferred_element_type=F32)
        out = p if out is None else out + p
    return out


def _cummax_lanes(x, reverse):
    t = x.shape[-1]
    lane = lax.broadcasted_iota(jnp.int32, x.shape, 1)
    sh = 1
    while sh < t:
        if reverse:
            cand = jnp.where(lane < t - sh, pltpu.roll(x, t - sh, axis=1), -jnp.inf)
        else:
            cand = jnp.where(lane >= sh, pltpu.roll(x, sh, axis=1), -jnp.inf)
        x = jnp.maximum(x, cand)
        sh *= 2
    return x


def _mlstm_gate_rows(g_ref, m_col, d):
    t = g_ref.shape[-1]
    base = d * 2 * N_HEADS
    li = g_ref[0, base:base + N_HEADS, :]
    lf = _log_sigmoid(g_ref[0, base + N_HEADS:base + 2 * N_HEADS, :])
    b = _cumsum_lanes_exact(lf, reverse=(d == 1))
    g = li - b
    a = jnp.maximum(_cummax_lanes(g, reverse=(d == 1)), m_col)
    end = 0 if d == 1 else t - 1
    a_end = a[:, end:end + 1]
    return dict(g=g, a=a, ie=jnp.exp(m_col - a), emt=jnp.exp(-b - a), we=jnp.exp(g - a_end),
                dec=jnp.exp(m_col - a_end), m_new=b[:, end:end + 1] + a_end)


def _mlstm_body(qf_ref, kf_ref, vf_ref, gf_ref, qr_ref, kr_ref, vr_ref, gr_ref, st0_ref, m0_ref,
                hf_ref, hr_ref, st_ref, m_ref):
    @pl.when(pl.program_id(1) == 0)
    def _():
        st_ref[...] = st0_ref[...]
        m_ref[...] = m0_ref[...]

    t = qf_ref.shape[1]
    m_all = m_ref[0]
    rows = [_mlstm_gate_rows(g_ref, m_all[d * N_HEADS:(d + 1) * N_HEADS], d) for d, g_ref in ((0, gf_ref), (1, gr_ref))]
    both = lambda name: jnp.concatenate([rows[0][name], rows[1][name]], axis=0)
    n_hd = 2 * N_HEADS
    pad = jnp.zeros((128 - 3 * n_hd, t), F32)
    cols = jnp.concatenate([both('a'), both('ie'), both('emt'), pad], axis=0).T
    m_ref[0] = both('m_new')
    r_idx = lax.broadcasted_iota(jnp.int32, (t, t), 0)
    c_idx = lax.broadcasted_iota(jnp.int32, (t, t), 1)
    ones_blk = jnp.ones((t, D_V), BF16)
    for d, (q_ref, kt_ref, v_ref, h_ref) in enumerate(((qf_ref, kf_ref, vf_ref, hf_ref), (qr_ref, kr_ref, vr_ref, hr_ref))):
        causal = (c_idx >= r_idx) if d == 1 else (c_idx <= r_idx)
        for h in range(N_HEADS):
            j = d * N_HEADS + h
            q = q_ref[0, :, h * D_QK:(h + 1) * D_QK]
            kt = kt_ref[0, h * D_QK:(h + 1) * D_QK, :]
            v1 = jnp.concatenate([v_ref[0, :, h * D_V:(h + 1) * D_V], ones_blk], axis=1)
            state = st_ref[0, j]
            a_col = cols[:, j:j + 1]
            ie_col = cols[:, n_hd + j:n_hd + j + 1]
            emt_col = cols[:, 2 * n_hd + j:2 * n_hd + j + 1]
            dmat = jnp.exp(jnp.where(causal, rows[d]['g'][h:h + 1, :] - a_col, -jnp.inf))
            s = (jnp.dot(q, kt, preferred_element_type=F32) * dmat).astype(BF16)
            z = jnp.dot(q, state.astype(BF16), preferred_element_type=F32)
            p = jnp.dot(s, v1, preferred_element_type=F32)
            num = ie_col * z[:, :D_V] + p[:, :D_V]
            den = ie_col * z[:, D_V:] + p[:, D_V:]
            h_ref[0, :, h * D_V:(h + 1) * D_V] = (num / jnp.maximum(jnp.abs(den), emt_col)).astype(h_ref.dtype)
            kw = (kt.astype(F32) * rows[d]['we'][h:h + 1, :]).astype(BF16)
            st_ref[0, j] = rows[d]['dec'][h:h + 1, :] * state + jnp.dot(kw, v1, preferred_element_type=F32)


def _mlstm(q, k_t, v, gif_t, st0, m0):
    bsz, length, hv = v.shape
    t = M_CHUNK
    nc = length // t
    hq = N_HEADS * D_QK
    ng = gif_t.shape[1]
    fwd = lambda b, i: (b, i, 0)
    rev = lambda b, i: (b, nc - 1 - i, 0)
    fwd_t = lambda b, i: (b, 0, i)
    rev_t = lambda b, i: (b, 0, nc - 1 - i)
    st_spec = pl.BlockSpec((1,) + st0.shape[1:], lambda b, i: (b, 0, 0, 0))
    m_spec = pl.BlockSpec((1,) + m0.shape[1:], lambda b, i: (b, 0, 0))
    vmem = 24 * 1024 * 1024
    return pl.pallas_call(
        _mlstm_body,
        grid=(bsz, nc),
        in_specs=[pl.BlockSpec((1, t, hq), fwd), pl.BlockSpec((1, hq, t), fwd_t),
                  pl.BlockSpec((1, t, hv), fwd), pl.BlockSpec((1, ng, t), fwd_t),
                  pl.BlockSpec((1, t, hq), rev), pl.BlockSpec((1, hq, t), rev_t),
                  pl.BlockSpec((1, t, hv), rev), pl.BlockSpec((1, ng, t), rev_t),
                  st_spec, m_spec],
        out_specs=[pl.BlockSpec((1, t, hv), fwd), pl.BlockSpec((1, t, hv), rev), st_spec, m_spec],
        out_shape=[jax.ShapeDtypeStruct((bsz, length, hv), BF16),
                   jax.ShapeDtypeStruct((bsz, length, hv), BF16),
                   jax.ShapeDtypeStruct(st0.shape, F32),
                   jax.ShapeDtypeStruct(m0.shape, F32)],
        compiler_params=_cparams(2, vmem),
        name="mlstm",
    )(q, k_t, v, gif_t, q, k_t, v, gif_t, st0, m0)


def _s5_tables(a_re, a_im, log_dt, b_re, b_im, c_re, c_im, d_skip):
    hp = lax.Precision.HIGHEST
    t = S5_CHUNK
    n_dir, groups, p = a_re.shape
    cg = b_re.shape[-1]
    dt = jnp.exp(log_dt)[..., None]

    def lam_pow(n):
        mag = jnp.exp(n * (dt * a_re)[..., None])
        ang = n * (dt * a_im)[..., None]
        return mag * jnp.cos(ang), mag * jnp.sin(ang)

    ab_re, ab_im = (z[..., 0] for z in lam_pow(jnp.ones((1,), F32)))
    den = a_re * a_re + a_im * a_im
    xr = ab_re - 1.0
    cf_re = (xr * a_re + ab_im * a_im) / den
    cf_im = (ab_im * a_re - xr * a_im) / den
    bb_re = cf_re[..., None] * b_re - cf_im[..., None] * b_im
    bb_im = cf_re[..., None] * b_im + cf_im[..., None] * b_re
    jj = jnp.arange(t + 1, dtype=F32)
    lp_re, lp_im = lam_pow(jj)

    def w_dir(d, exps):
        lr = lp_re[d][:, :, exps]
        li = lp_im[d][:, :, exps]
        wr = lr[..., None] * bb_re[d][:, :, None, :] - li[..., None] * bb_im[d][:, :, None, :]
        wi = lr[..., None] * bb_im[d][:, :, None, :] + li[..., None] * bb_re[d][:, :, None, :]
        to_rows = lambda w: jnp.transpose(w, (0, 2, 3, 1)).reshape(groups, t * cg, p)
        return to_rows(wr), to_rows(wi)

    s_idx = jnp.arange(t)
    wf_re, wf_im = w_dir(0, t - 1 - s_idx)
    wr_re, wr_im = w_dir(1, s_idx)
    zw = jnp.zeros_like(wf_re)
    w_in = jnp.concatenate([jnp.concatenate([wf_re, zw, wf_im, zw], axis=-1),
                            jnp.concatenate([zw, wr_re, zw, wr_im], axis=-1)], axis=1)

    def c_dir(d, exps):
        lr = lp_re[d][:, :, exps]
        li = lp_im[d][:, :, exps]
        cr = jnp.transpose(c_re[d], (0, 2, 1))
        ci = jnp.transpose(c_im[d], (0, 2, 1))
        o_re = cr[:, :, None, :] * lr[..., None] - ci[:, :, None, :] * li[..., None]
        o_im = cr[:, :, None, :] * li[..., None] + ci[:, :, None, :] * lr[..., None]
        return o_re.reshape(groups, p, t * cg), (-o_im).reshape(groups, p, t * cg)

    cf_r, cf_i = c_dir(0, s_idx + 1)
    cr_r, cr_i = c_dir(1, t - s_idx)
    c_out = jnp.concatenate([cf_r, cr_r, cf_i, cr_i], axis=1)

    def k_dir(d):
        lr = lp_re[d][:, :, :t]
        li = lp_im[d][:, :, :t]
        clr = c_re[d][:, :, :, None] * lr[:, None] - c_im[d][:, :, :, None] * li[:, None]
        cli = c_re[d][:, :, :, None] * li[:, None] + c_im[d][:, :, :, None] * lr[:, None]
        return (jnp.einsum('gqpj,gpc->gjqc', clr, bb_re[d], precision=hp)
                - jnp.einsum('gqpj,gpc->gjqc', cli, bb_im[d], precision=hp))

    kf = k_dir(0)
    kr = k_dir(1)
    lag = s_idx[None, :] - s_idx[:, None]
    sel = jnp.concatenate([lag[None] == s_idx[:, None, None], -lag[None] == s_idx[:, None, None]], axis=0).astype(F32)
    resp_t = jnp.einsum('jst,gjqc->gtqsc', sel, jnp.concatenate([kf, kr], axis=1), precision=hp)
    skip = (jnp.eye(t, dtype=F32)[None, :, None, :, None] * jnp.eye(cg, dtype=F32)[None, None, :, None, :]
            * d_skip.reshape(groups, 1, cg, 1, 1))
    m_tz_t = (resp_t + skip).reshape(groups, t * cg, t * cg)

    n_pow = 8
    kk = (t * (2 ** jnp.arange(n_pow))).astype(F32)
    mp_re, mp_im = lam_pow(kk)
    mp_re = jnp.transpose(mp_re, (0, 1, 3, 2))
    mp_im = jnp.transpose(mp_im, (0, 1, 3, 2))
    mu_re = jnp.concatenate([mp_re[0], mp_re[1]], axis=-1)
    mu_im = jnp.concatenate([mp_im[0], mp_im[1]], axis=-1)
    tr = lambda m: jnp.transpose(m, (0, 2, 1)).astype(BF16)
    return w_in.astype(BF16), m_tz_t.astype(BF16), tr(c_out), mu_re, mu_im


def _pad_rows(x, rows):
    return x if x.shape[0] == rows else jnp.concatenate([x, jnp.zeros((rows - x.shape[0],) + x.shape[1:], x.dtype)], axis=0)


def _s5_body(*refs, nch, n_steps, with_output):
    if with_output:
        (u_ref, h0_ref, win_ref, mtzt_ref, coutt_ref, mure_ref, muim_ref, wglut_ref, bglu_ref, wbt_ref,
         yb_ref, hout_ref, xt_scr, yt_scr) = refs
    else:
        u_ref, h0_ref, win_ref, mure_ref, muim_ref, hout_ref, xt_scr = refs
    t_c, cg, groups = S5_CHUNK, S5_GROUP, S5_GROUPS
    p2 = 2 * S5_STATE
    ncp = xt_scr.shape[-1]
    for t in range(t_c):
        ut = jnp.concatenate([u_ref[0, j, pl.ds(t, nch, stride=t_c), :] for j in range(u_ref.shape[1])], axis=1)
        ut = _pad_rows(ut, ncp)
        xt_scr[:, t] = ut.T.astype(BF16).reshape(groups, cg, ncp)

    r_idx = lax.broadcasted_iota(jnp.int32, (ncp, ncp), 0)
    c_idx = lax.broadcasted_iota(jnp.int32, (ncp, ncp), 1)
    flip = jnp.where(r_idx + c_idx == nch - 1, 1.0, 0.0).astype(BF16)
    row = lax.broadcasted_iota(jnp.int32, (ncp, p2), 0)
    is_fwd = (lax.broadcasted_iota(jnp.int32, (ncp, 2 * p2), 1) % p2) < S5_STATE

    def group(g, carry):
        x = xt_scr[g].reshape(t_c * cg, ncp)
        x_rev = jnp.dot(x, flip, preferred_element_type=F32).astype(BF16)
        local = _dot_tn(jnp.concatenate([x, x_rev], axis=0), win_ref[g])
        l_re, l_im = local[:, :p2], local[:, p2:]
        h0 = h0_ref[0, pl.ds(g, 1), :]
        z_re = jnp.where(row == 0, h0[:, :p2], pltpu.roll(l_re, 1, axis=0))
        z_im = jnp.where(row == 0, h0[:, p2:], pltpu.roll(l_im, 1, axis=0))
        for k in range(n_steps):
            sft = 1 << k
            a_re = mure_ref[g, k:k + 1, :]
            a_im = muim_ref[g, k:k + 1, :]
            p_re = jnp.where(row >= sft, pltpu.roll(z_re, sft, axis=0), 0.0)
            p_im = jnp.where(row >= sft, pltpu.roll(z_im, sft, axis=0), 0.0)
            z_re, z_im = z_re + a_re * p_re - a_im * p_im, z_im + a_re * p_im + a_im * p_re
        if with_output:
            z = jnp.concatenate([z_re, z_im], axis=1).astype(BF16)
            z_flip = jnp.dot(flip, z, preferred_element_type=F32).astype(BF16)
            z_nat = jnp.where(is_fwd, z, z_flip)
            y = jnp.dot(mtzt_ref[g], x, preferred_element_type=F32) + _dot_nt(coutt_ref[g], z_nat)
            yt_scr[g] = jax.nn.gelu(y).astype(BF16).reshape(t_c, cg, ncp)
        a_re = mure_ref[g, 0:1, :]
        a_im = muim_ref[g, 0:1, :]
        e_re, e_im = z_re[nch - 1:nch], z_im[nch - 1:nch]
        hout_ref[0, pl.ds(g, 1), :] = jnp.concatenate(
            [a_re * e_re - a_im * e_im + l_re[nch - 1:nch], a_re * e_im + a_im * e_re + l_im[nch - 1:nch]], axis=1)
        return carry

    lax.fori_loop(0, groups, group, 0, unroll=8)

    if with_output:
        def tail(t, carry):
            ys = yt_scr[:, t].reshape(groups * cg, ncp)
            glu = jax.nn.sigmoid(jnp.dot(wglut_ref[...], ys, preferred_element_type=F32) + bglu_ref[...])
            ybt = jnp.dot(wbt_ref[...], (ys.astype(F32) * glu).astype(BF16), preferred_element_type=F32)
            yb_ref[0, t] = ybt.T[:nch].astype(yb_ref.dtype)
            return carry

        lax.fori_loop(0, t_c, tail, 0, unroll=4)


def _s5(u, h0, w_in_t, m_tz_t, c_out_t, mu_a, mu_b, w_glu_t=None, b_glu=None, w_b_t=None):
    bsz, n_blk, length, _ = u.shape
    width = n_blk * LANES
    with_output = w_glu_t is not None
    nch = length // S5_CHUNK
    ncp = -(-nch // 128) * 128
    p4 = 4 * S5_STATE
    n_steps = max(1, (nch - 1).bit_length())
    tc = S5_CHUNK * S5_GROUP
    one = pl.Buffered(1)
    full = lambda a: pl.BlockSpec(a.shape, lambda b: (0,) * a.ndim, pipeline_mode=one)
    in_specs = [pl.BlockSpec((1, n_blk, length, LANES), lambda b: (b, 0, 0, 0), pipeline_mode=one),
                pl.BlockSpec((1, S5_GROUPS, p4), lambda b: (b, 0, 0)), full(w_in_t)]
    args = [u, h0, w_in_t]
    scratch = [pltpu.VMEM((S5_GROUPS, S5_CHUNK, S5_GROUP, ncp), BF16)]
    out_specs = [pl.BlockSpec((1, S5_GROUPS, p4), lambda b: (b, 0, 0))]
    out_shape = [jax.ShapeDtypeStruct((bsz, S5_GROUPS, p4), F32)]
    vmem = length * width * 4 + 4 * S5_GROUPS * tc * tc * 2 + 2 * S5_GROUPS * tc * ncp * 2 + 8 * 1024 * 1024
    if with_output:
        d_out = w_b_t.shape[0]
        in_specs += [full(m_tz_t), full(c_out_t), full(mu_a), full(mu_b), full(w_glu_t),
                     pl.BlockSpec((width, 1), lambda b: (0, 0)), full(w_b_t)]
        args += [m_tz_t, c_out_t, mu_a, mu_b, w_glu_t, b_glu.reshape(width, 1), w_b_t]
        scratch.append(pltpu.VMEM((S5_GROUPS, S5_CHUNK, S5_GROUP, ncp), BF16))
        out_specs.insert(0, pl.BlockSpec((1, S5_CHUNK, nch, d_out), lambda b: (b, 0, 0, 0)))
        out_shape.insert(0, jax.ShapeDtypeStruct((bsz, S5_CHUNK, nch, d_out), BF16))
        vmem += 2 * S5_CHUNK * nch * d_out * 2 + (width * width + width * d_out) * 2
    else:
        in_specs += [full(mu_a), full(mu_b)]
        args += [mu_a, mu_b]
    return pl.pallas_call(
        functools.partial(_s5_body, nch=nch, n_steps=n_steps, with_output=with_output),
        grid=(bsz,),
        in_specs=in_specs,
        out_specs=out_specs,
        out_shape=out_shape,
        scratch_shapes=scratch,
        compiler_params=_cparams(1, vmem),
        name="s5",
    )(*args)


def _mixout_body(x_ref, hf_ref, hr_ref, yb_ref, mod_ref, g1_ref, g2_ref, gmh_ref, wog_ref, wa_ref,
                 bgate_ref, wo_ref, wr_ref, br_ref,
                 x1_ref, h2_ref, ti_ref, tp_ref, rk_ref, cnt_ref, base_ref, yb_scr):
    x = x_ref[0]
    d = x.shape[-1]
    mod = mod_ref[0]
    h = _norm_mod(x, g1_ref[...], mod[1:2], mod[0:1]).astype(BF16)
    og = jnp.dot(h, wog_ref[...], preferred_element_type=F32)
    hm = hf_ref[0].astype(F32) + hr_ref[0].astype(F32)
    heads = []
    for hd in range(N_HEADS):
        blk = hm[:, hd * D_V:(hd + 1) * D_V]
        heads.append(blk * lax.rsqrt(jnp.mean(blk * blk, axis=-1, keepdims=True) + EPS))
    hn = jnp.concatenate(heads, axis=1) * gmh_ref[...]
    y_a = jnp.dot((hn * jax.nn.sigmoid(og[:, :d])).astype(BF16), wa_ref[...], preferred_element_type=F32)
    n_pos = yb_ref.shape[1]
    for t in range(n_pos):
        blk = yb_ref[0, t].astype(F32)
        for j in range(yb_scr.shape[0]):
            yb_scr[j, pl.ds(t, yb_ref.shape[2], stride=n_pos), :] = blk[:, j * LANES:(j + 1) * LANES]
    y_b = jnp.concatenate([yb_scr[j] for j in range(yb_scr.shape[0])], axis=1)
    gates = jax.nn.sigmoid(og[:, d:] + bgate_ref[...])
    merged = gates[:, :d] * y_a + gates[:, d:] * y_b
    x1 = x + mod[2:3] * jnp.dot(merged.astype(BF16), wo_ref[...], preferred_element_type=F32)
    x1_ref[0] = x1
    h2f = _norm_mod(x1, g2_ref[...], mod[4:5], mod[3:4])
    h2_ref[0] = h2f
    logits = jnp.dot(h2f.astype(BF16), wr_ref[...], preferred_element_type=F32) + br_ref[...]
    tm, n_e = logits.shape
    e_iota = lax.broadcasted_iota(jnp.int32, (tm, n_e), 1)
    lane = lax.broadcasted_iota(jnp.int32, (tm, LANES), 1)
    n_slot = ti_ref.shape[0]
    ti = jnp.zeros(lane.shape, jnp.int32)
    tv = jnp.zeros(lane.shape, F32)
    top = None
    chosen = []
    for k in range(TOP_K):
        mx = jnp.max(logits, axis=-1, keepdims=True)
        idx = jnp.min(jnp.where(logits == mx, e_iota, n_e), axis=-1, keepdims=True)
        top = mx if top is None else top
        ti = jnp.where(lane == k, idx, ti)
        tv = jnp.where(lane == k, jnp.exp(mx - top), tv)
        chosen.append(e_iota == idx)
        logits = jnp.where(chosen[-1], -jnp.inf, logits)
    ti_ref[...] = ti.T[:n_slot]
    tp_ref[0] = (tv / jnp.sum(tv, axis=-1, keepdims=True))[:, :tp_ref.shape[-1]]

    @pl.when((pl.program_id(0) == 0) & (pl.program_id(1) == 0))
    def _():
        base_ref[...] = jnp.zeros(base_ref.shape, F32)

    onehot = jnp.zeros((tm, n_e), F32)
    for sel in chosen:
        onehot = onehot + jnp.where(sel, 1.0, 0.0)
    below = (lax.broadcasted_iota(jnp.int32, (tm, tm), 1) < lax.broadcasted_iota(jnp.int32, (tm, tm), 0))
    before = jnp.dot(jnp.where(below, 1.0, 0.0).astype(BF16), onehot.astype(BF16),
                     preferred_element_type=F32) + base_ref[...]
    rk = jnp.zeros(lane.shape, jnp.int32)
    for k, sel in enumerate(chosen):
        rank = jnp.sum(jnp.where(sel, before, 0.0), axis=-1, keepdims=True)
        rk = jnp.where(lane == k, rank.astype(jnp.int32), rk)
    rk_ref[...] = rk.T[:n_slot]
    base_ref[...] = base_ref[...] + jnp.sum(onehot, axis=0, keepdims=True)
    cnt_ref[...] = base_ref[...].astype(jnp.int32)


def _mixout(x, h_f, h_r, y_b, mod, g1, g2, g_mh, w_og, w_a, b_gate, w_o, w_r, b_r, tm):
    bsz, length, d = x.shape
    n_pos = y_b.shape[1]
    n_e = w_r.shape[-1]
    nt = length // tm
    tok = lambda b, i: (b, i, 0)
    slot = lambda b, i: (0, b * nt + i)
    const2 = lambda b, i: (0, 0)

    def wspec(w):
        return pl.BlockSpec(w.shape, const2, pipeline_mode=pl.Buffered(1))

    weights = (w_og, w_a, w_o, w_r)
    w_bytes = sum(int(w.size) * w.dtype.itemsize for w in weights)
    vmem = w_bytes + 2 * tm * (d * 4 + 3 * d * 2 + d * 4 + d * 4) + 15 * tm * d * 4
    return pl.pallas_call(
        _mixout_body,
        grid=(bsz, length // tm),
        in_specs=[pl.BlockSpec((1, tm, d), tok), pl.BlockSpec((1, tm, d), tok), pl.BlockSpec((1, tm, d), tok),
                  pl.BlockSpec((1, n_pos, tm // n_pos, d), lambda b, i: (b, 0, i, 0)),
                  pl.BlockSpec((1, 6, d), lambda b, i: (b, 0, 0)),
                  pl.BlockSpec((1, d), const2), pl.BlockSpec((1, d), const2), pl.BlockSpec((1, d), const2),
                  wspec(w_og), wspec(w_a),
                  pl.BlockSpec((1, 2 * d), const2), wspec(w_o), wspec(w_r), pl.BlockSpec((1, n_e), const2)],
        out_specs=[pl.BlockSpec((1, tm, d), tok), pl.BlockSpec((1, tm, d), tok),
                   pl.BlockSpec((SUBLANES, tm), slot), pl.BlockSpec((1, tm, 8), tok), pl.BlockSpec((SUBLANES, tm), slot),
                   pl.BlockSpec((1, n_e), const2)],
        out_shape=[jax.ShapeDtypeStruct((bsz, length, d), F32),
                   jax.ShapeDtypeStruct((bsz, length, d), F32),
                   jax.ShapeDtypeStruct((SUBLANES, bsz * length), jnp.int32),
                   jax.ShapeDtypeStruct((bsz, length, 8), F32),
                   jax.ShapeDtypeStruct((SUBLANES, bsz * length), jnp.int32),
                   jax.ShapeDtypeStruct((1, n_e), jnp.int32)],
        scratch_shapes=[pltpu.VMEM((1, n_e), F32), pltpu.VMEM((d // LANES, tm, LANES), F32)],
        compiler_params=_cparams(2, vmem),
        name="mixout",
    )(x, h_f, h_r, y_b, mod, g1.reshape(1, d), g2.reshape(1, d), g_mh.reshape(1, d), w_og, w_a,
      b_gate.reshape(1, 2 * d), w_o, w_r, b_r.reshape(1, n_e))


def _rowpos_body(rs_ref, ti_ref, rk_ref, pos_ref):
    ti = ti_ref[...]
    start = jnp.zeros(ti.shape, jnp.int32)
    for e in range(N_EXPERTS):
        start = jnp.where(ti == e, rs_ref[e], start)
    pos_ref[...] = start + rk_ref[...]


def _rowpos(row_start, top_i, rank, tn):
    w, n = top_i.shape
    grid_spec = pltpu.PrefetchScalarGridSpec(
        num_scalar_prefetch=1,
        grid=(n // tn,),
        in_specs=[pl.BlockSpec((w, tn), lambda i, rs: (0, i)), pl.BlockSpec((w, tn), lambda i, rs: (0, i))],
        out_specs=pl.BlockSpec((w, tn), lambda i, rs: (0, i)),
    )
    return pl.pallas_call(
        _rowpos_body,
        grid_spec=grid_spec,
        out_shape=jax.ShapeDtypeStruct((w, n), jnp.int32),
        compiler_params=_cparams(1, 16 * w * tn * 4),
        name="rowpos",
    )(row_start, top_i, rank)


def _slot_specs(tm, n_blocks, block_of):
    return [pl.BlockSpec((tm,), functools.partial(lambda i, k: (k * n_blocks + block_of(i),), k=k),
                         memory_space=pltpu.SMEM) for k in range(TOP_K)]


def _dispatch_body(*refs):
    pos_refs, (h_ref, xs_hbm, sem) = refs[:TOP_K], refs[TOP_K:]
    tm = h_ref.shape[0]

    def issue(i, carry):
        base = pl.multiple_of(i * SUBLANES, SUBLANES)
        for s in range(SUBLANES):
            for k in range(TOP_K):
                row = pos_refs[k][base + s]
                pltpu.make_async_copy(h_ref.at[pl.ds(base + s, 1), :], xs_hbm.at[row],
                                      sem).start(priority=k % 2)
        return carry

    lax.fori_loop(0, tm // SUBLANES, issue, 0)
    for k in range(TOP_K):
        pltpu.make_async_copy(h_ref, h_ref, sem).wait()


def _dispatch(pos_flat, h_rows, n_rows, tm):
    n, w = h_rows.shape
    return pl.pallas_call(
        _dispatch_body,
        grid=(n // tm,),
        in_specs=_slot_specs(tm, n // tm, lambda i: i) + [pl.BlockSpec((tm, w), lambda i: (i, 0))],
        out_specs=pl.BlockSpec(memory_space=pl.ANY),
        out_shape=jax.ShapeDtypeStruct((n_rows, 1, w), h_rows.dtype),
        scratch_shapes=[pltpu.SemaphoreType.DMA(())],
        compiler_params=_cparams(1, 8 * tm * w * 4),
        name="dispatch",
    )(*([pos_flat] * TOP_K), h_rows)


def _experts_body(te_ref, tv_ref, nt_ref, xs_ref, win_ref, bin_ref, wout_ref, bout_ref, y_ref, win_bf, wout_bf,
                  *, f_chunk, cast_rows):
    i = pl.program_id(0)
    e = te_ref[i]
    e_prev = te_ref[jnp.maximum(i - 1, 0)]
    d, f2 = win_bf.shape
    f = f2 // 2

    @pl.when((i == 0) | (e != e_prev))
    def _():
        def cast_in(r, carry):
            rows = pl.ds(pl.multiple_of(r * cast_rows, cast_rows), cast_rows)
            win_bf[rows, :] = win_ref[0, rows, :].astype(BF16)
            return carry

        def cast_out(r, carry):
            rows = pl.ds(pl.multiple_of(r * cast_rows, cast_rows), cast_rows)
            wout_bf[rows, :] = wout_ref[0, rows, :].astype(BF16)
            return carry

        lax.fori_loop(0, d // cast_rows, cast_in, 0)
        lax.fori_loop(0, f // cast_rows, cast_out, 0)

    @pl.when(i < nt_ref[0])
    def _():
        xs = xs_ref[:, 0, :]
        live = lax.broadcasted_iota(jnp.int32, xs.shape, 0) < tv_ref[i]
        x = jnp.where(live, xs, 0.0).astype(BF16)
        acc = None
        for c in range(f // f_chunk):
            lo = c * f_chunk
            zg = jnp.dot(x, win_bf[:, lo:lo + f_chunk], preferred_element_type=F32) + bin_ref[0, :, lo:lo + f_chunk]
            zl = (jnp.dot(x, win_bf[:, f + lo:f + lo + f_chunk], preferred_element_type=F32)
                  + bin_ref[0, :, f + lo:f + lo + f_chunk])
            glu = jnp.minimum(zg, SWIGLU_LIMIT)
            lin = jnp.clip(zl, -SWIGLU_LIMIT, SWIGLU_LIMIT)
            act = glu * jax.nn.sigmoid(SWIGLU_ALPHA * glu) * (lin + 1.0)
            part = jnp.dot(act.astype(BF16), wout_bf[lo:lo + f_chunk, :], preferred_element_type=F32)
            acc = part if acc is None else acc + part
        y_ref[:, 0, :] = acc + bout_ref[0]

    @pl.when(i >= nt_ref[0])
    def _():
        y_ref[...] = jnp.zeros(y_ref.shape, y_ref.dtype)


def _experts(tile_expert, tile_valid, n_tiles, xs, w_e_in, b_e_in, w_e_out, b_e_out, tm):
    rows, _, d = xs.shape
    n_e, _, f2 = w_e_in.shape
    f = f2 // 2
    nt_max = rows // tm
    row_map = lambda i, te, tv, nt: (jnp.minimum(i, nt[0] - 1), 0, 0)
    exp_map = lambda i, te, tv, nt: (te[i], 0, 0)
    vmem = 2 * (d * f2 + f * d) * 4 + (d * f2 + f * d) * 2 + 4 * tm * d * 2 + 10 * tm * d * 4
    grid_spec = pltpu.PrefetchScalarGridSpec(
        num_scalar_prefetch=3,
        grid=(nt_max,),
        in_specs=[pl.BlockSpec((tm, 1, d), row_map),
                  pl.BlockSpec((1, d, f2), exp_map), pl.BlockSpec((1, 1, f2), exp_map),
                  pl.BlockSpec((1, f, d), exp_map), pl.BlockSpec((1, 1, d), exp_map)],
        out_specs=pl.BlockSpec((tm, 1, d), lambda i, te, tv, nt: (i, 0, 0)),
        scratch_shapes=[pltpu.VMEM((d, f2), BF16), pltpu.VMEM((f, d), BF16)],
    )
    return pl.pallas_call(
        functools.partial(_experts_body, f_chunk=F_CHUNK, cast_rows=CAST_ROWS),
        grid_spec=grid_spec,
        out_shape=jax.ShapeDtypeStruct((rows, 1, d), F32),
        compiler_params=_cparams(1, vmem),
        name="experts",
    )(tile_expert, tile_valid, n_tiles, xs, w_e_in, b_e_in.reshape(n_e, 1, f2), w_e_out, b_e_out.reshape(n_e, 1, d))


def _final_body(*refs):
    pos_refs, posn_refs = refs[:TOP_K], refs[TOP_K:2 * TOP_K]
    x1_ref, tp_ref, gt_ref, g_ref, y_hbm, o_ref, buf, sem = refs[2 * TOP_K:]
    i = pl.program_id(0)
    n = pl.num_programs(0)
    tm = x1_ref.shape[0]

    def gather(p_refs, slot):
        def issue(i, carry):
            base = pl.multiple_of(i * SUBLANES, SUBLANES)
            for s in range(SUBLANES):
                for k in range(TOP_K):
                    row = p_refs[k][base + s]
                    pltpu.make_async_copy(y_hbm.at[row], buf.at[slot, k, pl.ds(base + s, 1), :],
                                          sem.at[slot]).start(priority=k % 2)
            return carry

        lax.fori_loop(0, tm // SUBLANES, issue, 0)

    @pl.when(i == 0)
    def _():
        gather(pos_refs, 0)

    @pl.when(i + 1 < n)
    def _():
        gather(posn_refs, (i + 1) % 2)

    slot = i % 2
    for k in range(TOP_K):
        pltpu.make_async_copy(buf.at[slot, k], buf.at[slot, k], sem.at[slot]).wait()
    moe = None
    for k in range(TOP_K):
        term = tp_ref[:, k:k + 1] * buf[slot, k]
        moe = term if moe is None else moe + term
    x2 = x1_ref[...] + gt_ref[0] * moe
    o_ref[...] = x2 * lax.rsqrt(jnp.mean(x2 * x2, axis=-1, keepdims=True) + EPS) * g_ref[...]


def _final(pos_flat, x1, top_p, gt2, g_final, y_rows, tm):
    n_tok, d = x1.shape
    bsz = gt2.shape[0]
    n_blk = n_tok // tm
    per_b = n_blk // bsz
    return pl.pallas_call(
        _final_body,
        grid=(n_blk,),
        in_specs=_slot_specs(tm, n_blk, lambda i: i) + _slot_specs(tm, n_blk, lambda i: jnp.minimum(i + 1, n_blk - 1)) + [
                  pl.BlockSpec((tm, d), lambda i: (i, 0)),
                  pl.BlockSpec((tm, top_p.shape[-1]), lambda i: (i, 0)),
                  pl.BlockSpec((1, 1, d), lambda i: (i // per_b, 0, 0)),
                  pl.BlockSpec((1, d), lambda i: (0, 0)),
                  pl.BlockSpec(memory_space=pl.ANY)],
        out_specs=pl.BlockSpec((tm, d), lambda i: (i, 0)),
        out_shape=jax.ShapeDtypeStruct((n_tok, d), F32),
        scratch_shapes=[pltpu.VMEM((2, TOP_K, tm, d), F32), pltpu.SemaphoreType.DMA((2,))],
        compiler_params=_cparams(1, 2 * TOP_K * tm * d * 4 + 4 * tm * d * 4 + 8 * tm * d * 4),
        name="final",
    )(*([pos_flat] * (2 * TOP_K)), x1, top_p, gt2, g_final.reshape(1, d), y_rows)


def _tile_table(counts, tm, nt_max):
    tiles = (counts + tm - 1) // tm
    tile_end = jnp.cumsum(tiles)
    tile_start = tile_end - tiles
    n_tiles = tile_end[-1]
    tile_ids = jnp.arange(nt_max, dtype=jnp.int32)
    last_e = jnp.sum((n_tiles - 1) >= tile_end).astype(jnp.int32)
    te = jnp.sum(tile_ids[:, None] >= tile_end[None, :], axis=1).astype(jnp.int32)
    te = jnp.where(tile_ids < n_tiles, te, last_e)
    sel = te[:, None] == jnp.arange(counts.shape[0], dtype=jnp.int32)[None, :]
    cnt_t = jnp.sum(jnp.where(sel, counts[None, :], 0), axis=1)
    start_t = jnp.sum(jnp.where(sel, tile_start[None, :], 0), axis=1)
    live = jnp.clip(cnt_t - (tile_ids - start_t) * tm, 0, tm).astype(jnp.int32)
    live = jnp.where(tile_ids < n_tiles, live, 0)
    return (tile_start * tm).astype(jnp.int32), te, live, n_tiles.reshape(1).astype(jnp.int32)


def kernel(x, c, ctx, c_ctx, w_ada, b_ada, g_norm1, g_norm2, w_in, w_conv_qk, b_ifgate, g_mh, w_branch_m,
           s5_a_re, s5_a_im, s5_log_dt, s5_b_re, s5_b_im, s5_c_re, s5_c_im, s5_d, w_glu, b_glu, w_branch_s,
           b_merge_gate, w_o, w_router, b_router, w_e_in, b_e_in, w_e_out, b_e_out, g_final):
    bsz, length, d = x.shape
    l_ctx = ctx.shape[1]
    n_qk = 2 * N_HEADS * D_QK
    n_v = N_HEADS * D_V
    n_if = 4 * N_HEADS
    n_u = S5_GROUPS * S5_GROUP
    off_if = n_qk + n_v
    off_u = off_if + n_if
    off_o = off_u + n_u
    layer = 0

    pad_rows = -(bsz + 1) % 8
    c_rows = jnp.concatenate([c, c_ctx[None, :], jnp.zeros((pad_rows, d), F32)], axis=0)
    mod_all = _ada(c_rows, w_ada[layer], b_ada[layer])
    mod = mod_all[:bsz].reshape(bsz, 6, d)
    mod_c = mod_all[bsz, :2 * d].reshape(2, 1, 1, d)

    w_l = w_in[layer]
    w_state = jnp.concatenate([w_l[:, :off_if], w_l[:, off_u:off_o]], axis=1).astype(BF16)
    w_if_t = w_l[:, off_if:off_u].T.astype(BF16)
    b_if = b_ifgate[layer].reshape(n_if)
    proj_c = _proj(ctx, g_norm1[layer], mod_c[1], mod_c[0], w_state, w_if_t, b_if, n_qk, n_v, n_u, l_ctx)
    proj_l = _proj(x, g_norm1[layer], mod[:, 1:2], mod[:, 0:1], w_state, w_if_t, b_if, n_qk, n_v, n_u,
                   min(TM_PROJ, length))
    qk_c, v_c, u_c, gif_c = proj_c
    qk_l, v_l, u_l, gif_l = proj_l

    w9 = w_conv_qk[layer].reshape(9, n_qk)
    q_c, kt_c = _conv(qk_c, w9, D_QK ** -0.5, l_ctx, l_ctx)
    q_l, kt_l = _conv(qk_l, w9, D_QK ** -0.5, GRID_W, min(TM_CONV, length))

    st0 = jnp.zeros((bsz, 2 * N_HEADS, D_QK, 2 * D_V), F32)
    m0 = jnp.zeros((bsz, 2 * N_HEADS, 1), F32)
    _, _, st_c, m_c = _mlstm(q_c, kt_c, v_c, gif_c, st0, m0)
    h_f, h_r, _, _ = _mlstm(q_l, kt_l, v_l, gif_l, st_c, m_c)

    w_in_t, m_tz_t, c_out_t, mu_a, mu_b = _s5_tables(
        s5_a_re[layer], s5_a_im[layer], s5_log_dt[layer], s5_b_re[layer], s5_b_im[layer],
        s5_c_re[layer], s5_c_im[layer], s5_d[layer])
    hs0 = jnp.zeros((bsz, S5_GROUPS, 4 * S5_STATE), F32)
    hs_c, = _s5(u_c, hs0, w_in_t, m_tz_t, c_out_t, mu_a, mu_b)
    y_b, _ = _s5(u_l, hs_c, w_in_t, m_tz_t, c_out_t, mu_a, mu_b,
                 w_glu[layer].T.astype(BF16), b_glu[layer], w_branch_s[layer].T.astype(BF16))

    w_og = w_l[:, off_o:].astype(BF16)
    x1, h2, top_i, top_p, rank, counts = _mixout(
        x, h_f, h_r, y_b, mod, g_norm1[layer], g_norm2[layer], g_mh[layer], w_og, w_branch_m[layer].astype(BF16),
        b_merge_gate[layer], w_o[layer].astype(BF16), w_router[layer].astype(BF16), b_router[layer],
        min(TM_MIX, length))

    tm_e = TM_EXPERT
    n_tok = bsz * length
    nt_max = n_tok * TOP_K // tm_e + N_EXPERTS
    row_start, tile_expert, tile_live, n_tiles = _tile_table(counts.reshape(N_EXPERTS), tm_e, nt_max)
    pos = _rowpos(row_start, top_i, rank, min(TN_ROWPOS, n_tok))
    pos_flat = pos[:TOP_K].reshape(TOP_K * n_tok)
    xs = _dispatch(pos_flat, h2.reshape(n_tok, d), nt_max * tm_e, min(TM_DISPATCH, n_tok))
    y_rows = _experts(tile_expert, tile_live, n_tiles, xs, w_e_in[layer], b_e_in[layer], w_e_out[layer],
                      b_e_out[layer], tm_e)
    out = _final(pos_flat, x1.reshape(n_tok, d), top_p.reshape(n_tok, 8), mod[:, 5:6], g_final, y_rows,
                 min(TM_FINAL, n_tok))
    return out.reshape(bsz, length, d)
```

```python
import functools

import jax
import jax.numpy as jnp
from jax import lax
from jax.experimental import pallas as pl
from jax.experimental.pallas import tpu as pltpu

F32 = jnp.float32
BF16 = jnp.bfloat16
EPS = 1e-6

N_HEADS = 8
D_QK = 64
D_V = 128
M_CHUNK = 256
GRID_W = 64
S5_GROUPS = 32
S5_GROUP = 16
S5_STATE = 64
S5_CHUNK = 16
N_EXPERTS = 32
TOP_K = 4
SWIGLU_LIMIT = 7.0
SWIGLU_ALPHA = 1.702

LANES = 128
SUBLANES = 8
TM_PROJ = 512
TM_CONV = 1024
TM_MIX = 512
TM_EXPERT = 512
TM_DISPATCH = 1024
TM_FINAL = 512
TN_ROWPOS = 8192
F_CHUNK = 512
CAST_ROWS = 128

V7X_VMEM_BYTES = 64 * 1024 * 1024
_VMEM_CAP = V7X_VMEM_BYTES - 8 * 1024 * 1024


def _cparams(n_axes, vmem_bytes):
    limit = int(min(_VMEM_CAP, max(32 * 1024 * 1024, vmem_bytes)))
    return pltpu.CompilerParams(dimension_semantics=("arbitrary",) * n_axes, vmem_limit_bytes=limit)


def _silu(x):
    return x * jax.nn.sigmoid(x)


def _norm_mod(x, g, scale, shift):
    ms = jnp.mean(x * x, axis=-1, keepdims=True)
    return (x * lax.rsqrt(ms + EPS) * g) * (1.0 + scale) + shift


def _split3(x):
    hi = x.astype(BF16)
    r1 = x - hi.astype(F32)
    mid = r1.astype(BF16)
    lo = (r1 - mid.astype(F32)).astype(BF16)
    return hi, mid, lo


def _dot_nt(a, b):
    return lax.dot_general(a, b, (((1,), (1,)), ((), ())), preferred_element_type=F32)


def _dot_tn(a, b):
    return lax.dot_general(a, b, (((0,), (0,)), ((), ())), preferred_element_type=F32)


def _ada_body(c_ref, w_ref, b_ref, o_ref):
    s = _silu(c_ref[...])
    o_ref[...] = jnp.dot(s, w_ref[...], preferred_element_type=F32,
                         precision=lax.Precision.HIGHEST) + b_ref[...]


def _ada(c_rows, w_ada, b_ada):
    rows, d = c_rows.shape
    n = w_ada.shape[1]
    tn = 1024
    return pl.pallas_call(
        _ada_body,
        grid=(n // tn,),
        in_specs=[pl.BlockSpec((rows, d), lambda j: (0, 0)),
                  pl.BlockSpec((d, tn), lambda j: (0, j)),
                  pl.BlockSpec((1, tn), lambda j: (0, j))],
        out_specs=pl.BlockSpec((rows, tn), lambda j: (0, j)),
        out_shape=jax.ShapeDtypeStruct((rows, n), F32),
        compiler_params=_cparams(1, 4 * d * tn * 4),
        name="ada",
    )(c_rows, w_ada, b_ada.reshape(1, n))


def _proj_body(x_ref, g_ref, sc_ref, sh_ref, w_ref, wif_ref, bif_ref, qk_ref, v_ref, u_ref, gif_ref):
    h = _norm_mod(x_ref[0], g_ref[...], sc_ref[0], sh_ref[0]).astype(BF16)
    n_qk = qk_ref.shape[-1]
    n_v = v_ref.shape[-1]
    qk_ref[0] = jnp.dot(h, w_ref[:, :n_qk], preferred_element_type=F32).astype(BF16)
    v_ref[0] = jnp.dot(h, w_ref[:, n_qk:n_qk + n_v], preferred_element_type=F32).astype(BF16)
    u = jnp.dot(h, w_ref[:, n_qk + n_v:], preferred_element_type=F32)
    for j in range(u_ref.shape[1]):
        u_ref[0, j] = u[:, j * LANES:(j + 1) * LANES]
    gif_ref[0] = _dot_nt(wif_ref[...], h) + bif_ref[...]


def _proj(x, g, scale, shift, w_state, w_if_t, b_if, n_qk, n_v, n_u, tm):
    bsz, length, d = x.shape
    n_if = b_if.shape[-1]
    per_batch = scale.shape[0] == bsz
    mod_map = (lambda b, i: (b, 0, 0)) if per_batch else (lambda b, i: (0, 0, 0))
    cols = w_state.shape[1]
    vmem = 2 * (tm * d * 4 + d * cols * 2 + tm * (n_qk + n_v + n_u) * 2 + tm * 128 * 4) + 6 * tm * d * 4
    return pl.pallas_call(
        _proj_body,
        grid=(bsz, length // tm),
        in_specs=[pl.BlockSpec((1, tm, d), lambda b, i: (b, i, 0)),
                  pl.BlockSpec((1, d), lambda b, i: (0, 0)),
                  pl.BlockSpec((1, 1, d), mod_map),
                  pl.BlockSpec((1, 1, d), mod_map),
                  pl.BlockSpec((d, cols), lambda b, i: (0, 0)),
                  pl.BlockSpec((n_if, d), lambda b, i: (0, 0)),
                  pl.BlockSpec((n_if, 1), lambda b, i: (0, 0))],
        out_specs=[pl.BlockSpec((1, tm, n_qk), lambda b, i: (b, i, 0)),
                   pl.BlockSpec((1, tm, n_v), lambda b, i: (b, i, 0)),
                   pl.BlockSpec((1, n_u // LANES, tm, LANES), lambda b, i: (b, 0, i, 0)),
                   pl.BlockSpec((1, n_if, tm), lambda b, i: (b, 0, i))],
        out_shape=[jax.ShapeDtypeStruct((bsz, length, n_qk), BF16),
                   jax.ShapeDtypeStruct((bsz, length, n_v), BF16),
                   jax.ShapeDtypeStruct((bsz, n_u // LANES, length, LANES), F32),
                   jax.ShapeDtypeStruct((bsz, n_if, length), F32)],
        compiler_params=_cparams(2, vmem),
        name="proj",
    )(x, g.reshape(1, d), scale, shift, w_state, w_if_t, b_if.reshape(n_if, 1))


def _conv_body(main_ref, prev_ref, next_ref, w_ref, q_ref, kt_ref, *, width, q_scale):
    i = pl.program_id(1)
    last = pl.num_programs(1) - 1
    t = main_ref.shape[1]
    n = t + 2 * width
    main = main_ref[0].astype(F32)
    prev = jnp.where(i > 0, prev_ref[0].astype(F32), 0.0)
    nxt = jnp.where(i < last, next_ref[0].astype(F32), 0.0)
    ext = jnp.concatenate([prev, main, nxt], axis=0)
    col = lax.broadcasted_iota(jnp.int32, (t, 1), 0) % width
    acc = None
    for dx in (-1, 0, 1):
        shifted = ext if dx == 0 else pltpu.roll(ext, (-dx) % n, axis=0)
        part = None
        for dy in (-1, 0, 1):
            tap = w_ref[(dy + 1) * 3 + (dx + 1):(dy + 1) * 3 + (dx + 1) + 1, :]
            term = tap * shifted[width + dy * width:width + dy * width + t]
            part = term if part is None else part + term
        if dx == -1:
            part = jnp.where(col == 0, 0.0, part)
        elif dx == 1:
            part = jnp.where(col == width - 1, 0.0, part)
        acc = part if acc is None else acc + part
    y = _silu(acc)

    @pl.when(pl.program_id(2) == 0)
    def _():
        q_ref[0] = (y * q_scale).astype(q_ref.dtype)

    @pl.when(pl.program_id(2) == 1)
    def _():
        kt_ref[0] = y.T.astype(kt_ref.dtype)


def _conv(qk_pre, w9, q_scale, width, t_block):
    bsz, length, ch2 = qk_pre.shape
    ch = ch2 // 2
    rpb = t_block // width
    n_rows = length // width
    vmem = 4 * (t_block + 2 * width) * ch * 2 + 14 * (t_block + 2 * width) * ch * 4
    return pl.pallas_call(
        functools.partial(_conv_body, width=width, q_scale=q_scale),
        grid=(bsz, length // t_block, 2),
        in_specs=[pl.BlockSpec((1, t_block, ch), lambda b, i, c: (b, i, c)),
                  pl.BlockSpec((1, width, ch), lambda b, i, c: (b, jnp.maximum(i * rpb - 1, 0), c)),
                  pl.BlockSpec((1, width, ch), lambda b, i, c: (b, jnp.minimum((i + 1) * rpb, n_rows - 1), c)),
                  pl.BlockSpec((9, ch), lambda b, i, c: (0, c))],
        out_specs=[pl.BlockSpec((1, t_block, ch), lambda b, i, c: (b, i, 0)),
                   pl.BlockSpec((1, ch, t_block), lambda b, i, c: (b, 0, i))],
        out_shape=[jax.ShapeDtypeStruct((bsz, length, ch), BF16),
                   jax.ShapeDtypeStruct((bsz, ch, length), BF16)],
        compiler_params=_cparams(3, vmem),
        name="conv",
    )(qk_pre, qk_pre, qk_pre, w9)


def _log_sigmoid(x):
    return jnp.minimum(x, 0.0) - jnp.log1p(jnp.exp(-jnp.abs(x)))


def _cumsum_lanes_exact(x, reverse):
    t = x.shape[-1]
    r_idx = lax.broadcasted_iota(jnp.int32, (t, t), 0)
    c_idx = lax.broadcasted_iota(jnp.int32, (t, t), 1)
    u01 = jnp.where((r_idx >= c_idx) if reverse else (r_idx <= c_idx), 1.0, 0.0).astype(BF16)
    out = None
    for piece in _split3(x):
        p = jnp.dot(piece, u01, preferred_element_type=F32)
        out = p if out is None else out + p
    return out


def _cummax_lanes(x, reverse):
    t = x.shape[-1]
    lane = lax.broadcasted_iota(jnp.int32, x.shape, 1)
    sh = 1
    while sh < t:
        if reverse:
            cand = jnp.where(lane < t - sh, pltpu.roll(x, t - sh, axis=1), -jnp.inf)
        else:
            cand = jnp.where(lane >= sh, pltpu.roll(x, sh, axis=1), -jnp.inf)
        x = jnp.maximum(x, cand)
        sh *= 2
    return x


def _mlstm_gate_rows(g_ref, m_col, d):
    t = g_ref.shape[-1]
    base = d * 2 * N_HEADS
    li = g_ref[0, base:base + N_HEADS, :]
    lf = _log_sigmoid(g_ref[0, base + N_HEADS:base + 2 * N_HEADS, :])
    b = _cumsum_lanes_exact(lf, reverse=(d == 1))
    g = li - b
    a = jnp.maximum(_cummax_lanes(g, reverse=(d == 1)), m_col)
    end = 0 if d == 1 else t - 1
    a_end = a[:, end:end + 1]
    return dict(g=g, a=a, ie=jnp.exp(m_col - a), emt=jnp.exp(-b - a), we=jnp.exp(g - a_end),
                dec=jnp.exp(m_col - a_end), m_new=b[:, end:end + 1] + a_end)


def _mlstm_body(qf_ref, kf_ref, vf_ref, gf_ref, qr_ref, kr_ref, vr_ref, gr_ref, st0_ref, m0_ref,
                hf_ref, hr_ref, st_ref, m_ref):
    @pl.when(pl.program_id(1) == 0)
    def _():
        st_ref[...] = st0_ref[...]
        m_ref[...] = m0_ref[...]

    t = qf_ref.shape[1]
    m_all = m_ref[0]
    rows = [_mlstm_gate_rows(g_ref, m_all[d * N_HEADS:(d + 1) * N_HEADS], d) for d, g_ref in ((0, gf_ref), (1, gr_ref))]
    both = lambda name: jnp.concatenate([rows[0][name], rows[1][name]], axis=0)
    n_hd = 2 * N_HEADS
    pad = jnp.zeros((128 - 3 * n_hd, t), F32)
    cols = jnp.concatenate([both('a'), both('ie'), both('emt'), pad], axis=0).T
    m_ref[0] = both('m_new')
    r_idx = lax.broadcasted_iota(jnp.int32, (t, t), 0)
    c_idx = lax.broadcasted_iota(jnp.int32, (t, t), 1)
    ones_blk = jnp.ones((t, D_V), BF16)
    for d, (q_ref, kt_ref, v_ref, h_ref) in enumerate(((qf_ref, kf_ref, vf_ref, hf_ref), (qr_ref, kr_ref, vr_ref, hr_ref))):
        causal = (c_idx >= r_idx) if d == 1 else (c_idx <= r_idx)
        for h in range(N_HEADS):
            j = d * N_HEADS + h
            q = q_ref[0, :, h * D_QK:(h + 1) * D_QK]
            kt = kt_ref[0, h * D_QK:(h + 1) * D_QK, :]
            v1 = jnp.concatenate([v_ref[0, :, h * D_V:(h + 1) * D_V], ones_blk], axis=1)
            state = st_ref[0, j]
            a_col = cols[:, j:j + 1]
            ie_col = cols[:, n_hd + j:n_hd + j + 1]
            emt_col = cols[:, 2 * n_hd + j:2 * n_hd + j + 1]
            dmat = jnp.exp(jnp.where(causal, rows[d]['g'][h:h + 1, :] - a_col, -jnp.inf))
            s = (jnp.dot(q, kt, preferred_element_type=F32) * dmat).astype(BF16)
            z = jnp.dot(q, state.astype(BF16), preferred_element_type=F32)
            p = jnp.dot(s, v1, preferred_element_type=F32)
            num = ie_col * z[:, :D_V] + p[:, :D_V]
            den = ie_col * z[:, D_V:] + p[:, D_V:]
            h_ref[0, :, h * D_V:(h + 1) * D_V] = (num / jnp.maximum(jnp.abs(den), emt_col)).astype(h_ref.dtype)
            kw = (kt.astype(F32) * rows[d]['we'][h:h + 1, :]).astype(BF16)
            st_ref[0, j] = rows[d]['dec'][h:h + 1, :] * state + jnp.dot(kw, v1, preferred_element_type=F32)


def _mlstm(q, k_t, v, gif_t, st0, m0):
    bsz, length, hv = v.shape
    t = M_CHUNK
    nc = length // t
    hq = N_HEADS * D_QK
    ng = gif_t.shape[1]
    fwd = lambda b, i: (b, i, 0)
    rev = lambda b, i: (b, nc - 1 - i, 0)
    fwd_t = lambda b, i: (b, 0, i)
    rev_t = lambda b, i: (b, 0, nc - 1 - i)
    st_spec = pl.BlockSpec((1,) + st0.shape[1:], lambda b, i: (b, 0, 0, 0))
    m_spec = pl.BlockSpec((1,) + m0.shape[1:], lambda b, i: (b, 0, 0))
    vmem = 24 * 1024 * 1024
    return pl.pallas_call(
        _mlstm_body,
        grid=(bsz, nc),
        in_specs=[pl.BlockSpec((1, t, hq), fwd), pl.BlockSpec((1, hq, t), fwd_t),
                  pl.BlockSpec((1, t, hv), fwd), pl.BlockSpec((1, ng, t), fwd_t),
                  pl.BlockSpec((1, t, hq), rev), pl.BlockSpec((1, hq, t), rev_t),
                  pl.BlockSpec((1, t, hv), rev), pl.BlockSpec((1, ng, t), rev_t),
                  st_spec, m_spec],
        out_specs=[pl.BlockSpec((1, t, hv), fwd), pl.BlockSpec((1, t, hv), rev), st_spec, m_spec],
        out_shape=[jax.ShapeDtypeStruct((bsz, length, hv), BF16),
                   jax.ShapeDtypeStruct((bsz, length, hv), BF16),
                   jax.ShapeDtypeStruct(st0.shape, F32),
                   jax.ShapeDtypeStruct(m0.shape, F32)],
        compiler_params=_cparams(2, vmem),
        name="mlstm",
    )(q, k_t, v, gif_t, q, k_t, v, gif_t, st0, m0)


def _s5_tables(a_re, a_im, log_dt, b_re, b_im, c_re, c_im, d_skip):
    hp = lax.Precision.HIGHEST
    t = S5_CHUNK
    n_dir, groups, p = a_re.shape
    cg = b_re.shape[-1]
    dt = jnp.exp(log_dt)[..., None]

    def lam_pow(n):
        mag = jnp.exp(n * (dt * a_re)[..., None])
        ang = n * (dt * a_im)[..., None]
        return mag * jnp.cos(ang), mag * jnp.sin(ang)

    ab_re, ab_im = (z[..., 0] for z in lam_pow(jnp.ones((1,), F32)))
    den = a_re * a_re + a_im * a_im
    xr = ab_re - 1.0
    cf_re = (xr * a_re + ab_im * a_im) / den
    cf_im = (ab_im * a_re - xr * a_im) / den
    bb_re = cf_re[..., None] * b_re - cf_im[..., None] * b_im
    bb_im = cf_re[..., None] * b_im + cf_im[..., None] * b_re
    jj = jnp.arange(t + 1, dtype=F32)
    lp_re, lp_im = lam_pow(jj)

    def w_dir(d, exps):
        lr = lp_re[d][:, :, exps]
        li = lp_im[d][:, :, exps]
        wr = lr[..., None] * bb_re[d][:, :, None, :] - li[..., None] * bb_im[d][:, :, None, :]
        wi = lr[..., None] * bb_im[d][:, :, None, :] + li[..., None] * bb_re[d][:, :, None, :]
        to_rows = lambda w: jnp.transpose(w, (0, 2, 3, 1)).reshape(groups, t * cg, p)
        return to_rows(wr), to_rows(wi)

    s_idx = jnp.arange(t)
    wf_re, wf_im = w_dir(0, t - 1 - s_idx)
    wr_re, wr_im = w_dir(1, s_idx)
    zw = jnp.zeros_like(wf_re)
    w_in = jnp.concatenate([jnp.concatenate([wf_re, zw, wf_im, zw], axis=-1),
                            jnp.concatenate([zw, wr_re, zw, wr_im], axis=-1)], axis=1)

    def c_dir(d, exps):
        lr = lp_re[d][:, :, exps]
        li = lp_im[d][:, :, exps]
        cr = jnp.transpose(c_re[d], (0, 2, 1))
        ci = jnp.transpose(c_im[d], (0, 2, 1))
        o_re = cr[:, :, None, :] * lr[..., None] - ci[:, :, None, :] * li[..., None]
        o_im = cr[:, :, None, :] * li[..., None] + ci[:, :, None, :] * lr[..., None]
        return o_re.reshape(groups, p, t * cg), (-o_im).reshape(groups, p, t * cg)

    cf_r, cf_i = c_dir(0, s_idx + 1)
    cr_r, cr_i = c_dir(1, t - s_idx)
    c_out = jnp.concatenate([cf_r, cr_r, cf_i, cr_i], axis=1)

    def k_dir(d):
        lr = lp_re[d][:, :, :t]
        li = lp_im[d][:, :, :t]
        clr = c_re[d][:, :, :, None] * lr[:, None] - c_im[d][:, :, :, None] * li[:, None]
        cli = c_re[d][:, :, :, None] * li[:, None] + c_im[d][:, :, :, None] * lr[:, None]
        return (jnp.einsum('gqpj,gpc->gjqc', clr, bb_re[d], precision=hp)
                - jnp.einsum('gqpj,gpc->gjqc', cli, bb_im[d], precision=hp))

    kf = k_dir(0)
    kr = k_dir(1)
    lag = s_idx[None, :] - s_idx[:, None]
    sel = jnp.concatenate([lag[None] == s_idx[:, None, None], -lag[None] == s_idx[:, None, None]], axis=0).astype(F32)
    resp_t = jnp.einsum('jst,gjqc->gtqsc', sel, jnp.concatenate([kf, kr], axis=1), precision=hp)
    skip = (jnp.eye(t, dtype=F32)[None, :, None, :, None] * jnp.eye(cg, dtype=F32)[None, None, :, None, :]
            * d_skip.reshape(groups, 1, cg, 1, 1))
    m_tz_t = (resp_t + skip).reshape(groups, t * cg, t * cg)

    n_pow = 8
    kk = (t * (2 ** jnp.arange(n_pow))).astype(F32)
    mp_re, mp_im = lam_pow(kk)
    mp_re = jnp.transpose(mp_re, (0, 1, 3, 2))
    mp_im = jnp.transpose(mp_im, (0, 1, 3, 2))
    mu_re = jnp.concatenate([mp_re[0], mp_re[1]], axis=-1)
    mu_im = jnp.concatenate([mp_im[0], mp_im[1]], axis=-1)
    tr = lambda m: jnp.transpose(m, (0, 2, 1)).astype(BF16)
    return w_in.astype(BF16), m_tz_t.astype(BF16), tr(c_out), mu_re, mu_im


def _pad_rows(x, rows):
    return x if x.shape[0] == rows else jnp.concatenate([x, jnp.zeros((rows - x.shape[0],) + x.shape[1:], x.dtype)], axis=0)


def _s5_body(*refs, nch, n_steps, with_output):
    if with_output:
        (u_ref, h0_ref, win_ref, mtzt_ref, coutt_ref, mure_ref, muim_ref, wglut_ref, bglu_ref, wbt_ref,
         yb_ref, hout_ref, xt_scr, yt_scr) = refs
    else:
        u_ref, h0_ref, win_ref, mure_ref, muim_ref, hout_ref, xt_scr = refs
    t_c, cg, groups = S5_CHUNK, S5_GROUP, S5_GROUPS
    p2 = 2 * S5_STATE
    ncp = xt_scr.shape[-1]
    for t in range(t_c):
        ut = jnp.concatenate([u_ref[0, j, pl.ds(t, nch, stride=t_c), :] for j in range(u_ref.shape[1])], axis=1)
        ut = _pad_rows(ut, ncp)
        xt_scr[:, t] = ut.T.astype(BF16).reshape(groups, cg, ncp)

    r_idx = lax.broadcasted_iota(jnp.int32, (ncp, ncp), 0)
    c_idx = lax.broadcasted_iota(jnp.int32, (ncp, ncp), 1)
    flip = jnp.where(r_idx + c_idx == nch - 1, 1.0, 0.0).astype(BF16)
    row = lax.broadcasted_iota(jnp.int32, (ncp, p2), 0)
    is_fwd = (lax.broadcasted_iota(jnp.int32, (ncp, 2 * p2), 1) % p2) < S5_STATE

    def group(g, carry):
        x = xt_scr[g].reshape(t_c * cg, ncp)
        x_rev = jnp.dot(x, flip, preferred_element_type=F32).astype(BF16)
        local = _dot_tn(jnp.concatenate([x, x_rev], axis=0), win_ref[g])
        l_re, l_im = local[:, :p2], local[:, p2:]
        h0 = h0_ref[0, pl.ds(g, 1), :]
        z_re = jnp.where(row == 0, h0[:, :p2], pltpu.roll(l_re, 1, axis=0))
        z_im = jnp.where(row == 0, h0[:, p2:], pltpu.roll(l_im, 1, axis=0))
        for k in range(n_steps):
            sft = 1 << k
            a_re = mure_ref[g, k:k + 1, :]
            a_im = muim_ref[g, k:k + 1, :]
            p_re = jnp.where(row >= sft, pltpu.roll(z_re, sft, axis=0), 0.0)
            p_im = jnp.where(row >= sft, pltpu.roll(z_im, sft, axis=0), 0.0)
            z_re, z_im = z_re + a_re * p_re - a_im * p_im, z_im + a_re * p_im + a_im * p_re
        if with_output:
            z = jnp.concatenate([z_re, z_im], axis=1).astype(BF16)
            z_flip = jnp.dot(flip, z, preferred_element_type=F32).astype(BF16)
            z_nat = jnp.where(is_fwd, z, z_flip)
            y = jnp.dot(mtzt_ref[g], x, preferred_element_type=F32) + _dot_nt(coutt_ref[g], z_nat)
            yt_scr[g] = jax.nn.gelu(y).astype(BF16).reshape(t_c, cg, ncp)
        a_re = mure_ref[g, 0:1, :]
        a_im = muim_ref[g, 0:1, :]
        e_re, e_im = z_re[nch - 1:nch], z_im[nch - 1:nch]
        hout_ref[0, pl.ds(g, 1), :] = jnp.concatenate(
            [a_re * e_re - a_im * e_im + l_re[nch - 1:nch], a_re * e_im + a_im * e_re + l_im[nch - 1:nch]], axis=1)
        return carry

    lax.fori_loop(0, groups, group, 0, unroll=8)

    if with_output:
        def tail(t, carry):
            ys = yt_scr[:, t].reshape(groups * cg, ncp)
            glu = jax.nn.sigmoid(jnp.dot(wglut_ref[...], ys, preferred_element_type=F32) + bglu_ref[...])
            ybt = jnp.dot(wbt_ref[...], (ys.astype(F32) * glu).astype(BF16), preferred_element_type=F32)
            yb_ref[0, t] = ybt.T[:nch].astype(yb_ref.dtype)
            return carry

        lax.fori_loop(0, t_c, tail, 0, unroll=4)


def _s5(u, h0, w_in_t, m_tz_t, c_out_t, mu_a, mu_b, w_glu_t=None, b_glu=None, w_b_t=None):
    bsz, n_blk, length, _ = u.shape
    width = n_blk * LANES
    with_output = w_glu_t is not None
    nch = length // S5_CHUNK
    ncp = -(-nch // 128) * 128
    p4 = 4 * S5_STATE
    n_steps = max(1, (nch - 1).bit_length())
    tc = S5_CHUNK * S5_GROUP
    one = pl.Buffered(1)
    full = lambda a: pl.BlockSpec(a.shape, lambda b: (0,) * a.ndim, pipeline_mode=one)
    in_specs = [pl.BlockSpec((1, n_blk, length, LANES), lambda b: (b, 0, 0, 0), pipeline_mode=one),
                pl.BlockSpec((1, S5_GROUPS, p4), lambda b: (b, 0, 0)), full(w_in_t)]
    args = [u, h0, w_in_t]
    scratch = [pltpu.VMEM((S5_GROUPS, S5_CHUNK, S5_GROUP, ncp), BF16)]
    out_specs = [pl.BlockSpec((1, S5_GROUPS, p4), lambda b: (b, 0, 0))]
    out_shape = [jax.ShapeDtypeStruct((bsz, S5_GROUPS, p4), F32)]
    vmem = length * width * 4 + 4 * S5_GROUPS * tc * tc * 2 + 2 * S5_GROUPS * tc * ncp * 2 + 8 * 1024 * 1024
    if with_output:
        d_out = w_b_t.shape[0]
        in_specs += [full(m_tz_t), full(c_out_t), full(mu_a), full(mu_b), full(w_glu_t),
                     pl.BlockSpec((width, 1), lambda b: (0, 0)), full(w_b_t)]
        args += [m_tz_t, c_out_t, mu_a, mu_b, w_glu_t, b_glu.reshape(width, 1), w_b_t]
        scratch.append(pltpu.VMEM((S5_GROUPS, S5_CHUNK, S5_GROUP, ncp), BF16))
        out_specs.insert(0, pl.BlockSpec((1, S5_CHUNK, nch, d_out), lambda b: (b, 0, 0, 0)))
        out_shape.insert(0, jax.ShapeDtypeStruct((bsz, S5_CHUNK, nch, d_out), BF16))
        vmem += 2 * S5_CHUNK * nch * d_out * 2 + (width * width + width * d_out) * 2
    else:
        in_specs += [full(mu_a), full(mu_b)]
        args += [mu_a, mu_b]
    return pl.pallas_call(
        functools.partial(_s5_body, nch=nch, n_steps=n_steps, with_output=with_output),
        grid=(bsz,),
        in_specs=in_specs,
        out_specs=out_specs,
        out_shape=out_shape,
        scratch_shapes=scratch,
        compiler_params=_cparams(1, vmem),
        name="s5",
    )(*args)


def _mixout_body(x_ref, hf_ref, hr_ref, yb_ref, mod_ref, g1_ref, g2_ref, gmh_ref, wog_ref, wa_ref,
                 bgate_ref, wo_ref, wr_ref, br_ref,
                 x1_ref, h2_ref, ti_ref, tp_ref, rk_ref, cnt_ref, base_ref, yb_scr):
    x = x_ref[0]
    d = x.shape[-1]
    mod = mod_ref[0]
    h = _norm_mod(x, g1_ref[...], mod[1:2], mod[0:1]).astype(BF16)
    og = jnp.dot(h, wog_ref[...], preferred_element_type=F32)
    hm = hf_ref[0].astype(F32) + hr_ref[0].astype(F32)
    heads = []
    for hd in range(N_HEADS):
        blk = hm[:, hd * D_V:(hd + 1) * D_V]
        heads.append(blk * lax.rsqrt(jnp.mean(blk * blk, axis=-1, keepdims=True) + EPS))
    hn = jnp.concatenate(heads, axis=1) * gmh_ref[...]
    y_a = jnp.dot((hn * jax.nn.sigmoid(og[:, :d])).astype(BF16), wa_ref[...], preferred_element_type=F32)
    n_pos = yb_ref.shape[1]
    for t in range(n_pos):
        blk = yb_ref[0, t].astype(F32)
        for j in range(yb_scr.shape[0]):
            yb_scr[j, pl.ds(t, yb_ref.shape[2], stride=n_pos), :] = blk[:, j * LANES:(j + 1) * LANES]
    y_b = jnp.concatenate([yb_scr[j] for j in range(yb_scr.shape[0])], axis=1)
    gates = jax.nn.sigmoid(og[:, d:] + bgate_ref[...])
    merged = gates[:, :d] * y_a + gates[:, d:] * y_b
    x1 = x + mod[2:3] * jnp.dot(merged.astype(BF16), wo_ref[...], preferred_element_type=F32)
    x1_ref[0] = x1
    h2f = _norm_mod(x1, g2_ref[...], mod[4:5], mod[3:4])
    h2_ref[0] = h2f
    logits = jnp.dot(h2f.astype(BF16), wr_ref[...], preferred_element_type=F32) + br_ref[...]
    tm, n_e = logits.shape
    e_iota = lax.broadcasted_iota(jnp.int32, (tm, n_e), 1)
    lane = lax.broadcasted_iota(jnp.int32, (tm, LANES), 1)
    n_slot = ti_ref.shape[0]
    ti = jnp.zeros(lane.shape, jnp.int32)
    tv = jnp.zeros(lane.shape, F32)
    top = None
    chosen = []
    for k in range(TOP_K):
        mx = jnp.max(logits, axis=-1, keepdims=True)
        idx = jnp.min(jnp.where(logits == mx, e_iota, n_e), axis=-1, keepdims=True)
        top = mx if top is None else top
        ti = jnp.where(lane == k, idx, ti)
        tv = jnp.where(lane == k, jnp.exp(mx - top), tv)
        chosen.append(e_iota == idx)
        logits = jnp.where(chosen[-1], -jnp.inf, logits)
    ti_ref[...] = ti.T[:n_slot]
    tp_ref[0] = (tv / jnp.sum(tv, axis=-1, keepdims=True))[:, :tp_ref.shape[-1]]

    @pl.when((pl.program_id(0) == 0) & (pl.program_id(1) == 0))
    def _():
        base_ref[...] = jnp.zeros(base_ref.shape, F32)

    onehot = jnp.zeros((tm, n_e), F32)
    for sel in chosen:
        onehot = onehot + jnp.where(sel, 1.0, 0.0)
    below = (lax.broadcasted_iota(jnp.int32, (tm, tm), 1) < lax.broadcasted_iota(jnp.int32, (tm, tm), 0))
    before = jnp.dot(jnp.where(below, 1.0, 0.0).astype(BF16), onehot.astype(BF16),
                     preferred_element_type=F32) + base_ref[...]
    rk = jnp.zeros(lane.shape, jnp.int32)
    for k, sel in enumerate(chosen):
        rank = jnp.sum(jnp.where(sel, before, 0.0), axis=-1, keepdims=True)
        rk = jnp.where(lane == k, rank.astype(jnp.int32), rk)
    rk_ref[...] = rk.T[:n_slot]
    base_ref[...] = base_ref[...] + jnp.sum(onehot, axis=0, keepdims=True)
    cnt_ref[...] = base_ref[...].astype(jnp.int32)


def _mixout(x, h_f, h_r, y_b, mod, g1, g2, g_mh, w_og, w_a, b_gate, w_o, w_r, b_r, tm):
    bsz, length, d = x.shape
    n_pos = y_b.shape[1]
    n_e = w_r.shape[-1]
    nt = length // tm
    tok = lambda b, i: (b, i, 0)
    slot = lambda b, i: (0, b * nt + i)
    const2 = lambda b, i: (0, 0)

    def wspec(w):
        return pl.BlockSpec(w.shape, const2, pipeline_mode=pl.Buffered(1))

    weights = (w_og, w_a, w_o, w_r)
    w_bytes = sum(int(w.size) * w.dtype.itemsize for w in weights)
    vmem = w_bytes + 2 * tm * (d * 4 + 3 * d * 2 + d * 4 + d * 4) + 15 * tm * d * 4
    return pl.pallas_call(
        _mixout_body,
        grid=(bsz, length // tm),
        in_specs=[pl.BlockSpec((1, tm, d), tok), pl.BlockSpec((1, tm, d), tok), pl.BlockSpec((1, tm, d), tok),
                  pl.BlockSpec((1, n_pos, tm // n_pos, d), lambda b, i: (b, 0, i, 0)),
                  pl.BlockSpec((1, 6, d), lambda b, i: (b, 0, 0)),
                  pl.BlockSpec((1, d), const2), pl.BlockSpec((1, d), const2), pl.BlockSpec((1, d), const2),
                  wspec(w_og), wspec(w_a),
                  pl.BlockSpec((1, 2 * d), const2), wspec(w_o), wspec(w_r), pl.BlockSpec((1, n_e), const2)],
        out_specs=[pl.BlockSpec((1, tm, d), tok), pl.BlockSpec((1, tm, d), tok),
                   pl.BlockSpec((SUBLANES, tm), slot), pl.BlockSpec((1, tm, 8), tok), pl.BlockSpec((SUBLANES, tm), slot),
                   pl.BlockSpec((1, n_e), const2)],
        out_shape=[jax.ShapeDtypeStruct((bsz, length, d), F32),
                   jax.ShapeDtypeStruct((bsz, length, d), F32),
                   jax.ShapeDtypeStruct((SUBLANES, bsz * length), jnp.int32),
                   jax.ShapeDtypeStruct((bsz, length, 8), F32),
                   jax.ShapeDtypeStruct((SUBLANES, bsz * length), jnp.int32),
                   jax.ShapeDtypeStruct((1, n_e), jnp.int32)],
        scratch_shapes=[pltpu.VMEM((1, n_e), F32), pltpu.VMEM((d // LANES, tm, LANES), F32)],
        compiler_params=_cparams(2, vmem),
        name="mixout",
    )(x, h_f, h_r, y_b, mod, g1.reshape(1, d), g2.reshape(1, d), g_mh.reshape(1, d), w_og, w_a,
      b_gate.reshape(1, 2 * d), w_o, w_r, b_r.reshape(1, n_e))


def _rowpos_body(rs_ref, ti_ref, rk_ref, pos_ref):
    ti = ti_ref[...]
    start = jnp.zeros(ti.shape, jnp.int32)
    for e in range(N_EXPERTS):
        start = jnp.where(ti == e, rs_ref[e], start)
    pos_ref[...] = start + rk_ref[...]


def _rowpos(row_start, top_i, rank, tn):
    w, n = top_i.shape
    grid_spec = pltpu.PrefetchScalarGridSpec(
        num_scalar_prefetch=1,
        grid=(n // tn,),
        in_specs=[pl.BlockSpec((w, tn), lambda i, rs: (0, i)), pl.BlockSpec((w, tn), lambda i, rs: (0, i))],
        out_specs=pl.BlockSpec((w, tn), lambda i, rs: (0, i)),
    )
    return pl.pallas_call(
        _rowpos_body,
        grid_spec=grid_spec,
        out_shape=jax.ShapeDtypeStruct((w, n), jnp.int32),
        compiler_params=_cparams(1, 16 * w * tn * 4),
        name="rowpos",
    )(row_start, top_i, rank)


def _slot_specs(tm, n_blocks, block_of):
    return [pl.BlockSpec((tm,), functools.partial(lambda i, k: (k * n_blocks + block_of(i),), k=k),
                         memory_space=pltpu.SMEM) for k in range(TOP_K)]


def _dispatch_body(*refs):
    pos_refs, (h_ref, xs_hbm, sem) = refs[:TOP_K], refs[TOP_K:]
    tm = h_ref.shape[0]

    def issue(i, carry):
        base = pl.multiple_of(i * SUBLANES, SUBLANES)
        for s in range(SUBLANES):
            for k in range(TOP_K):
                row = pos_refs[k][base + s]
                pltpu.make_async_copy(h_ref.at[pl.ds(base + s, 1), :], xs_hbm.at[row],
                                      sem).start(priority=k % 2)
        return carry

    lax.fori_loop(0, tm // SUBLANES, issue, 0)
    for k in range(TOP_K):
        pltpu.make_async_copy(h_ref, h_ref, sem).wait()


def _dispatch(pos_flat, h_rows, n_rows, tm):
    n, w = h_rows.shape
    return pl.pallas_call(
        _dispatch_body,
        grid=(n // tm,),
        in_specs=_slot_specs(tm, n // tm, lambda i: i) + [pl.BlockSpec((tm, w), lambda i: (i, 0))],
        out_specs=pl.BlockSpec(memory_space=pl.ANY),
        out_shape=jax.ShapeDtypeStruct((n_rows, 1, w), h_rows.dtype),
        scratch_shapes=[pltpu.SemaphoreType.DMA(())],
        compiler_params=_cparams(1, 8 * tm * w * 4),
        name="dispatch",
    )(*([pos_flat] * TOP_K), h_rows)


def _experts_body(te_ref, tv_ref, nt_ref, xs_ref, win_ref, bin_ref, wout_ref, bout_ref, y_ref, win_bf, wout_bf,
                  *, f_chunk, cast_rows):
    i = pl.program_id(0)
    e = te_ref[i]
    e_prev = te_ref[jnp.maximum(i - 1, 0)]
    d, f2 = win_bf.shape
    f = f2 // 2

    @pl.when((i == 0) | (e != e_prev))
    def _():
        def cast_in(r, carry):
            rows = pl.ds(pl.multiple_of(r * cast_rows, cast_rows), cast_rows)
            win_bf[rows, :] = win_ref[0, rows, :].astype(BF16)
            return carry

        def cast_out(r, carry):
            rows = pl.ds(pl.multiple_of(r * cast_rows, cast_rows), cast_rows)
            wout_bf[rows, :] = wout_ref[0, rows, :].astype(BF16)
            return carry

        lax.fori_loop(0, d // cast_rows, cast_in, 0)
        lax.fori_loop(0, f // cast_rows, cast_out, 0)

    @pl.when(i < nt_ref[0])
    def _():
        xs = xs_ref[:, 0, :]
        live = lax.broadcasted_iota(jnp.int32, xs.shape, 0) < tv_ref[i]
        x = jnp.where(live, xs, 0.0).astype(BF16)
        acc = None
        for c in range(f // f_chunk):
            lo = c * f_chunk
            zg = jnp.dot(x, win_bf[:, lo:lo + f_chunk], preferred_element_type=F32) + bin_ref[0, :, lo:lo + f_chunk]
            zl = (jnp.dot(x, win_bf[:, f + lo:f + lo + f_chunk], preferred_element_type=F32)
                  + bin_ref[0, :, f + lo:f + lo + f_chunk])
            glu = jnp.minimum(zg, SWIGLU_LIMIT)
            lin = jnp.clip(zl, -SWIGLU_LIMIT, SWIGLU_LIMIT)
            act = glu * jax.nn.sigmoid(SWIGLU_ALPHA * glu) * (lin + 1.0)
            part = jnp.dot(act.astype(BF16), wout_bf[lo:lo + f_chunk, :], preferred_element_type=F32)
            acc = part if acc is None else acc + part
        y_ref[:, 0, :] = acc + bout_ref[0]

    @pl.when(i >= nt_ref[0])
    def _():
        y_ref[...] = jnp.zeros(y_ref.shape, y_ref.dtype)


def _experts(tile_expert, tile_valid, n_tiles, xs, w_e_in, b_e_in, w_e_out, b_e_out, tm):
    rows, _, d = xs.shape
    n_e, _, f2 = w_e_in.shape
    f = f2 // 2
    nt_max = rows // tm
    row_map = lambda i, te, tv, nt: (jnp.minimum(i, nt[0] - 1), 0, 0)
    exp_map = lambda i, te, tv, nt: (te[i], 0, 0)
    vmem = 2 * (d * f2 + f * d) * 4 + (d * f2 + f * d) * 2 + 4 * tm * d * 2 + 10 * tm * d * 4
    grid_spec = pltpu.PrefetchScalarGridSpec(
        num_scalar_prefetch=3,
        grid=(nt_max,),
        in_specs=[pl.BlockSpec((tm, 1, d), row_map),
                  pl.BlockSpec((1, d, f2), exp_map), pl.BlockSpec((1, 1, f2), exp_map),
                  pl.BlockSpec((1, f, d), exp_map), pl.BlockSpec((1, 1, d), exp_map)],
        out_specs=pl.BlockSpec((tm, 1, d), lambda i, te, tv, nt: (i, 0, 0)),
        scratch_shapes=[pltpu.VMEM((d, f2), BF16), pltpu.VMEM((f, d), BF16)],
    )
    return pl.pallas_call(
        functools.partial(_experts_body, f_chunk=F_CHUNK, cast_rows=CAST_ROWS),
        grid_spec=grid_spec,
        out_shape=jax.ShapeDtypeStruct((rows, 1, d), F32),
        compiler_params=_cparams(1, vmem),
        name="experts",
    )(tile_expert, tile_valid, n_tiles, xs, w_e_in, b_e_in.reshape(n_e, 1, f2), w_e_out, b_e_out.reshape(n_e, 1, d))


def _final_body(*refs):
    pos_refs, posn_refs = refs[:TOP_K], refs[TOP_K:2 * TOP_K]
    x1_ref, tp_ref, gt_ref, g_ref, y_hbm, o_ref, buf, sem = refs[2 * TOP_K:]
    i = pl.program_id(0)
    n = pl.num_programs(0)
    tm = x1_ref.shape[0]

    def issue_group(p_refs, slot, base):
        for s in range(SUBLANES):
            for k in range(TOP_K):
                row = p_refs[k][base + s]
                pltpu.make_async_copy(y_hbm.at[row], buf.at[slot, k, pl.ds(base + s, 1), :],
                                      sem.at[slot]).start(priority=k % 2)

    def wait_slot(slot):
        for k in range(TOP_K):
            pltpu.make_async_copy(buf.at[slot, k], buf.at[slot, k], sem.at[slot]).wait()

    @pl.when(i == 0)
    def _():
        def first(j, carry):
            issue_group(pos_refs, 0, pl.multiple_of(j * SUBLANES, SUBLANES))
            return carry

        lax.fori_loop(0, tm // SUBLANES, first, 0)

    slot = i % 2
    nxt = (i + 1) % 2
    wait_slot(slot)

    def step(j, carry):
        n_rows = 4 * SUBLANES
        base = pl.multiple_of(j * n_rows, n_rows)
        for sub in range(4):
            issue_group(posn_refs, nxt, base + sub * SUBLANES)
        rows = pl.ds(base, n_rows)
        moe = None
        for k in range(TOP_K):
            term = tp_ref[rows, k:k + 1] * buf[slot, k, rows, :]
            moe = term if moe is None else moe + term
        x2 = x1_ref[rows, :] + gt_ref[0] * moe
        o_ref[rows, :] = x2 * lax.rsqrt(jnp.mean(x2 * x2, axis=-1, keepdims=True) + EPS) * g_ref[...]
        return carry

    lax.fori_loop(0, tm // (4 * SUBLANES), step, 0)

    @pl.when(i + 1 == n)
    def _():
        wait_slot(nxt)


def _final(pos_flat, x1, top_p, gt2, g_final, y_rows, tm):
    n_tok, d = x1.shape
    bsz = gt2.shape[0]
    n_blk = n_tok // tm
    per_b = n_blk // bsz
    return pl.pallas_call(
        _final_body,
        grid=(n_blk,),
        in_specs=_slot_specs(tm, n_blk, lambda i: i) + _slot_specs(tm, n_blk, lambda i: jnp.minimum(i + 1, n_blk - 1)) + [
                  pl.BlockSpec((tm, d), lambda i: (i, 0)),
                  pl.BlockSpec((tm, top_p.shape[-1]), lambda i: (i, 0)),
                  pl.BlockSpec((1, 1, d), lambda i: (i // per_b, 0, 0)),
                  pl.BlockSpec((1, d), lambda i: (0, 0)),
                  pl.BlockSpec(memory_space=pl.ANY)],
        out_specs=pl.BlockSpec((tm, d), lambda i: (i, 0)),
        out_shape=jax.ShapeDtypeStruct((n_tok, d), F32),
        scratch_shapes=[pltpu.VMEM((2, TOP_K, tm, d), F32), pltpu.SemaphoreType.DMA((2,))],
        compiler_params=_cparams(1, 2 * TOP_K * tm * d * 4 + 4 * tm * d * 4 + 8 * tm * d * 4),
        name="final",
    )(*([pos_flat] * (2 * TOP_K)), x1, top_p, gt2, g_final.reshape(1, d), y_rows)


def _tile_table(counts, tm, nt_max):
    tiles = (counts + tm - 1) // tm
    tile_end = jnp.cumsum(tiles)
    tile_start = tile_end - tiles
    n_tiles = tile_end[-1]
    tile_ids = jnp.arange(nt_max, dtype=jnp.int32)
    last_e = jnp.sum((n_tiles - 1) >= tile_end).astype(jnp.int32)
    te = jnp.sum(tile_ids[:, None] >= tile_end[None, :], axis=1).astype(jnp.int32)
    te = jnp.where(tile_ids < n_tiles, te, last_e)
    sel = te[:, None] == jnp.arange(counts.shape[0], dtype=jnp.int32)[None, :]
    cnt_t = jnp.sum(jnp.where(sel, counts[None, :], 0), axis=1)
    start_t = jnp.sum(jnp.where(sel, tile_start[None, :], 0), axis=1)
    live = jnp.clip(cnt_t - (tile_ids - start_t) * tm, 0, tm).astype(jnp.int32)
    live = jnp.where(tile_ids < n_tiles, live, 0)
    return (tile_start * tm).astype(jnp.int32), te, live, n_tiles.reshape(1).astype(jnp.int32)


def kernel(x, c, ctx, c_ctx, w_ada, b_ada, g_norm1, g_norm2, w_in, w_conv_qk, b_ifgate, g_mh, w_branch_m,
           s5_a_re, s5_a_im, s5_log_dt, s5_b_re, s5_b_im, s5_c_re, s5_c_im, s5_d, w_glu, b_glu, w_branch_s,
           b_merge_gate, w_o, w_router, b_router, w_e_in, b_e_in, w_e_out, b_e_out, g_final):
    bsz, length, d = x.shape
    l_ctx = ctx.shape[1]
    n_qk = 2 * N_HEADS * D_QK
    n_v = N_HEADS * D_V
    n_if = 4 * N_HEADS
    n_u = S5_GROUPS * S5_GROUP
    off_if = n_qk + n_v
    off_u = off_if + n_if
    off_o = off_u + n_u
    layer = 0

    pad_rows = -(bsz + 1) % 8
    c_rows = jnp.concatenate([c, c_ctx[None, :], jnp.zeros((pad_rows, d), F32)], axis=0)
    mod_all = _ada(c_rows, w_ada[layer], b_ada[layer])
    mod = mod_all[:bsz].reshape(bsz, 6, d)
    mod_c = mod_all[bsz, :2 * d].reshape(2, 1, 1, d)

    w_l = w_in[layer]
    w_state = jnp.concatenate([w_l[:, :off_if], w_l[:, off_u:off_o]], axis=1).astype(BF16)
    w_if_t = w_l[:, off_if:off_u].T.astype(BF16)
    b_if = b_ifgate[layer].reshape(n_if)
    proj_c = _proj(ctx, g_norm1[layer], mod_c[1], mod_c[0], w_state, w_if_t, b_if, n_qk, n_v, n_u, l_ctx)
    proj_l = _proj(x, g_norm1[layer], mod[:, 1:2], mod[:, 0:1], w_state, w_if_t, b_if, n_qk, n_v, n_u,
                   min(TM_PROJ, length))
    qk_c, v_c, u_c, gif_c = proj_c
    qk_l, v_l, u_l, gif_l = proj_l

    w9 = w_conv_qk[layer].reshape(9, n_qk)
    q_c, kt_c = _conv(qk_c, w9, D_QK ** -0.5, l_ctx, l_ctx)
    q_l, kt_l = _conv(qk_l, w9, D_QK ** -0.5, GRID_W, min(TM_CONV, length))

    st0 = jnp.zeros((bsz, 2 * N_HEADS, D_QK, 2 * D_V), F32)
    m0 = jnp.zeros((bsz, 2 * N_HEADS, 1), F32)
    _, _, st_c, m_c = _mlstm(q_c, kt_c, v_c, gif_c, st0, m0)
    h_f, h_r, _, _ = _mlstm(q_l, kt_l, v_l, gif_l, st_c, m_c)

    w_in_t, m_tz_t, c_out_t, mu_a, mu_b = _s5_tables(
        s5_a_re[layer], s5_a_im[layer], s5_log_dt[layer], s5_b_re[layer], s5_b_im[layer],
        s5_c_re[layer], s5_c_im[layer], s5_d[layer])
    hs0 = jnp.zeros((bsz, S5_GROUPS, 4 * S5_STATE), F32)
    hs_c, = _s5(u_c, hs0, w_in_t, m_tz_t, c_out_t, mu_a, mu_b)
    y_b, _ = _s5(u_l, hs_c, w_in_t, m_tz_t, c_out_t, mu_a, mu_b,
                 w_glu[layer].T.astype(BF16), b_glu[layer], w_branch_s[layer].T.astype(BF16))

    w_og = w_l[:, off_o:].astype(BF16)
    x1, h2, top_i, top_p, rank, counts = _mixout(
        x, h_f, h_r, y_b, mod, g_norm1[layer], g_norm2[layer], g_mh[layer], w_og, w_branch_m[layer].astype(BF16),
        b_merge_gate[layer], w_o[layer].astype(BF16), w_router[layer].astype(BF16), b_router[layer],
        min(TM_MIX, length))

    tm_e = TM_EXPERT
    n_tok = bsz * length
    nt_max = n_tok * TOP_K // tm_e + N_EXPERTS
    row_start, tile_expert, tile_live, n_tiles = _tile_table(counts.reshape(N_EXPERTS), tm_e, nt_max)
    pos = _rowpos(row_start, top_i, rank, min(TN_ROWPOS, n_tok))
    pos_flat = pos[:TOP_K].reshape(TOP_K * n_tok)
    xs = _dispatch(pos_flat, h2.reshape(n_tok, d), nt_max * tm_e, min(TM_DISPATCH, n_tok))
    y_rows = _experts(tile_expert, tile_live, n_tiles, xs, w_e_in[layer], b_e_in[layer], w_e_out[layer],
                      b_e_out[layer], tm_e)
    out = _final(pos_flat, x1.reshape(n_tok, d), top_p.reshape(n_tok, 8), mod[:, 5:6], g_final, y_rows,
                 min(TM_FINAL, n_tok))
    return out.reshape(bsz, length, d)
```
